```python
import math
import jax, jax.numpy as jnp
from jax import lax
import numpy as np

D_MODEL = 1024
BATCH = 8
SEQ = 4096
DEPTH = 2

HYENA_WIDTH = D_MODEL // 2
HYENA_ORDER = 2
SHORT_CONV = 3
FILTER_EMB = 33
FILTER_HIDDEN = 64
HYENA_TARGET = 1e-2
FAST_DECAY_PCT = 0.3
SLOW_DECAY_PCT = 1.5
N_DIFF_HEADS = 4
DIFF_HEAD_DIM = 64
DIFF_V_DIM = 2 * DIFF_HEAD_DIM
QK_WIDTH = N_DIFF_HEADS * 2 * DIFF_HEAD_DIM
ATTN_WIDTH = N_DIFF_HEADS * DIFF_V_DIM
ROPE_THETA = 10000.0
Q_BLOCK = 128
N_BRANCHES = 2
IN_WIDTH = 3 * HYENA_WIDTH + 2 * QK_WIDTH + ATTN_WIDTH + N_BRANCHES * D_MODEL
D_FF = 2752
N_EXPERTS = 8
TOP_K = 2
MOE_BLOCK = 128
N_DENSE = (DEPTH + 1) // 2
N_MOE = DEPTH // 2
RMS_EPS = 1e-6

kernel_name = "hybrid_hyena_diffattn_moe_encoder"

F32 = jnp.float32


def rms_norm(x, g):
    xf = x.astype(F32)
    y = xf * lax.rsqrt(jnp.mean(xf * xf, axis=-1, keepdims=True) + RMS_EPS)
    return (y * g.astype(F32)).astype(x.dtype)


def rope_tables(L, dim):
    inv = 1.0 / (ROPE_THETA ** (jnp.arange(0, dim, 2, dtype=F32) / dim))
    ang = jnp.arange(L, dtype=F32)[:, None] * inv[None, :]
    return jnp.cos(ang), jnp.sin(ang)


def apply_rope(x, cos, sin):
    xf = x.astype(F32)
    x1, x2 = jnp.split(xf, 2, axis=-1)
    c = cos[None, :, None, None, :]
    s = sin[None, :, None, None, :]
    return jnp.concatenate([x1 * c - x2 * s, x1 * s + x2 * c], axis=-1).astype(x.dtype)


def centred_short_conv(u, w):
    L = u.shape[1]
    pad = SHORT_CONV // 2
    up = jnp.pad(u, ((0, 0), (pad, SHORT_CONV - 1 - pad), (0, 0)))
    out = up[:, 0:L] * w[0]
    for i in range(1, SHORT_CONV):
        out = out + up[:, i:i + L] * w[i]
    return out


def hyena_filters(L, w_f1, b_f1, freq1, w_f2, b_f2, freq2, w_f3):
    bands = (FILTER_EMB - 1) // 2
    t = jnp.linspace(0.0, 1.0, L, dtype=F32)[:, None]
    w = 2.0 * math.pi * jnp.arange(L, dtype=F32)[:, None] / L
    f = jnp.linspace(1e-4, bands - 1, bands, dtype=F32)[None, :]
    z = jnp.concatenate([t, jnp.cos(f * w), -jnp.sin(f * w)], axis=-1)
    hdn = jnp.sin(freq1.astype(F32) * (z @ w_f1.astype(F32) + b_f1.astype(F32)))
    hdn = jnp.sin(freq2.astype(F32) * (hdn @ w_f2.astype(F32) + b_f2.astype(F32)))
    hf = hdn @ w_f3.astype(F32)
    max_decay = math.log(HYENA_TARGET) / FAST_DECAY_PCT
    min_decay = math.log(HYENA_TARGET) / SLOW_DECAY_PCT
    deltas = jnp.linspace(min_decay, max_decay, HYENA_WIDTH, dtype=F32)
    decay = jnp.exp(-t * jnp.abs(deltas)[None, :])
    hf = hf.reshape(L, HYENA_ORDER, 2, HYENA_WIDTH) * decay[:, None, None, :]
    fwd = hf[:, :, 0]
    bwd = hf[:, :, 1]
    kern = jnp.concatenate([fwd, jnp.zeros_like(fwd[:1]), bwd[1:][::-1]], axis=0)
    return jnp.fft.rfft(kern, axis=0)


def bidir_long_conv(z, kf, bias):
    L = z.shape[1]
    zf32 = z.astype(F32)
    zf = jnp.fft.rfft(zf32, n=2 * L, axis=1)
    y = jnp.fft.irfft(zf * kf[None], n=2 * L, axis=1)[:, :L]
    return (y + zf32 * bias.astype(F32)).astype(z.dtype)


def hyena_mixer(u, conv_w, filt, hy_bias):
    u = centred_short_conv(u, conv_w)
    x1, x2, v = jnp.split(u, 3, axis=-1)
    z = x1 * bidir_long_conv(v, filt[:, 0], hy_bias[0])
    return x2 * bidir_long_conv(z, filt[:, 1], hy_bias[1])


def diff_attention(q, k, v, lam, lambda_init, subln_g):
    B, L = q.shape[0], q.shape[1]
    nb = L // Q_BLOCK
    qb = q.reshape(B, nb, Q_BLOCK, N_DIFF_HEADS, 2, DIFF_HEAD_DIM).transpose(1, 0, 2, 3, 4, 5)
    scale = DIFF_HEAD_DIM ** -0.5

    def block(qblk):
        s = jnp.einsum('bqhcd,bkhcd->bhcqk', qblk, k, preferred_element_type=F32) * scale
        p = jax.nn.softmax(s, axis=-1)
        a = p[:, :, 0] - lam * p[:, :, 1]
        return jnp.einsum('bhqk,bkhe->bqhe', a.astype(v.dtype), v)

    o = lax.map(block, qb)
    o = o.transpose(1, 0, 2, 3, 4).reshape(B, L, N_DIFF_HEADS, DIFF_V_DIM)
    o = rms_norm(o, subln_g) * (1.0 - lambda_init)
    return o.reshape(B, L, ATTN_WIDTH)


def hybrid_mixer(x, norm_g, w_in, conv_w, w_f1, b_f1, freq1, w_f2, b_f2, freq2, w_f3, hy_bias,
                 q_norm, k_norm, lam_q1, lam_k1, lam_q2, lam_k2, subln_g, w_hy_o, w_at_o, w_out,
                 lambda_init, cos, sin):
    B, L, _ = x.shape
    h = rms_norm(x, norm_g)
    proj = h @ w_in
    s1 = 3 * HYENA_WIDTH
    s2 = s1 + QK_WIDTH
    s3 = s2 + QK_WIDTH
    s4 = s3 + ATTN_WIDTH
    hy_in = proj[..., :s1]
    q = proj[..., s1:s2].reshape(B, L, N_DIFF_HEADS, 2, DIFF_HEAD_DIM)
    k = proj[..., s2:s3].reshape(B, L, N_DIFF_HEADS, 2, DIFF_HEAD_DIM)
    v = proj[..., s3:s4].reshape(B, L, N_DIFF_HEADS, DIFF_V_DIM)
    gates = jax.nn.sigmoid(proj[..., s4:].astype(F32)).reshape(B, L, N_BRANCHES, D_MODEL)
    filt = hyena_filters(L, w_f1, b_f1, freq1, w_f2, b_f2, freq2, w_f3)
    y_h = hyena_mixer(hy_in, conv_w, filt, hy_bias)
    q = apply_rope(rms_norm(q, q_norm), cos, sin)
    k = apply_rope(rms_norm(k, k_norm), cos, sin)
    lam = (jnp.exp(jnp.sum(lam_q1.astype(F32) * lam_k1.astype(F32)))
           - jnp.exp(jnp.sum(lam_q2.astype(F32) * lam_k2.astype(F32))) + lambda_init)
    y_a = diff_attention(q, k, v, lam, lambda_init, subln_g)
    merged = (gates[..., 0, :] * (y_h @ w_hy_o).astype(F32)
              + gates[..., 1, :] * (y_a @ w_at_o).astype(F32))
    return merged.astype(x.dtype) @ w_out


def swiglu(x, w_gate, w_up, w_down):
    return (jax.nn.silu(x @ w_gate) * (x @ w_up)) @ w_down


def moe_ffn(x, w_router, w_gate, w_up, w_down):
    B, L, D = x.shape
    T = B * L
    xt = x.reshape(T, D)
    logits = (xt @ w_router).astype(F32)
    top_val, top_idx = lax.top_k(logits, TOP_K)
    top_w = jax.nn.softmax(top_val, axis=-1)
    A = T * TOP_K
    e_flat = top_idx.reshape(A).astype(jnp.int32)
    tok = jnp.arange(A, dtype=jnp.int32) // TOP_K
    order = jnp.argsort(e_flat)
    e_s = e_flat[order]
    tok_s = tok[order]
    w_s = top_w.reshape(A)[order]
    counts = jnp.bincount(e_flat, length=N_EXPERTS).astype(jnp.int32)
    starts = jnp.cumsum(counts) - counts
    padded = (counts + MOE_BLOCK - 1) // MOE_BLOCK * MOE_BLOCK
    pad_end = jnp.cumsum(padded)
    pad_start = pad_end - padded
    dest = pad_start[e_s] + (jnp.arange(A, dtype=jnp.int32) - starts[e_s])
    n_blocks = -(-A // MOE_BLOCK) + N_EXPERTS
    P = n_blocks * MOE_BLOCK
    xp = jnp.zeros((P, D), x.dtype).at[dest].set(xt[tok_s])
    block_start = jnp.arange(n_blocks, dtype=jnp.int32) * MOE_BLOCK
    block_e = jnp.minimum(jnp.searchsorted(pad_end, block_start, side='right'), N_EXPERTS - 1)

    def run(args):
        xb, e = args
        return swiglu(xb, w_gate[e], w_up[e], w_down[e])

    yp = lax.map(run, (xp.reshape(n_blocks, MOE_BLOCK, D), block_e)).reshape(P, D)
    contrib = yp[dest] * w_s[:, None].astype(x.dtype)
    out = jnp.zeros((T, D), x.dtype).at[tok_s].add(contrib)
    return out.reshape(B, L, D)


def setup_inputs(seed: int = 0) -> dict:
    key = jax.random.key(seed)
    ks = jax.random.split(key, 32)
    n = lambda k, shape, s: jax.random.normal(k, shape, F32) * s
    gain = lambda k, shape: 1.0 + 0.02 * jax.random.normal(k, shape, F32)
    HW = HYENA_WIDTH
    return {
        'x': jax.random.normal(ks[0], (BATCH, SEQ, D_MODEL), F32),
        'norm_mix': gain(ks[1], (DEPTH, D_MODEL)),
        'w_in': n(ks[2], (DEPTH, D_MODEL, IN_WIDTH), D_MODEL ** -0.5),
        'conv_w': n(ks[3], (DEPTH, SHORT_CONV, 3 * HW), SHORT_CONV ** -0.5),
        'w_f1': n(ks[4], (DEPTH, FILTER_EMB, FILTER_HIDDEN), FILTER_EMB ** -0.5),
        'b_f1': n(ks[5], (DEPTH, FILTER_HIDDEN), 0.1),
        'freq1': gain(ks[6], (DEPTH, FILTER_HIDDEN)),
        'w_f2': n(ks[7], (DEPTH, FILTER_HIDDEN, FILTER_HIDDEN), FILTER_HIDDEN ** -0.5),
        'b_f2': n(ks[8], (DEPTH, FILTER_HIDDEN), 0.1),
        'freq2': gain(ks[9], (DEPTH, FILTER_HIDDEN)),
        'w_f3': n(ks[10], (DEPTH, FILTER_HIDDEN, HYENA_ORDER * 2 * HW), (FILTER_HIDDEN * SEQ / 8.0) ** -0.5),
        'hy_bias': n(ks[11], (DEPTH, HYENA_ORDER, HW), 0.3),
        'q_norm': gain(ks[12], (DEPTH, DIFF_HEAD_DIM)),
        'k_norm': gain(ks[13], (DEPTH, DIFF_HEAD_DIM)),
        'lam_q1': n(ks[14], (DEPTH, DIFF_HEAD_DIM), 0.1),
        'lam_k1': n(ks[15], (DEPTH, DIFF_HEAD_DIM), 0.1),
        'lam_q2': n(ks[16], (DEPTH, DIFF_HEAD_DIM), 0.1),
        'lam_k2': n(ks[17], (DEPTH, DIFF_HEAD_DIM), 0.1),
        'subln_g': gain(ks[18], (DEPTH, DIFF_V_DIM)),
        'w_hy_o': n(ks[19], (DEPTH, HW, D_MODEL), HW ** -0.5),
        'w_at_o': n(ks[20], (DEPTH, ATTN_WIDTH, D_MODEL), ATTN_WIDTH ** -0.5),
        'w_out': n(ks[21], (DEPTH, D_MODEL, D_MODEL), D_MODEL ** -0.5),
        'norm_ffn': gain(ks[22], (DEPTH, D_MODEL)),
        'w_dense_gate': n(ks[23], (N_DENSE, D_MODEL, D_FF), D_MODEL ** -0.5),
        'w_dense_up': n(ks[24], (N_DENSE, D_MODEL, D_FF), D_MODEL ** -0.5),
        'w_dense_down': n(ks[25], (N_DENSE, D_FF, D_MODEL), D_FF ** -0.5),
        'w_router': n(ks[26], (N_MOE, D_MODEL, N_EXPERTS), D_MODEL ** -0.5),
        'w_moe_gate': n(ks[27], (N_MOE, N_EXPERTS, D_MODEL, D_FF), D_MODEL ** -0.5),
        'w_moe_up': n(ks[28], (N_MOE, N_EXPERTS, D_MODEL, D_FF), D_MODEL ** -0.5),
        'w_moe_down': n(ks[29], (N_MOE, N_EXPERTS, D_FF, D_MODEL), D_FF ** -0.5),
    }


def reference(x, norm_mix, w_in, conv_w, w_f1, b_f1, freq1, w_f2, b_f2, freq2, w_f3, hy_bias,
              q_norm, k_norm, lam_q1, lam_k1, lam_q2, lam_k2, subln_g, w_hy_o, w_at_o, w_out,
              norm_ffn, w_dense_gate, w_dense_up, w_dense_down, w_router, w_moe_gate, w_moe_up,
              w_moe_down):
    L = x.shape[1]
    cos, sin = rope_tables(L, DIFF_HEAD_DIM)
    for layer in range(DEPTH):
        lambda_init = 0.8 - 0.6 * math.exp(-0.3 * layer)
        x = x + hybrid_mixer(
            x, norm_mix[layer], w_in[layer], conv_w[layer], w_f1[layer], b_f1[layer], freq1[layer],
            w_f2[layer], b_f2[layer], freq2[layer], w_f3[layer], hy_bias[layer], q_norm[layer],
            k_norm[layer], lam_q1[layer], lam_k1[layer], lam_q2[layer], lam_k2[layer], subln_g[layer],
            w_hy_o[layer], w_at_o[layer], w_out[layer], lambda_init, cos, sin)
        h = rms_norm(x, norm_ffn[layer])
        i = layer // 2
        if layer % 2 == 0:
            x = x + swiglu(h, w_dense_gate[i], w_dense_up[i], w_dense_down[i])
        else:
            x = x + moe_ffn(h, w_router[i], w_moe_gate[i], w_moe_up[i], w_moe_down[i])
    return x
```

```python
import functools
import math

import numpy as np
import jax
import jax.numpy as jnp
from jax import lax
from jax.experimental import pallas as pl
from jax.experimental.pallas import tpu as pltpu

F32 = jnp.float32
BF16 = jnp.bfloat16
HIGHEST = lax.Precision.HIGHEST

N_HEADS = 4
HEAD_DIM = 64
V_DIM = 2 * HEAD_DIM
ROPE_THETA = 10000.0
FILTER_BANDS = 16
HYENA_TARGET = 1e-2
FAST_DECAY_PCT = 0.3
SLOW_DECAY_PCT = 1.5
TOP_K = 2
RMS_EPS = 1e-6

LANES = 128
BF16_SUBLANES = 16
VMEM_LIMIT = 56 * 1024 * 1024

DFT_N1 = 128
MOE_ROWS = 256


def _cparams(sem):
    return pltpu.CompilerParams(dimension_semantics=sem, vmem_limit_bytes=VMEM_LIMIT)


def _round_up(a, b):
    return (a + b - 1) // b * b


def _inproj_kernel(x_ref, g_ref, w_ref, o_ref, hn_ref):
    @pl.when(pl.program_id(1) == 0)
    def _():
        x = x_ref[...]
        ms = jnp.mean(x * x, axis=-1, keepdims=True)
        hn_ref[...] = (x * lax.rsqrt(ms + RMS_EPS) * g_ref[...]).astype(BF16)

    o_ref[...] = jnp.dot(hn_ref[...], w_ref[...], preferred_element_type=F32).astype(o_ref.dtype)


def _inproj(x2, g, w):
    T, D = x2.shape
    N = w.shape[1]
    tm = min(1024, T)
    tn = 1024
    return pl.pallas_call(
        _inproj_kernel,
        grid=(T // tm, N // tn),
        in_specs=[pl.BlockSpec((tm, D), lambda i, j: (i, 0)),
                  pl.BlockSpec((1, D), lambda i, j: (0, 0)),
                  pl.BlockSpec((D, tn), lambda i, j: (0, j))],
        out_specs=pl.BlockSpec((tm, tn), lambda i, j: (i, j)),
        out_shape=jax.ShapeDtypeStruct((T, N), BF16),
        scratch_shapes=[pltpu.VMEM((tm, D), BF16)],
        compiler_params=_cparams(("parallel", "arbitrary")),
        name="inproj",
    )(x2, g.reshape(1, D), w)


def _shortconv_kernel(u_ref, p_ref, n_ref, w_ref, o_ref):
    i = pl.program_id(1)
    last = pl.num_programs(1) - 1
    u = u_ref[0].astype(F32)
    tl = u.shape[0]
    prev_row = p_ref[0].astype(F32)[BF16_SUBLANES - 1:BF16_SUBLANES]
    next_row = n_ref[0].astype(F32)[0:1]
    prev_row = jnp.where(i > 0, prev_row, 0.0)
    next_row = jnp.where(i < last, next_row, 0.0)
    row = lax.broadcasted_iota(jnp.int32, u.shape, 0)
    up = jnp.where(row == 0, prev_row, pltpu.roll(u, 1, 0))
    un = jnp.where(row == tl - 1, next_row, pltpu.roll(u, tl - 1, 0))
    w = w_ref[...]
    o_ref[0, 0] = (up * w[0:1] + u * w[1:2] + un * w[2:3]).astype(o_ref.dtype)


def _shortconv(proj3, conv_w, hw):
    B, L, _ = proj3.shape
    tl = min(1024, L)
    hb = tl // BF16_SUBLANES
    nhb = L // BF16_SUBLANES
    return pl.pallas_call(
        _shortconv_kernel,
        grid=(B, L // tl, 3),
        in_specs=[pl.BlockSpec((1, tl, hw), lambda b, i, c: (b, i, c)),
                  pl.BlockSpec((1, BF16_SUBLANES, hw),
                               lambda b, i, c: (b, jnp.maximum(i * hb - 1, 0), c)),
                  pl.BlockSpec((1, BF16_SUBLANES, hw),
                               lambda b, i, c: (b, jnp.minimum((i + 1) * hb, nhb - 1), c)),
                  pl.BlockSpec((3, hw), lambda b, i, c: (0, c))],
        out_specs=pl.BlockSpec((1, 1, tl, hw), lambda b, i, c: (c, b, i, 0)),
        out_shape=jax.ShapeDtypeStruct((3, B, L, hw), BF16),
        compiler_params=_cparams(("parallel", "parallel", "arbitrary")),
        name="shortconv",
    )(proj3, proj3, proj3, conv_w)


def _filter_kernel(zz_ref, w1_ref, b1_ref, f1_ref, w2_ref, b2_ref, f2_ref, w3_ref, dl_ref, o_ref):
    zz = zz_ref[...]
    h = jnp.dot(zz, w1_ref[...], precision=HIGHEST, preferred_element_type=F32)
    h = jnp.sin(f1_ref[...] * (h + b1_ref[...]))
    h = jnp.dot(h, w2_ref[...], precision=HIGHEST, preferred_element_type=F32)
    h = jnp.sin(f2_ref[...] * (h + b2_ref[...]))
    hf = jnp.dot(h, w3_ref[0], precision=HIGHEST, preferred_element_type=F32)
    t = zz[:, 0:1]
    mask = zz[:, _MASK_COL:_MASK_COL + 1]
    o_ref[...] = hf * (jnp.exp(-t * dl_ref[...]) * mask)


_MASK_COL = 2 * FILTER_BANDS + 1


def _filter_positions(L):
    bands = FILTER_BANDS
    t = np.linspace(0.0, 1.0, L, dtype=np.float64)[:, None]
    w = 2.0 * math.pi * np.arange(L, dtype=np.float64)[:, None] / L
    f = np.linspace(1e-4, bands - 1, bands, dtype=np.float64)[None, :]
    z = np.concatenate([t, np.cos(f * w), -np.sin(f * w)], axis=-1)
    zz = np.zeros((2 * L, LANES), np.float64)
    zz[:L, :z.shape[1]] = z
    zz[L + 1:, :z.shape[1]] = z[1:][::-1]
    zz[:, _MASK_COL] = 1.0
    zz[L, :] = 0.0
    return zz.astype(np.float32)


def _hyena_time_filters(L, w_f1, b_f1, freq1, w_f2, b_f2, freq2, w_f3, hw):
    emb, hid = w_f1.shape
    zz = jnp.asarray(_filter_positions(L))
    w1 = jnp.zeros((LANES, hid), F32).at[:emb].set(w_f1)
    w3 = w_f3.reshape(hid, 2, 2, hw).transpose(2, 0, 1, 3).reshape(2, hid, 2 * hw)
    max_decay = math.log(HYENA_TARGET) / FAST_DECAY_PCT
    min_decay = math.log(HYENA_TARGET) / SLOW_DECAY_PCT
    deltas = np.abs(np.linspace(min_decay, max_decay, hw, dtype=np.float64))
    dl = jnp.asarray(np.tile(deltas, 2)[None, :], F32)
    tr = min(512, L)
    per_dir = L // tr
    row = lambda a: a.reshape(1, -1)
    return pl.pallas_call(
        _filter_kernel,
        grid=(2 * L // tr,),
        in_specs=[pl.BlockSpec((tr, LANES), lambda i: (i, 0)),
                  pl.BlockSpec((LANES, hid), lambda i: (0, 0)),
                  pl.BlockSpec((1, hid), lambda i: (0, 0)),
                  pl.BlockSpec((1, hid), lambda i: (0, 0)),
                  pl.BlockSpec((hid, hid), lambda i: (0, 0)),
                  pl.BlockSpec((1, hid), lambda i: (0, 0)),
                  pl.BlockSpec((1, hid), lambda i: (0, 0)),
                  pl.BlockSpec((1, hid, 2 * hw), lambda i: (i // per_dir, 0, 0)),
                  pl.BlockSpec((1, 2 * hw), lambda i: (0, 0))],
        out_specs=pl.BlockSpec((tr, 2 * hw), lambda i: (i, 0)),
        out_shape=jax.ShapeDtypeStruct((2 * L, 2 * hw), F32),
        compiler_params=_cparams(("parallel",)),
        name="hyena_filter",
    )(zz, w1, row(b_f1), row(freq1), w_f2, row(b_f2), row(freq2), w3, dl)


def _dft_matrices(L):
    N = 2 * L
    N1 = DFT_N1
    N2 = N // N1
    h = N2 // 2
    ang = 2.0 * np.pi * np.outer(np.arange(N2), np.arange(N2)) / N2
    c, s = np.cos(ang), np.sin(ang)
    m1 = np.zeros((N2, 2, N2))
    m1[:, 0, :h], m1[:, 0, h:] = c[:, :h], s[:, :h]
    m1[:, 1, :h], m1[:, 1, h:] = -s[:, :h], c[:, :h]
    m1 = m1.reshape(2 * N2, N2)
    m1f = np.stack([c, -s], axis=1).reshape(2 * N2, N2)
    m3 = np.zeros((2, h, N2, 2))
    m3[0, :, :, 0], m3[0, :, :, 1] = c[:h], -s[:h]
    m3[1, :, :, 0], m3[1, :, :, 1] = s[:h], c[:h]
    m3 = (m3 / N).reshape(N2, 2 * N2)
    k2 = jnp.arange(N2, dtype=jnp.int32)[:, None, None]
    k1 = jnp.arange(N1, dtype=jnp.int32)[None, :, None]
    n1 = jnp.arange(N1, dtype=jnp.int32)[None, None, :]
    m = (n1 * (N2 * k1 + k2)) % N
    th = m.astype(F32) * F32(2.0 * math.pi / N)
    C, S = jnp.cos(th), jnp.sin(th)
    m2 = jnp.concatenate([jnp.concatenate([C, S], axis=2),
                          jnp.concatenate([-S, C], axis=2)], axis=1)
    return (jnp.asarray(m1, F32), jnp.asarray(m1f, F32), m2, jnp.asarray(m3, F32))


def _dft_a_kernel(m_ref, z_ref, o_ref, *, precision):
    o_ref[0] = jnp.dot(m_ref[...], z_ref[0], precision=precision,
                       preferred_element_type=F32).astype(o_ref.dtype)


def _dft_a(m, z, out_dtype, precision=None):
    P, K, LN = z.shape
    R = m.shape[0]
    lt = min(8192, LN)
    return pl.pallas_call(
        functools.partial(_dft_a_kernel, precision=precision),
        grid=(P, LN // lt),
        in_specs=[pl.BlockSpec((R, K), lambda p, j: (0, 0)),
                  pl.BlockSpec((1, K, lt), lambda p, j: (p, 0, j))],
        out_specs=pl.BlockSpec((1, R, lt), lambda p, j: (p, 0, j)),
        out_shape=jax.ShapeDtypeStruct((P, R, LN), out_dtype),
        compiler_params=_cparams(("parallel", "parallel")),
        name="dft_a",
    )(m, z)


def _dft_filt_kernel(a_ref, m_ref, o_ref):
    o_ref[0] = jnp.dot(m_ref[0], a_ref[0], precision=HIGHEST, preferred_element_type=F32)


def _dft_filt(a, m2):
    N2, R, W = a.shape
    return pl.pallas_call(
        _dft_filt_kernel,
        grid=(N2,),
        in_specs=[pl.BlockSpec((1, R, W), lambda k: (k, 0, 0)),
                  pl.BlockSpec((1, R, R), lambda k: (k, 0, 0))],
        out_specs=pl.BlockSpec((1, R, W), lambda k: (k, 0, 0)),
        out_shape=jax.ShapeDtypeStruct((N2, R, W), F32),
        compiler_params=_cparams(("parallel",)),
        name="dft_filter_mid",
    )(a, m2)


def _dft_mid_kernel(a_ref, m_ref, mt_ref, kf_ref, o_ref):
    m = m_ref[0]
    mt = mt_ref[0]
    n1 = m.shape[0] // 2
    kr = kf_ref[0, :n1]
    ki = kf_ref[0, n1:]
    for p in range(a_ref.shape[0]):
        x = jnp.dot(m, a_ref[p, 0], preferred_element_type=F32)
        xr = x[:n1]
        xi = x[n1:]
        y = jnp.concatenate([xr * kr - xi * ki, xr * ki + xi * kr], axis=0).astype(BF16)
        o_ref[p, 0] = jnp.dot(mt, y, preferred_element_type=F32).astype(o_ref.dtype)


def _dft_mid(a, m2, m2t, kf, order):
    P, N2, R, C = a.shape
    return pl.pallas_call(
        _dft_mid_kernel,
        grid=(N2,),
        in_specs=[pl.BlockSpec((P, 1, R, C), lambda k: (0, k, 0, 0)),
                  pl.BlockSpec((1, R, R), lambda k: (k, 0, 0)),
                  pl.BlockSpec((1, R, R), lambda k: (k, 0, 0)),
                  pl.BlockSpec((1, R, C), lambda k: (k, 0, order))],
        out_specs=pl.BlockSpec((P, 1, R, C), lambda k: (0, k, 0, 0)),
        out_shape=jax.ShapeDtypeStruct((P, N2, R, C), BF16),
        compiler_params=_cparams(("parallel",)),
        name="dft_mid",
    )(a, m2, m2t, kf)


def _dft_c_kernel(m_ref, b_ref, z_ref, g_ref, bias_ref, o_ref):
    y = jnp.dot(m_ref[...], b_ref[0], preferred_element_type=F32)
    z = z_ref[0].astype(F32)
    g = g_ref[0].astype(F32)
    o_ref[0] = (g * (y + z * bias_ref[...])).astype(o_ref.dtype)


def _dft_c(m3, b, z, g, bias_t):
    P, R, LN = b.shape
    K = z.shape[1]
    lt = bias_t.shape[1]
    return pl.pallas_call(
        _dft_c_kernel,
        grid=(P, LN // lt),
        in_specs=[pl.BlockSpec((K, R), lambda p, j: (0, 0)),
                  pl.BlockSpec((1, R, lt), lambda p, j: (p, 0, j)),
                  pl.BlockSpec((1, K, lt), lambda p, j: (p, 0, j)),
                  pl.BlockSpec((1, K, lt), lambda p, j: (p, 0, j)),
                  pl.BlockSpec((1, lt), lambda p, j: (0, 0))],
        out_specs=pl.BlockSpec((1, K, lt), lambda p, j: (p, 0, j)),
        out_shape=jax.ShapeDtypeStruct((P, K, LN), BF16),
        compiler_params=_cparams(("parallel", "parallel")),
        name="dft_c",
    )(m3, b, z, g, bias_t)


def _hyena(u3, kern, hy_bias, mats):
    _, B, L, C = u3.shape
    m1, m1f, m2, m3 = mats
    N1 = DFT_N1
    N2 = 2 * L // N1
    P = B // 2
    LN = N1 * C
    af = _dft_a(m1f, kern.reshape(1, N2, N1 * 2 * C), F32, precision=HIGHEST)
    kf = _dft_filt(af.reshape(N2, 2 * N1, 2 * C), m2)
    m1b, m3b, m2b = m1.astype(BF16), m3.astype(BF16), m2.astype(BF16)
    m2tb = jnp.swapaxes(m2b, 1, 2)
    lt = min(8192, LN)
    x1, x2, v = (u3[i].reshape(P, N2, LN) for i in range(3))

    def long_conv(z, gate, order):
        a = _dft_a(m1b, z, BF16)
        b = _dft_mid(a.reshape(P, N2, 2 * N1, C), m2b, m2tb, kf, order)
        bias_t = jnp.tile(hy_bias[order].astype(F32), lt // C)[None, :]
        return _dft_c(m3b, b.reshape(P, 2 * N2, LN), z, gate, bias_t)

    z = long_conv(v, x1, 0)
    y = long_conv(z, x2, 1)
    return y.reshape(B, L, C)


def _qk_kernel(q_ref, k_ref, cos_ref, sin_ref, qg_ref, kg_ref, qo_ref, ko_ref, *, scale):
    cos = cos_ref[...]
    sin = sin_ref[...]
    lane = lax.broadcasted_iota(jnp.int32, cos.shape, 1)
    lo = lane < HEAD_DIM
    first = (lane % HEAD_DIM) < (HEAD_DIM // 2)

    def one(x_ref, g_ref, o_ref, mult):
        for h in range(x_ref.shape[2] // LANES):
            x = x_ref[0, :, h * LANES:(h + 1) * LANES].astype(F32)
            x2 = x * x
            s_lo = jnp.sum(jnp.where(lo, x2, 0.0), axis=-1, keepdims=True)
            s_hi = jnp.sum(jnp.where(lo, 0.0, x2), axis=-1, keepdims=True)
            r = lax.rsqrt(jnp.where(lo, s_lo, s_hi) * (1.0 / HEAD_DIM) + RMS_EPS)
            xn = x * r * g_ref[...]
            partner = jnp.where(first, pltpu.roll(xn, LANES - HEAD_DIM // 2, 1),
                                pltpu.roll(xn, HEAD_DIM // 2, 1))
            o_ref[0, :, h * LANES:(h + 1) * LANES] = ((xn * cos + partner * sin) * mult).astype(o_ref.dtype)

    one(q_ref, qg_ref, qo_ref, scale)
    one(k_ref, kg_ref, ko_ref, 1.0)


def _rope_tables(L):
    half = HEAD_DIM // 2
    inv = 1.0 / (ROPE_THETA ** (np.arange(0, HEAD_DIM, 2, dtype=np.float64) / HEAD_DIM))
    ang = np.arange(L, dtype=np.float64)[:, None] * inv[None, :]
    cos, sin = np.cos(ang), np.sin(ang)
    cos_t = np.tile(cos, (1, LANES // half))
    sin_t = np.tile(np.concatenate([-sin, sin], axis=1), (1, LANES // HEAD_DIM))
    return jnp.asarray(cos_t, F32), jnp.asarray(sin_t, F32)


def _qk_prep(proj3, q_norm, k_norm, qcol, kcol, width):
    B, L, _ = proj3.shape
    tl = min(512, L)
    cos_t, sin_t = _rope_tables(L)
    qg = jnp.tile(q_norm.astype(F32), LANES // HEAD_DIM)[None, :]
    kg = jnp.tile(k_norm.astype(F32), LANES // HEAD_DIM)[None, :]
    blk = lambda col: pl.BlockSpec((1, tl, width), lambda b, i: (b, i, col))
    tab = pl.BlockSpec((tl, LANES), lambda b, i: (i, 0))
    vec = pl.BlockSpec((1, LANES), lambda b, i: (0, 0))
    out = pl.BlockSpec((1, tl, width), lambda b, i: (b, i, 0))
    return pl.pallas_call(
        functools.partial(_qk_kernel, scale=HEAD_DIM ** -0.5),
        grid=(B, L // tl),
        in_specs=[blk(qcol), blk(kcol), tab, tab, vec, vec],
        out_specs=[out, out],
        out_shape=[jax.ShapeDtypeStruct((B, L, width), BF16)] * 2,
        compiler_params=_cparams(("parallel", "parallel")),
        name="qk_norm_rope",
    )(proj3, proj3, cos_t, sin_t, qg, kg)


def _attn_kernel(lam_ref, q_ref, k_ref, v_ref, g_ref, o_ref, *, post):
    q = q_ref[0]
    k = k_ref[0]
    v = v_ref[0]
    lane = lax.broadcasted_iota(jnp.int32, q.shape, 1)
    outs = []
    for c in range(2):
        keep = (lane < HEAD_DIM) if c == 0 else (lane >= HEAD_DIM)
        qc = jnp.where(keep, q, jnp.zeros_like(q))
        s = lax.dot_general(qc, k, (((1,), (1,)), ((), ())), preferred_element_type=F32)
        m = jnp.max(s, axis=-1, keepdims=True)
        p = jnp.exp(s - m)
        l = jnp.sum(p, axis=-1, keepdims=True)
        o = jnp.dot(p.astype(BF16), v, preferred_element_type=F32)
        outs.append(o / l)
    a = outs[0] - lam_ref[0] * outs[1]
    ms = jnp.mean(a * a, axis=-1, keepdims=True)
    o_ref[0] = (a * lax.rsqrt(ms + RMS_EPS) * g_ref[...] * post).astype(o_ref.dtype)


def _attention(qn, kn, proj3, vcol, lam, subln_g, post):
    B, L, W = qn.shape
    tq = min(256, L)
    return pl.pallas_call(
        functools.partial(_attn_kernel, post=post),
        grid=(B, W // V_DIM, L // tq),
        in_specs=[pl.BlockSpec(memory_space=pltpu.SMEM),
                  pl.BlockSpec((1, tq, V_DIM), lambda b, h, i: (b, i, h)),
                  pl.BlockSpec((1, L, V_DIM), lambda b, h, i: (b, 0, h)),
                  pl.BlockSpec((1, L, V_DIM), lambda b, h, i: (b, 0, vcol + h)),
                  pl.BlockSpec((1, V_DIM), lambda b, h, i: (0, 0))],
        out_specs=pl.BlockSpec((1, tq, V_DIM), lambda b, h, i: (b, i, h)),
        out_shape=jax.ShapeDtypeStruct((B, L, W), BF16),
        compiler_params=_cparams(("parallel", "parallel", "arbitrary")),
        name="diff_attention",
    )(lam.reshape(1).astype(F32), qn, kn, proj3, subln_g.astype(F32).reshape(1, V_DIM))


def _merge_kernel(x_ref, yh_ref, ya_ref, gh_ref, ga_ref, who_ref, wao_ref, wout_ref, g_ref, *rest,
                  with_router):
    if with_router:
        wr_ref, xo_ref, hn_ref, lg_ref = rest
    else:
        xo_ref, hn_ref = rest
    th = jnp.dot(yh_ref[...], who_ref[...], preferred_element_type=F32)
    ta = jnp.dot(ya_ref[...], wao_ref[...], preferred_element_type=F32)
    merged = (jax.nn.sigmoid(gh_ref[...].astype(F32)) * th
              + jax.nn.sigmoid(ga_ref[...].astype(F32)) * ta)
    xn = x_ref[...] + jnp.dot(merged.astype(BF16), wout_ref[...], preferred_element_type=F32)
    xo_ref[...] = xn
    ms = jnp.mean(xn * xn, axis=-1, keepdims=True)
    h = xn * lax.rsqrt(ms + RMS_EPS) * g_ref[...]
    hn_ref[...] = h.astype(hn_ref.dtype)
    if with_router:
        lg_ref[...] = jnp.dot(h, wr_ref[...], precision=HIGHEST, preferred_element_type=F32)


def _merge(x2, yh, ya, proj, gcol, who, wao, wout, g_ffn, w_router):
    T, D = x2.shape
    W = yh.shape[1]
    tm = min(512, T)
    with_router = w_router is not None
    row = lambda w: pl.BlockSpec((tm, w), lambda i: (i, 0))
    full = lambda a: pl.BlockSpec(a.shape, lambda i: (0, 0))
    in_specs = [row(D), row(W), row(W),
                pl.BlockSpec((tm, D), lambda i: (i, gcol)),
                pl.BlockSpec((tm, D), lambda i: (i, gcol + 1)),
                full(who), full(wao), full(wout), pl.BlockSpec((1, D), lambda i: (0, 0))]
    args = [x2, yh, ya, proj, proj, who, wao, wout, g_ffn.astype(F32).reshape(1, D)]
    out_specs = [row(D), row(D)]
    out_shape = [jax.ShapeDtypeStruct((T, D), F32),
                 jax.ShapeDtypeStruct((T, D), F32 if with_router else BF16)]
    if with_router:
        E = w_router.shape[1]
        wr = jnp.zeros((D, LANES), F32).at[:, :E].set(w_router)
        in_specs.append(full(wr))
        args.append(wr)
        out_specs.append(row(LANES))
        out_shape.append(jax.ShapeDtypeStruct((T, LANES), F32))
    return pl.pallas_call(
        functools.partial(_merge_kernel, with_router=with_router),
        grid=(T // tm,),
        in_specs=in_specs,
        out_specs=out_specs,
        out_shape=out_shape,
        compiler_params=_cparams(("parallel",)),
        name="merge_outproj",
    )(*args)


def _swiglu(h, wg_ref, wu_ref, wd_ref, chunks):
    F = wg_ref.shape[-1]
    fc = F // chunks
    acc = None
    for c in range(chunks):
        sl = slice(c * fc, (c + 1) * fc)
        g = jnp.dot(h, wg_ref[:, sl], preferred_element_type=F32)
        u = jnp.dot(h, wu_ref[:, sl], preferred_element_type=F32)
        a = (g * jax.nn.sigmoid(g) * u).astype(BF16)
        d = jnp.dot(a, wd_ref[sl, :], preferred_element_type=F32)
        acc = d if acc is None else acc + d
    return acc


def _ffn_kernel(h_ref, x_ref, wg_ref, wu_ref, wd_ref, o_ref, *, chunks):
    o_ref[...] = x_ref[...] + _swiglu(h_ref[...], wg_ref, wu_ref, wd_ref, chunks)


def _ff_chunks(F):
    return 2 if F % (2 * LANES) == 0 else 1


def _dense_ffn(hn, x2, wg, wu, wd):
    T, D = x2.shape
    F = wg.shape[1]
    tm = min(256, T)
    row = pl.BlockSpec((tm, D), lambda i: (i, 0))
    return pl.pallas_call(
        functools.partial(_ffn_kernel, chunks=_ff_chunks(F)),
        grid=(T // tm,),
        in_specs=[row, row,
                  pl.BlockSpec((D, F), lambda i: (0, 0)),
                  pl.BlockSpec((D, F), lambda i: (0, 0)),
                  pl.BlockSpec((F, D), lambda i: (0, 0))],
        out_specs=row,
        out_shape=jax.ShapeDtypeStruct((T, D), F32),
        compiler_params=_cparams(("parallel",)),
        name="dense_ffn",
    )(hn, x2, wg, wu, wd)


def _row_copy(src_hbm, row, dst, r, sem):
    return pltpu.make_async_copy(src_hbm.at[pl.ds(row, 1)], dst.at[pl.ds(r, 1)], sem)


def _expert_kernel(be_ref, nb_ref, tok_ref, hn_hbm, wg_ref, wu_ref, wd_ref, o_ref, xbuf, sem, *, chunks):
    i = pl.program_id(0)
    rows = xbuf.shape[0]
    base = i * rows

    @pl.when(i < nb_ref[0])
    def _():
        def issue(r, carry):
            _row_copy(hn_hbm, tok_ref[base + r], xbuf, r, sem).start()
            return carry

        lax.fori_loop(0, rows, issue, 0)

        def wait(r, carry):
            _row_copy(hn_hbm, 0, xbuf, r, sem).wait()
            return carry

        lax.fori_loop(0, rows, wait, 0)
        o_ref[...] = _swiglu(xbuf[...].astype(BF16), wg_ref.at[0], wu_ref.at[0], wd_ref.at[0], chunks)

    @pl.when(i >= nb_ref[0])
    def _():
        o_ref[...] = jnp.zeros_like(o_ref)


def _experts(block_e, n_used, tok_of_slot, hn, wg, wu, wd):
    T, D = hn.shape
    E, _, F = wg.shape
    P = tok_of_slot.shape[0]
    rows = MOE_ROWS
    wspec = lambda shape: pl.BlockSpec((1,) + shape, lambda i, be, nb, tok: (be[i], 0, 0))
    return pl.pallas_call(
        functools.partial(_expert_kernel, chunks=_ff_chunks(F)),
        grid_spec=pltpu.PrefetchScalarGridSpec(
            num_scalar_prefetch=3,
            grid=(P // rows,),
            in_specs=[pl.BlockSpec(memory_space=pl.ANY),
                      wspec((D, F)), wspec((D, F)), wspec((F, D))],
            out_specs=pl.BlockSpec((rows, D), lambda i, be, nb, tok: (i, 0)),
            scratch_shapes=[pltpu.VMEM((rows, D), F32), pltpu.SemaphoreType.DMA(())]),
        out_shape=jax.ShapeDtypeStruct((P, D), F32),
        compiler_params=_cparams(("arbitrary",)),
        name="moe_experts",
    )(block_e, n_used, tok_of_slot, hn, wg, wu, wd)


def _combine_kernel(dest_ref, yp_hbm, x_ref, w_ref, o_ref, buf0, buf1, sem):
    i = pl.program_id(0)
    rows = buf0.shape[0]
    base = i * rows * TOP_K

    def issue(r, carry):
        _row_copy(yp_hbm, dest_ref[base + TOP_K * r], buf0, r, sem).start()
        _row_copy(yp_hbm, dest_ref[base + TOP_K * r + 1], buf1, r, sem).start()
        return carry

    lax.fori_loop(0, rows, issue, 0)

    def wait(r, carry):
        _row_copy(yp_hbm, 0, buf0, r, sem).wait()
        _row_copy(yp_hbm, 0, buf1, r, sem).wait()
        return carry

    lax.fori_loop(0, rows, wait, 0)
    w = w_ref[...]
    o_ref[...] = x_ref[...] + (buf0[...] * w[:, 0:1] + buf1[...] * w[:, 1:2])


def _combine(dest, yp, x2, top_w):
    T, D = x2.shape
    tm = min(256, T)
    return pl.pallas_call(
        _combine_kernel,
        grid_spec=pltpu.PrefetchScalarGridSpec(
            num_scalar_prefetch=1,
            grid=(T // tm,),
            in_specs=[pl.BlockSpec(memory_space=pl.ANY),
                      pl.BlockSpec((tm, D), lambda i, d: (i, 0)),
                      pl.BlockSpec((tm, TOP_K), lambda i, d: (i, 0))],
            out_specs=pl.BlockSpec((tm, D), lambda i, d: (i, 0)),
            scratch_shapes=[pltpu.VMEM((tm, D), F32), pltpu.VMEM((tm, D), F32),
                            pltpu.SemaphoreType.DMA(())]),
        out_shape=jax.ShapeDtypeStruct((T, D), F32),
        compiler_params=_cparams(("arbitrary",)),
        name="moe_combine",
    )(dest, yp, x2, top_w)


def _route(logits, n_experts, rows):
    T = logits.shape[0]
    top_val, top_idx = lax.top_k(logits[:, :n_experts], TOP_K)
    top_w = jax.nn.softmax(top_val, axis=-1)
    A = T * TOP_K
    e_flat = top_idx.reshape(A).astype(jnp.int32)
    onehot = (e_flat[:, None] == jnp.arange(n_experts, dtype=jnp.int32)[None, :]).astype(jnp.int32)
    csum = jnp.cumsum(onehot, axis=0)
    counts = csum[-1]
    rank = jnp.sum((csum - onehot) * onehot, axis=1)
    padded = (counts + rows - 1) // rows * rows
    pad_end = jnp.cumsum(padded)
    pad_start = pad_end - padded
    dest = pad_start[e_flat] + rank
    n_blocks = -(-A // rows) + n_experts
    P = n_blocks * rows
    order = jnp.argsort(e_flat)
    starts = jnp.cumsum(counts) - counts
    slot = jnp.arange(P, dtype=jnp.int32)
    slot_e = jnp.minimum(jnp.sum((slot[:, None] >= pad_end[None, :]).astype(jnp.int32), axis=1),
                         n_experts - 1)
    within = slot - pad_start[slot_e]
    valid = within < counts[slot_e]
    src = jnp.clip(starts[slot_e] + within, 0, A - 1)
    tok_of_slot = jnp.where(valid, order[src] // TOP_K, 0).astype(jnp.int32)
    block_e = slot_e[::rows]
    n_used = (pad_end[-1] // rows).astype(jnp.int32).reshape(1)
    return top_w.astype(F32), dest.astype(jnp.int32), tok_of_slot, block_e, n_used


def _pad_ff(w, axis):
    F = w.shape[axis]
    pad = [(0, 0)] * w.ndim
    pad[axis] = (0, _round_up(F, 2 * LANES) - F)
    return jnp.pad(w.astype(BF16), pad)


def kernel(x, norm_mix, w_in, conv_w, w_f1, b_f1, freq1, w_f2, b_f2, freq2, w_f3, hy_bias, q_norm, k_norm, lam_q1, lam_k1, lam_q2, lam_k2, subln_g, w_hy_o, w_at_o, w_out, norm_ffn, w_dense_gate, w_dense_up, w_dense_down, w_router, w_moe_gate, w_moe_up, w_moe_down):
    B, L, D = x.shape
    T = B * L
    depth = w_in.shape[0]
    hw = w_hy_o.shape[1]
    qk_w = N_HEADS * 2 * HEAD_DIM
    at_w = N_HEADS * V_DIM
    qcol = 3 * hw // qk_w
    kcol = qcol + 1
    vcol = (3 * hw + 2 * qk_w) // V_DIM
    gcol = (3 * hw + 2 * qk_w + at_w) // D
    mats = _dft_matrices(L)

    x2 = x.reshape(T, D)
    for layer in range(depth):
        lambda_init = 0.8 - 0.6 * math.exp(-0.3 * layer)
        proj = _inproj(x2, norm_mix[layer].astype(F32), w_in[layer].astype(BF16))
        proj3 = proj.reshape(B, L, -1)
        u3 = _shortconv(proj3, conv_w[layer].astype(F32), hw)
        kern = _hyena_time_filters(L, w_f1[layer], b_f1[layer], freq1[layer], w_f2[layer], b_f2[layer],
                                   freq2[layer], w_f3[layer], hw)
        y_h = _hyena(u3, kern, hy_bias[layer], mats)
        qn, kn = _qk_prep(proj3, q_norm[layer], k_norm[layer], qcol, kcol, qk_w)
        lam = (jnp.exp(jnp.sum(lam_q1[layer].astype(F32) * lam_k1[layer].astype(F32)))
               - jnp.exp(jnp.sum(lam_q2[layer].astype(F32) * lam_k2[layer].astype(F32))) + lambda_init)
        y_a = _attention(qn, kn, proj3, vcol, lam, subln_g[layer], 1.0 - lambda_init)
        i = layer // 2
        moe = layer % 2 == 1
        outs = _merge(x2, y_h.reshape(T, hw), y_a.reshape(T, at_w), proj, gcol,
                      w_hy_o[layer].astype(BF16), w_at_o[layer].astype(BF16), w_out[layer].astype(BF16),
                      norm_ffn[layer], w_router[i] if moe else None)
        if not moe:
            x2, hn = outs
            x2 = _dense_ffn(hn, x2, _pad_ff(w_dense_gate[i], 1), _pad_ff(w_dense_up[i], 1),
                            _pad_ff(w_dense_down[i], 0))
        else:
            x2, hn, logits = outs
            n_experts = w_router.shape[2]
            top_w, dest, tok_of_slot, block_e, n_used = _route(logits, n_experts, MOE_ROWS)
            yp = _experts(block_e, n_used, tok_of_slot, hn, _pad_ff(w_moe_gate[i], 2),
                          _pad_ff(w_moe_up[i], 2), _pad_ff(w_moe_down[i], 1))
            x2 = _combine(dest, yp, x2, top_w)
    return x2.reshape(B, L, D)
```

```python
import functools
import math

import numpy as np
import jax
import jax.numpy as jnp
from jax import lax
from jax.experimental import pallas as pl
from jax.experimental.pallas import tpu as pltpu

F32 = jnp.float32
BF16 = jnp.bfloat16
HIGHEST = lax.Precision.HIGHEST

N_HEADS = 4
HEAD_DIM = 64
V_DIM = 2 * HEAD_DIM
ROPE_THETA = 10000.0
FILTER_BANDS = 16
HYENA_TARGET = 1e-2
FAST_DECAY_PCT = 0.3
SLOW_DECAY_PCT = 1.5
TOP_K = 2
RMS_EPS = 1e-6

LANES = 128
BF16_SUBLANES = 16
VMEM_LIMIT = 56 * 1024 * 1024

DFT_N1 = 128
MOE_ROWS = 256


def _cparams(sem):
    return pltpu.CompilerParams(dimension_semantics=sem, vmem_limit_bytes=VMEM_LIMIT)


def _round_up(a, b):
    return (a + b - 1) // b * b


def _inproj_kernel(x_ref, g_ref, w_ref, o_ref, hn_ref):
    @pl.when(pl.program_id(1) == 0)
    def _():
        x = x_ref[...]
        ms = jnp.mean(x * x, axis=-1, keepdims=True)
        hn_ref[...] = (x * lax.rsqrt(ms + RMS_EPS) * g_ref[...]).astype(BF16)

    o_ref[...] = jnp.dot(hn_ref[...], w_ref[...], preferred_element_type=F32).astype(o_ref.dtype)


def _inproj(x2, g, w):
    T, D = x2.shape
    N = w.shape[1]
    tm = min(1024, T)
    tn = 1024
    return pl.pallas_call(
        _inproj_kernel,
        grid=(T // tm, N // tn),
        in_specs=[pl.BlockSpec((tm, D), lambda i, j: (i, 0)),
                  pl.BlockSpec((1, D), lambda i, j: (0, 0)),
                  pl.BlockSpec((D, tn), lambda i, j: (0, j))],
        out_specs=pl.BlockSpec((tm, tn), lambda i, j: (i, j)),
        out_shape=jax.ShapeDtypeStruct((T, N), BF16),
        scratch_shapes=[pltpu.VMEM((tm, D), BF16)],
        compiler_params=_cparams(("parallel", "arbitrary")),
        name="inproj",
    )(x2, g.reshape(1, D), w)


def _shortconv_kernel(u_ref, p_ref, n_ref, w_ref, o_ref):
    i = pl.program_id(1)
    last = pl.num_programs(1) - 1
    u = u_ref[0].astype(F32)
    tl = u.shape[0]
    prev_row = p_ref[0].astype(F32)[BF16_SUBLANES - 1:BF16_SUBLANES]
    next_row = n_ref[0].astype(F32)[0:1]
    prev_row = jnp.where(i > 0, prev_row, 0.0)
    next_row = jnp.where(i < last, next_row, 0.0)
    row = lax.broadcasted_iota(jnp.int32, u.shape, 0)
    up = jnp.where(row == 0, prev_row, pltpu.roll(u, 1, 0))
    un = jnp.where(row == tl - 1, next_row, pltpu.roll(u, tl - 1, 0))
    w = w_ref[...]
    o_ref[0, 0] = (up * w[0:1] + u * w[1:2] + un * w[2:3]).astype(o_ref.dtype)


def _shortconv(proj3, conv_w, hw):
    B, L, _ = proj3.shape
    tl = min(1024, L)
    hb = tl // BF16_SUBLANES
    nhb = L // BF16_SUBLANES
    return pl.pallas_call(
        _shortconv_kernel,
        grid=(B, L // tl, 3),
        in_specs=[pl.BlockSpec((1, tl, hw), lambda b, i, c: (b, i, c)),
                  pl.BlockSpec((1, BF16_SUBLANES, hw),
                               lambda b, i, c: (b, jnp.maximum(i * hb - 1, 0), c)),
                  pl.BlockSpec((1, BF16_SUBLANES, hw),
                               lambda b, i, c: (b, jnp.minimum((i + 1) * hb, nhb - 1), c)),
                  pl.BlockSpec((3, hw), lambda b, i, c: (0, c))],
        out_specs=pl.BlockSpec((1, 1, tl, hw), lambda b, i, c: (c, b, i, 0)),
        out_shape=jax.ShapeDtypeStruct((3, B, L, hw), BF16),
        compiler_params=_cparams(("parallel", "parallel", "arbitrary")),
        name="shortconv",
    )(proj3, proj3, proj3, conv_w)


def _filter_kernel(zz_ref, w1_ref, b1_ref, f1_ref, w2_ref, b2_ref, f2_ref, w3_ref, dl_ref, o_ref):
    zz = zz_ref[...]
    h = jnp.dot(zz, w1_ref[...], precision=HIGHEST, preferred_element_type=F32)
    h = jnp.sin(f1_ref[...] * (h + b1_ref[...]))
    h = jnp.dot(h, w2_ref[...], precision=HIGHEST, preferred_element_type=F32)
    h = jnp.sin(f2_ref[...] * (h + b2_ref[...]))
    hf = jnp.dot(h, w3_ref[0], precision=HIGHEST, preferred_element_type=F32)
    t = zz[:, 0:1]
    mask = zz[:, _MASK_COL:_MASK_COL + 1]
    o_ref[...] = hf * (jnp.exp(-t * dl_ref[...]) * mask)


_MASK_COL = 2 * FILTER_BANDS + 1


def _filter_positions(L):
    bands = FILTER_BANDS
    t = np.linspace(0.0, 1.0, L, dtype=np.float64)[:, None]
    w = 2.0 * math.pi * np.arange(L, dtype=np.float64)[:, None] / L
    f = np.linspace(1e-4, bands - 1, bands, dtype=np.float64)[None, :]
    z = np.concatenate([t, np.cos(f * w), -np.sin(f * w)], axis=-1)
    zz = np.zeros((2 * L, LANES), np.float64)
    zz[:L, :z.shape[1]] = z
    zz[L + 1:, :z.shape[1]] = z[1:][::-1]
    zz[:, _MASK_COL] = 1.0
    zz[L, :] = 0.0
    return zz.astype(np.float32)


def _hyena_time_filters(L, w_f1, b_f1, freq1, w_f2, b_f2, freq2, w_f3, hw):
    emb, hid = w_f1.shape
    zz = jnp.asarray(_filter_positions(L))
    w1 = jnp.zeros((LANES, hid), F32).at[:emb].set(w_f1)
    w3 = w_f3.reshape(hid, 2, 2, hw).transpose(2, 0, 1, 3).reshape(2, hid, 2 * hw)
    max_decay = math.log(HYENA_TARGET) / FAST_DECAY_PCT
    min_decay = math.log(HYENA_TARGET) / SLOW_DECAY_PCT
    deltas = np.abs(np.linspace(min_decay, max_decay, hw, dtype=np.float64))
    dl = jnp.asarray(np.tile(deltas, 2)[None, :], F32)
    tr = min(512, L)
    per_dir = L // tr
    row = lambda a: a.reshape(1, -1)
    return pl.pallas_call(
        _filter_kernel,
        grid=(2 * L // tr,),
        in_specs=[pl.BlockSpec((tr, LANES), lambda i: (i, 0)),
                  pl.BlockSpec((LANES, hid), lambda i: (0, 0)),
                  pl.BlockSpec((1, hid), lambda i: (0, 0)),
                  pl.BlockSpec((1, hid), lambda i: (0, 0)),
                  pl.BlockSpec((hid, hid), lambda i: (0, 0)),
                  pl.BlockSpec((1, hid), lambda i: (0, 0)),
                  pl.BlockSpec((1, hid), lambda i: (0, 0)),
                  pl.BlockSpec((1, hid, 2 * hw), lambda i: (i // per_dir, 0, 0)),
                  pl.BlockSpec((1, 2 * hw), lambda i: (0, 0))],
        out_specs=pl.BlockSpec((tr, 2 * hw), lambda i: (i, 0)),
        out_shape=jax.ShapeDtypeStruct((2 * L, 2 * hw), F32),
        compiler_params=_cparams(("parallel",)),
        name="hyena_filter",
    )(zz, w1, row(b_f1), row(freq1), w_f2, row(b_f2), row(freq2), w3, dl)


def _dft_matrices(L):
    N = 2 * L
    N1 = DFT_N1
    N2 = N // N1
    h = N2 // 2
    ang = 2.0 * np.pi * np.outer(np.arange(N2), np.arange(N2)) / N2
    c, s = np.cos(ang), np.sin(ang)
    m1 = np.zeros((N2, 2, N2))
    m1[:, 0, :h], m1[:, 0, h:] = c[:, :h], s[:, :h]
    m1[:, 1, :h], m1[:, 1, h:] = -s[:, :h], c[:, :h]
    m1 = m1.reshape(2 * N2, N2)
    m1f = np.stack([c, -s], axis=1).reshape(2 * N2, N2)
    m3 = np.zeros((2, h, N2, 2))
    m3[0, :, :, 0], m3[0, :, :, 1] = c[:h], -s[:h]
    m3[1, :, :, 0], m3[1, :, :, 1] = s[:h], c[:h]
    m3 = (m3 / N).reshape(N2, 2 * N2)
    k2 = jnp.arange(N2, dtype=jnp.int32)[:, None, None]
    k1 = jnp.arange(N1, dtype=jnp.int32)[None, :, None]
    n1 = jnp.arange(N1, dtype=jnp.int32)[None, None, :]
    m = (n1 * (N2 * k1 + k2)) % N
    th = m.astype(F32) * F32(2.0 * math.pi / N)
    C, S = jnp.cos(th), jnp.sin(th)
    m2 = jnp.concatenate([jnp.concatenate([C, S], axis=2),
                          jnp.concatenate([-S, C], axis=2)], axis=1)
    return (jnp.asarray(m1, F32), jnp.asarray(m1f, F32), m2, jnp.asarray(m3, F32))


def _dft_a_kernel(m_ref, z_ref, o_ref, *, precision):
    o_ref[0] = jnp.dot(m_ref[...], z_ref[0], precision=precision,
                       preferred_element_type=F32).astype(o_ref.dtype)


def _dft_a(m, z, out_dtype, precision=None):
    P, K, LN = z.shape
    R = m.shape[0]
    lt = min(8192, LN)
    return pl.pallas_call(
        functools.partial(_dft_a_kernel, precision=precision),
        grid=(P, LN // lt),
        in_specs=[pl.BlockSpec((R, K), lambda p, j: (0, 0)),
                  pl.BlockSpec((1, K, lt), lambda p, j: (p, 0, j))],
        out_specs=pl.BlockSpec((1, R, lt), lambda p, j: (p, 0, j)),
        out_shape=jax.ShapeDtypeStruct((P, R, LN), out_dtype),
        compiler_params=_cparams(("parallel", "parallel")),
        name="dft_a",
    )(m, z)


def _dft_filt_kernel(a_ref, m_ref, o_ref):
    o_ref[0] = jnp.dot(m_ref[0], a_ref[0], precision=HIGHEST, preferred_element_type=F32)


def _dft_filt(a, m2):
    N2, R, W = a.shape
    return pl.pallas_call(
        _dft_filt_kernel,
        grid=(N2,),
        in_specs=[pl.BlockSpec((1, R, W), lambda k: (k, 0, 0)),
                  pl.BlockSpec((1, R, R), lambda k: (k, 0, 0))],
        out_specs=pl.BlockSpec((1, R, W), lambda k: (k, 0, 0)),
        out_shape=jax.ShapeDtypeStruct((N2, R, W), F32),
        compiler_params=_cparams(("parallel",)),
        name="dft_filter_mid",
    )(a, m2)


def _dft_mid_kernel(a_ref, m_ref, mt_ref, kf_ref, o_ref):
    m = m_ref[0]
    mt = mt_ref[0]
    n1 = m.shape[0] // 2
    kr = kf_ref[0, :n1]
    ki = kf_ref[0, n1:]
    for p in range(a_ref.shape[0]):
        x = jnp.dot(m, a_ref[p, 0], preferred_element_type=F32)
        xr = x[:n1]
        xi = x[n1:]
        y = jnp.concatenate([xr * kr - xi * ki, xr * ki + xi * kr], axis=0).astype(BF16)
        o_ref[p, 0] = jnp.dot(mt, y, preferred_element_type=F32).astype(o_ref.dtype)


def _dft_mid(a, m2, m2t, kf, order):
    P, N2, R, C = a.shape
    return pl.pallas_call(
        _dft_mid_kernel,
        grid=(N2,),
        in_specs=[pl.BlockSpec((P, 1, R, C), lambda k: (0, k, 0, 0)),
                  pl.BlockSpec((1, R, R), lambda k: (k, 0, 0)),
                  pl.BlockSpec((1, R, R), lambda k: (k, 0, 0)),
                  pl.BlockSpec((1, R, C), lambda k: (k, 0, order))],
        out_specs=pl.BlockSpec((P, 1, R, C), lambda k: (0, k, 0, 0)),
        out_shape=jax.ShapeDtypeStruct((P, N2, R, C), BF16),
        compiler_params=_cparams(("parallel",)),
        name="dft_mid",
    )(a, m2, m2t, kf)


def _dft_c_kernel(m_ref, b_ref, z_ref, g_ref, bias_ref, o_ref):
    y = jnp.dot(m_ref[...], b_ref[0], preferred_element_type=F32)
    z = z_ref[0].astype(F32)
    g = g_ref[0].astype(F32)
    o_ref[0] = (g * (y + z * bias_ref[...])).astype(o_ref.dtype)


def _dft_c(m3, b, z, g, bias_t):
    P, R, LN = b.shape
    K = z.shape[1]
    lt = bias_t.shape[1]
    return pl.pallas_call(
        _dft_c_kernel,
        grid=(P, LN // lt),
        in_specs=[pl.BlockSpec((K, R), lambda p, j: (0, 0)),
                  pl.BlockSpec((1, R, lt), lambda p, j: (p, 0, j)),
                  pl.BlockSpec((1, K, lt), lambda p, j: (p, 0, j)),
                  pl.BlockSpec((1, K, lt), lambda p, j: (p, 0, j)),
                  pl.BlockSpec((1, lt), lambda p, j: (0, 0))],
        out_specs=pl.BlockSpec((1, K, lt), lambda p, j: (p, 0, j)),
        out_shape=jax.ShapeDtypeStruct((P, K, LN), BF16),
        compiler_params=_cparams(("parallel", "parallel")),
        name="dft_c",
    )(m3, b, z, g, bias_t)


def _hyena(u3, kern, hy_bias, mats):
    _, B, L, C = u3.shape
    m1, m1f, m2, m3 = mats
    N1 = DFT_N1
    N2 = 2 * L // N1
    P = B // 2
    LN = N1 * C
    af = _dft_a(m1f, kern.reshape(1, N2, N1 * 2 * C), F32, precision=HIGHEST)
    kf = _dft_filt(af.reshape(N2, 2 * N1, 2 * C), m2)
    m1b, m3b, m2b = m1.astype(BF16), m3.astype(BF16), m2.astype(BF16)
    m2tb = jnp.swapaxes(m2b, 1, 2)
    lt = min(8192, LN)
    x1, x2, v = (u3[i].reshape(P, N2, LN) for i in range(3))

    def long_conv(z, gate, order):
        a = _dft_a(m1b, z, BF16)
        b = _dft_mid(a.reshape(P, N2, 2 * N1, C), m2b, m2tb, kf, order)
        bias_t = jnp.tile(hy_bias[order].astype(F32), lt // C)[None, :]
        return _dft_c(m3b, b.reshape(P, 2 * N2, LN), z, gate, bias_t)

    z = long_conv(v, x1, 0)
    y = long_conv(z, x2, 1)
    return y.reshape(B, L, C)


def _qk_kernel(q_ref, k_ref, cos_ref, sin_ref, qg_ref, kg_ref, qo_ref, ko_ref, *, scale):
    cos = cos_ref[...]
    sin = sin_ref[...]
    lane = lax.broadcasted_iota(jnp.int32, cos.shape, 1)
    lo = lane < HEAD_DIM
    first = (lane % HEAD_DIM) < (HEAD_DIM // 2)

    def one(x_ref, g_ref, o_ref, mult):
        for h in range(x_ref.shape[2] // LANES):
            x = x_ref[0, :, h * LANES:(h + 1) * LANES].astype(F32)
            x2 = x * x
            s_lo = jnp.sum(jnp.where(lo, x2, 0.0), axis=-1, keepdims=True)
            s_hi = jnp.sum(jnp.where(lo, 0.0, x2), axis=-1, keepdims=True)
            r = lax.rsqrt(jnp.where(lo, s_lo, s_hi) * (1.0 / HEAD_DIM) + RMS_EPS)
            xn = x * r * g_ref[...]
            partner = jnp.where(first, pltpu.roll(xn, LANES - HEAD_DIM // 2, 1),
                                pltpu.roll(xn, HEAD_DIM // 2, 1))
            o_ref[0, :, h * LANES:(h + 1) * LANES] = ((xn * cos + partner * sin) * mult).astype(o_ref.dtype)

    one(q_ref, qg_ref, qo_ref, scale)
    one(k_ref, kg_ref, ko_ref, 1.0)


def _rope_tables(L):
    half = HEAD_DIM // 2
    inv = 1.0 / (ROPE_THETA ** (np.arange(0, HEAD_DIM, 2, dtype=np.float64) / HEAD_DIM))
    ang = np.arange(L, dtype=np.float64)[:, None] * inv[None, :]
    cos, sin = np.cos(ang), np.sin(ang)
    cos_t = np.tile(cos, (1, LANES // half))
    sin_t = np.tile(np.concatenate([-sin, sin], axis=1), (1, LANES // HEAD_DIM))
    return jnp.asarray(cos_t, F32), jnp.asarray(sin_t, F32)


def _qk_prep(proj3, q_norm, k_norm, qcol, kcol, width):
    B, L, _ = proj3.shape
    tl = min(512, L)
    cos_t, sin_t = _rope_tables(L)
    qg = jnp.tile(q_norm.astype(F32), LANES // HEAD_DIM)[None, :]
    kg = jnp.tile(k_norm.astype(F32), LANES // HEAD_DIM)[None, :]
    blk = lambda col: pl.BlockSpec((1, tl, width), lambda b, i: (b, i, col))
    tab = pl.BlockSpec((tl, LANES), lambda b, i: (i, 0))
    vec = pl.BlockSpec((1, LANES), lambda b, i: (0, 0))
    out = pl.BlockSpec((1, tl, width), lambda b, i: (b, i, 0))
    return pl.pallas_call(
        functools.partial(_qk_kernel, scale=HEAD_DIM ** -0.5 * math.log2(math.e)),
        grid=(B, L // tl),
        in_specs=[blk(qcol), blk(kcol), tab, tab, vec, vec],
        out_specs=[out, out],
        out_shape=[jax.ShapeDtypeStruct((B, L, width), BF16)] * 2,
        compiler_params=_cparams(("parallel", "parallel")),
        name="qk_norm_rope",
    )(proj3, proj3, cos_t, sin_t, qg, kg)


def _attn_kernel(lam_ref, q_ref, k_ref, v_ref, g_ref, o_ref, *, post):
    q = q_ref[0]
    k = k_ref[0]
    v = v_ref[0]
    lane = lax.broadcasted_iota(jnp.int32, q.shape, 1)
    zero = jnp.zeros_like(q)
    outs = []
    for c in range(2):
        qc = jnp.where(lane < HEAD_DIM, q, zero) if c == 0 else jnp.where(lane < HEAD_DIM, zero, q)
        s = lax.dot_general(qc, k, (((1,), (1,)), ((), ())), preferred_element_type=F32)
        m = jnp.max(s, axis=-1, keepdims=True)
        p = jnp.exp2(s - m)
        l = jnp.sum(p, axis=-1, keepdims=True)
        outs.append(jnp.dot(p.astype(BF16), v, preferred_element_type=F32) / l)
    a = outs[0] - lam_ref[0] * outs[1]
    ms = jnp.mean(a * a, axis=-1, keepdims=True)
    o_ref[0] = (a * lax.rsqrt(ms + RMS_EPS) * g_ref[...] * post).astype(o_ref.dtype)


def _attention(qn, kn, proj3, vcol, lam, subln_g, post):
    B, L, W = qn.shape
    tq = min(256, L)
    return pl.pallas_call(
        functools.partial(_attn_kernel, post=post),
        grid=(B, W // V_DIM, L // tq),
        in_specs=[pl.BlockSpec(memory_space=pltpu.SMEM),
                  pl.BlockSpec((1, tq, V_DIM), lambda b, h, i: (b, i, h)),
                  pl.BlockSpec((1, L, V_DIM), lambda b, h, i: (b, 0, h)),
                  pl.BlockSpec((1, L, V_DIM), lambda b, h, i: (b, 0, vcol + h)),
                  pl.BlockSpec((1, V_DIM), lambda b, h, i: (0, 0))],
        out_specs=pl.BlockSpec((1, tq, V_DIM), lambda b, h, i: (b, i, h)),
        out_shape=jax.ShapeDtypeStruct((B, L, W), BF16),
        compiler_params=_cparams(("parallel", "parallel", "arbitrary")),
        name="diff_attention",
    )(lam.reshape(1).astype(F32), qn, kn, proj3, subln_g.astype(F32).reshape(1, V_DIM))


def _merge_kernel(x_ref, yh_ref, ya_ref, gh_ref, ga_ref, who_ref, wao_ref, wout_ref, g_ref, *rest,
                  with_router):
    if with_router:
        wr_ref, xo_ref, hn_ref, lg_ref = rest
    else:
        xo_ref, hn_ref = rest
    th = jnp.dot(yh_ref[...], who_ref[...], preferred_element_type=F32)
    ta = jnp.dot(ya_ref[...], wao_ref[...], preferred_element_type=F32)
    merged = (jax.nn.sigmoid(gh_ref[...].astype(F32)) * th
              + jax.nn.sigmoid(ga_ref[...].astype(F32)) * ta)
    xn = x_ref[...] + jnp.dot(merged.astype(BF16), wout_ref[...], preferred_element_type=F32)
    xo_ref[...] = xn
    ms = jnp.mean(xn * xn, axis=-1, keepdims=True)
    h = xn * lax.rsqrt(ms + RMS_EPS) * g_ref[...]
    hn_ref[...] = h.astype(hn_ref.dtype)
    if with_router:
        lane = lax.broadcasted_iota(jnp.int32, lg_ref.shape, 1)
        lg = jnp.zeros(lg_ref.shape, F32)
        for e in range(wr_ref.shape[0]):
            lg = jnp.where(lane == e, jnp.sum(h * wr_ref[e:e + 1, :], axis=-1, keepdims=True), lg)
        lg_ref[...] = lg


def _merge(x2, yh, ya, proj, gcol, who, wao, wout, g_ffn, w_router):
    T, D = x2.shape
    W = yh.shape[1]
    tm = min(512, T)
    with_router = w_router is not None
    row = lambda w: pl.BlockSpec((tm, w), lambda i: (i, 0))
    full = lambda a: pl.BlockSpec(a.shape, lambda i: (0, 0))
    in_specs = [row(D), row(W), row(W),
                pl.BlockSpec((tm, D), lambda i: (i, gcol)),
                pl.BlockSpec((tm, D), lambda i: (i, gcol + 1)),
                full(who), full(wao), full(wout), pl.BlockSpec((1, D), lambda i: (0, 0))]
    args = [x2, yh, ya, proj, proj, who, wao, wout, g_ffn.astype(F32).reshape(1, D)]
    out_specs = [row(D), row(D)]
    out_shape = [jax.ShapeDtypeStruct((T, D), F32),
                 jax.ShapeDtypeStruct((T, D), F32 if with_router else BF16)]
    if with_router:
        wr = w_router.astype(F32).T
        in_specs.append(full(wr))
        args.append(wr)
        out_specs.append(row(LANES))
        out_shape.append(jax.ShapeDtypeStruct((T, LANES), F32))
    return pl.pallas_call(
        functools.partial(_merge_kernel, with_router=with_router),
        grid=(T // tm,),
        in_specs=in_specs,
        out_specs=out_specs,
        out_shape=out_shape,
        compiler_params=_cparams(("parallel",)),
        name="merge_outproj",
    )(*args)


def _swiglu(h, wg_ref, wu_ref, wd_ref, chunks):
    F = wg_ref.shape[-1]
    fc = F // chunks
    acc = None
    for c in range(chunks):
        sl = slice(c * fc, (c + 1) * fc)
        g = jnp.dot(h, wg_ref[:, sl], preferred_element_type=F32)
        u = jnp.dot(h, wu_ref[:, sl], preferred_element_type=F32)
        a = (g * jax.nn.sigmoid(g) * u).astype(BF16)
        d = jnp.dot(a, wd_ref[sl, :], preferred_element_type=F32)
        acc = d if acc is None else acc + d
    return acc


def _ffn_kernel(h_ref, x_ref, wg_ref, wu_ref, wd_ref, o_ref, *, chunks):
    o_ref[...] = x_ref[...] + _swiglu(h_ref[...], wg_ref, wu_ref, wd_ref, chunks)


def _ff_chunks(F):
    return 2 if F % (2 * LANES) == 0 else 1


def _dense_ffn(hn, x2, wg, wu, wd):
    T, D = x2.shape
    F = wg.shape[1]
    tm = min(256, T)
    row = pl.BlockSpec((tm, D), lambda i: (i, 0))
    return pl.pallas_call(
        functools.partial(_ffn_kernel, chunks=_ff_chunks(F)),
        grid=(T // tm,),
        in_specs=[row, row,
                  pl.BlockSpec((D, F), lambda i: (0, 0)),
                  pl.BlockSpec((D, F), lambda i: (0, 0)),
                  pl.BlockSpec((F, D), lambda i: (0, 0))],
        out_specs=row,
        out_shape=jax.ShapeDtypeStruct((T, D), F32),
        compiler_params=_cparams(("parallel",)),
        name="dense_ffn",
    )(hn, x2, wg, wu, wd)


def _row_copy(src_hbm, row, dst, r, sem):
    return pltpu.make_async_copy(src_hbm.at[pl.ds(row, 1)], dst.at[pl.ds(r, 1)], sem)


def _expert_kernel(be_ref, nb_ref, tok_ref, hn_hbm, wg_ref, wu_ref, wd_ref, o_ref, xbuf, sem, *, chunks):
    i = pl.program_id(0)
    n_used = nb_ref[0]
    rows = xbuf.shape[1]

    def gather(blk, slot):
        def issue(r, carry):
            _row_copy(hn_hbm, tok_ref[blk * rows + r], xbuf.at[slot], r, sem.at[slot]).start()
            return carry

        lax.fori_loop(0, rows, issue, 0, unroll=8)

    @pl.when(i == 0)
    def _():
        gather(0, 0)

    @pl.when(i + 1 < n_used)
    def _():
        gather(i + 1, (i + 1) % 2)

    @pl.when(i < n_used)
    def _():
        slot = i % 2

        def wait(r, carry):
            _row_copy(hn_hbm, 0, xbuf.at[slot], r, sem.at[slot]).wait()
            return carry

        lax.fori_loop(0, rows, wait, 0, unroll=8)
        o_ref[...] = _swiglu(xbuf[slot].astype(BF16), wg_ref.at[0], wu_ref.at[0], wd_ref.at[0], chunks)

    @pl.when(i >= n_used)
    def _():
        o_ref[...] = jnp.zeros_like(o_ref)


def _experts(block_e, n_used, tok_of_slot, hn, wg, wu, wd):
    T, D = hn.shape
    E, _, F = wg.shape
    P = tok_of_slot.shape[0]
    rows = MOE_ROWS
    wspec = lambda shape: pl.BlockSpec((1,) + shape, lambda i, be, nb, tok: (be[i], 0, 0))
    return pl.pallas_call(
        functools.partial(_expert_kernel, chunks=_ff_chunks(F)),
        grid_spec=pltpu.PrefetchScalarGridSpec(
            num_scalar_prefetch=3,
            grid=(P // rows,),
            in_specs=[pl.BlockSpec(memory_space=pl.ANY),
                      wspec((D, F)), wspec((D, F)), wspec((F, D))],
            out_specs=pl.BlockSpec((rows, D), lambda i, be, nb, tok: (i, 0)),
            scratch_shapes=[pltpu.VMEM((2, rows, D), F32), pltpu.SemaphoreType.DMA((2,))]),
        out_shape=jax.ShapeDtypeStruct((P, D), F32),
        compiler_params=_cparams(("arbitrary",)),
        name="moe_experts",
    )(block_e, n_used, tok_of_slot, hn, wg, wu, wd)


def _combine_kernel(dest_ref, yp_hbm, x_ref, w_ref, o_ref, buf0, buf1, sem):
    i = pl.program_id(0)
    rows = buf0.shape[1]

    def gather(tile, slot):
        base = tile * rows * TOP_K

        def issue(r, carry):
            _row_copy(yp_hbm, dest_ref[base + TOP_K * r], buf0.at[slot], r, sem.at[slot]).start()
            _row_copy(yp_hbm, dest_ref[base + TOP_K * r + 1], buf1.at[slot], r, sem.at[slot]).start()
            return carry

        lax.fori_loop(0, rows, issue, 0, unroll=8)

    @pl.when(i == 0)
    def _():
        gather(0, 0)

    @pl.when(i + 1 < pl.num_programs(0))
    def _():
        gather(i + 1, (i + 1) % 2)

    slot = i % 2

    def wait(r, carry):
        _row_copy(yp_hbm, 0, buf0.at[slot], r, sem.at[slot]).wait()
        _row_copy(yp_hbm, 0, buf1.at[slot], r, sem.at[slot]).wait()
        return carry

    lax.fori_loop(0, rows, wait, 0, unroll=8)
    w = w_ref[...]
    o_ref[...] = x_ref[...] + (buf0[slot] * w[:, 0:1] + buf1[slot] * w[:, 1:2])


def _combine(dest, yp, x2, top_w):
    T, D = x2.shape
    tm = min(256, T)
    return pl.pallas_call(
        _combine_kernel,
        grid_spec=pltpu.PrefetchScalarGridSpec(
            num_scalar_prefetch=1,
            grid=(T // tm,),
            in_specs=[pl.BlockSpec(memory_space=pl.ANY),
                      pl.BlockSpec((tm, D), lambda i, d: (i, 0)),
                      pl.BlockSpec((tm, TOP_K), lambda i, d: (i, 0))],
            out_specs=pl.BlockSpec((tm, D), lambda i, d: (i, 0)),
            scratch_shapes=[pltpu.VMEM((2, tm, D), F32), pltpu.VMEM((2, tm, D), F32),
                            pltpu.SemaphoreType.DMA((2,))]),
        out_shape=jax.ShapeDtypeStruct((T, D), F32),
        compiler_params=_cparams(("arbitrary",)),
        name="moe_combine",
    )(dest, yp, x2, top_w)


def _route(logits, n_experts, rows):
    T = logits.shape[0]
    top_val, top_idx = lax.top_k(logits[:, :n_experts], TOP_K)
    top_w = jax.nn.softmax(top_val, axis=-1)
    A = T * TOP_K
    e_flat = top_idx.reshape(A).astype(jnp.int32)
    onehot = (e_flat[:, None] == jnp.arange(n_experts, dtype=jnp.int32)[None, :]).astype(jnp.int32)
    csum = jnp.cumsum(onehot, axis=0)
    counts = csum[-1]
    rank = jnp.sum((csum - onehot) * onehot, axis=1)
    padded = (counts + rows - 1) // rows * rows
    pad_end = jnp.cumsum(padded)
    pad_start = pad_end - padded
    dest = pad_start[e_flat] + rank
    n_blocks = -(-A // rows) + n_experts
    P = n_blocks * rows
    order = jnp.argsort(e_flat)
    starts = jnp.cumsum(counts) - counts
    slot = jnp.arange(P, dtype=jnp.int32)
    slot_e = jnp.minimum(jnp.sum((slot[:, None] >= pad_end[None, :]).astype(jnp.int32), axis=1),
                         n_experts - 1)
    within = slot - pad_start[slot_e]
    valid = within < counts[slot_e]
    src = jnp.clip(starts[slot_e] + within, 0, A - 1)
    tok_of_slot = jnp.where(valid, order[src] // TOP_K, 0).astype(jnp.int32)
    block_e = slot_e[::rows]
    n_used = (pad_end[-1] // rows).astype(jnp.int32).reshape(1)
    return top_w.astype(F32), dest.astype(jnp.int32), tok_of_slot, block_e, n_used


def _pad_ff(w, axis):
    F = w.shape[axis]
    pad = [(0, 0)] * w.ndim
    pad[axis] = (0, _round_up(F, 2 * LANES) - F)
    return jnp.pad(w.astype(BF16), pad)


def kernel(x, norm_mix, w_in, conv_w, w_f1, b_f1, freq1, w_f2, b_f2, freq2, w_f3, hy_bias, q_norm, k_norm, lam_q1, lam_k1, lam_q2, lam_k2, subln_g, w_hy_o, w_at_o, w_out, norm_ffn, w_dense_gate, w_dense_up, w_dense_down, w_router, w_moe_gate, w_moe_up, w_moe_down):
    B, L, D = x.shape
    T = B * L
    depth = w_in.shape[0]
    hw = w_hy_o.shape[1]
    qk_w = N_HEADS * 2 * HEAD_DIM
    at_w = N_HEADS * V_DIM
    qcol = 3 * hw // qk_w
    kcol = qcol + 1
    vcol = (3 * hw + 2 * qk_w) // V_DIM
    gcol = (3 * hw + 2 * qk_w + at_w) // D
    mats = _dft_matrices(L)

    x2 = x.reshape(T, D)
    for layer in range(depth):
        lambda_init = 0.8 - 0.6 * math.exp(-0.3 * layer)
        proj = _inproj(x2, norm_mix[layer].astype(F32), w_in[layer].astype(BF16))
        proj3 = proj.reshape(B, L, -1)
        u3 = _shortconv(proj3, conv_w[layer].astype(F32), hw)
        kern = _hyena_time_filters(L, w_f1[layer], b_f1[layer], freq1[layer], w_f2[layer], b_f2[layer],
                                   freq2[layer], w_f3[layer], hw)
        y_h = _hyena(u3, kern, hy_bias[layer], mats)
        qn, kn = _qk_prep(proj3, q_norm[layer], k_norm[layer], qcol, kcol, qk_w)
        lam = (jnp.exp(jnp.sum(lam_q1[layer].astype(F32) * lam_k1[layer].astype(F32)))
               - jnp.exp(jnp.sum(lam_q2[layer].astype(F32) * lam_k2[layer].astype(F32))) + lambda_init)
        y_a = _attention(qn, kn, proj3, vcol, lam, subln_g[layer], 1.0 - lambda_init)
        i = layer // 2
        moe = layer % 2 == 1
        outs = _merge(x2, y_h.reshape(T, hw), y_a.reshape(T, at_w), proj, gcol,
                      w_hy_o[layer].astype(BF16), w_at_o[layer].astype(BF16), w_out[layer].astype(BF16),
                      norm_ffn[layer], w_router[i] if moe else None)
        if not moe:
            x2, hn = outs
            x2 = _dense_ffn(hn, x2, _pad_ff(w_dense_gate[i], 1), _pad_ff(w_dense_up[i], 1),
                            _pad_ff(w_dense_down[i], 0))
        else:
            x2, hn, logits = outs
            n_experts = w_router.shape[2]
            top_w, dest, tok_of_slot, block_e, n_used = _route(logits, n_experts, MOE_ROWS)
            yp = _experts(block_e, n_used, tok_of_slot, hn, _pad_ff(w_moe_gate[i], 2),
                          _pad_ff(w_moe_up[i], 2), _pad_ff(w_moe_down[i], 1))
            x2 = _combine(dest, yp, x2, top_w)
    return x2.reshape(B, L, D)
```

```python
import functools
import math

import numpy as np
import jax
import jax.numpy as jnp
from jax import lax
from jax.experimental import pallas as pl
from jax.experimental.pallas import tpu as pltpu

F32 = jnp.float32
BF16 = jnp.bfloat16
U32 = jnp.uint32
HIGHEST = lax.Precision.HIGHEST

N_HEADS = 4
HEAD_DIM = 64
V_DIM = 2 * HEAD_DIM
ROPE_THETA = 10000.0
FILTER_BANDS = 16
HYENA_TARGET = 1e-2
FAST_DECAY_PCT = 0.3
SLOW_DECAY_PCT = 1.5
TOP_K = 2
RMS_EPS = 1e-6

LANES = 128
BF16_SUBLANES = 16
VMEM_LIMIT = 56 * 1024 * 1024

DFT_N1 = 128
DFT_PAIRS = 16
DFT_LANE_BLOCKS = 2
MOE_ROWS = 256


def _cparams(sem):
    return pltpu.CompilerParams(dimension_semantics=sem, vmem_limit_bytes=VMEM_LIMIT)


def _round_up(a, b):
    return (a + b - 1) // b * b


def _inproj_kernel(x_ref, g_ref, w_ref, o_ref, hn_ref):
    @pl.when(pl.program_id(1) == 0)
    def _():
        x = x_ref[...]
        ms = jnp.mean(x * x, axis=-1, keepdims=True)
        hn_ref[...] = (x * lax.rsqrt(ms + RMS_EPS) * g_ref[...]).astype(BF16)

    o_ref[...] = jnp.dot(hn_ref[...], w_ref[...], preferred_element_type=F32).astype(o_ref.dtype)


def _inproj(x2, g, w):
    T, D = x2.shape
    N = w.shape[1]
    tm = min(1024, T)
    tn = 1024
    return pl.pallas_call(
        _inproj_kernel,
        grid=(T // tm, N // tn),
        in_specs=[pl.BlockSpec((tm, D), lambda i, j: (i, 0)),
                  pl.BlockSpec((1, D), lambda i, j: (0, 0)),
                  pl.BlockSpec((D, tn), lambda i, j: (0, j))],
        out_specs=pl.BlockSpec((tm, tn), lambda i, j: (i, j)),
        out_shape=jax.ShapeDtypeStruct((T, N), BF16),
        scratch_shapes=[pltpu.VMEM((tm, D), BF16)],
        compiler_params=_cparams(("parallel", "arbitrary")),
        name="inproj",
    )(x2, g.reshape(1, D), w)


def _shortconv_kernel(u_ref, p_ref, n_ref, w_ref, o_ref):
    i = pl.program_id(1)
    last = pl.num_programs(1) - 1
    u = u_ref[0].astype(F32)
    tl = u.shape[0]
    prev_row = p_ref[0].astype(F32)[BF16_SUBLANES - 1:BF16_SUBLANES]
    next_row = n_ref[0].astype(F32)[0:1]
    prev_row = jnp.where(i > 0, prev_row, 0.0)
    next_row = jnp.where(i < last, next_row, 0.0)
    row = lax.broadcasted_iota(jnp.int32, u.shape, 0)
    up = jnp.where(row == 0, prev_row, pltpu.roll(u, 1, 0))
    un = jnp.where(row == tl - 1, next_row, pltpu.roll(u, tl - 1, 0))
    w = w_ref[...]
    res = (up * w[0:1] + u * w[1:2] + un * w[2:3]).astype(BF16)
    pk = pltpu.bitcast(res, U32)
    for c in range(o_ref.shape[1]):
        o_ref[0, c, 0] = pk[:, c * LANES:(c + 1) * LANES]


def _shortconv(proj3, conv_w, hw):
    B, L, _ = proj3.shape
    tl = min(1024, L)
    hb = tl // BF16_SUBLANES
    nhb = L // BF16_SUBLANES
    nc = hw // LANES
    return pl.pallas_call(
        _shortconv_kernel,
        grid=(B, L // tl, 3),
        in_specs=[pl.BlockSpec((1, tl, hw), lambda b, i, c: (b, i, c)),
                  pl.BlockSpec((1, BF16_SUBLANES, hw),
                               lambda b, i, c: (b, jnp.maximum(i * hb - 1, 0), c)),
                  pl.BlockSpec((1, BF16_SUBLANES, hw),
                               lambda b, i, c: (b, jnp.minimum((i + 1) * hb, nhb - 1), c)),
                  pl.BlockSpec((3, hw), lambda b, i, c: (0, c))],
        out_specs=pl.BlockSpec((1, nc, 1, tl // 2, LANES), lambda b, i, c: (c, 0, b, i, 0)),
        out_shape=jax.ShapeDtypeStruct((3, nc, B, L // 2, LANES), U32),
        compiler_params=_cparams(("parallel", "parallel", "arbitrary")),
        name="shortconv",
    )(proj3, proj3, proj3, conv_w)


def _filter_kernel(zz_ref, w1_ref, b1_ref, f1_ref, w2_ref, b2_ref, f2_ref, w3_ref, dl_ref, o_ref):
    zz = zz_ref[...]
    h = jnp.dot(zz, w1_ref[...], precision=HIGHEST, preferred_element_type=F32)
    h = jnp.sin(f1_ref[...] * (h + b1_ref[...]))
    h = jnp.dot(h, w2_ref[...], precision=HIGHEST, preferred_element_type=F32)
    h = jnp.sin(f2_ref[...] * (h + b2_ref[...]))
    hf = jnp.dot(h, w3_ref[0], precision=HIGHEST, preferred_element_type=F32)
    t = zz[:, 0:1]
    mask = zz[:, _MASK_COL:_MASK_COL + 1]
    o_ref[...] = hf * (jnp.exp(-t * dl_ref[...]) * mask)


_MASK_COL = 2 * FILTER_BANDS + 1


def _filter_positions(L):
    bands = FILTER_BANDS
    t = np.linspace(0.0, 1.0, L, dtype=np.float64)[:, None]
    w = 2.0 * math.pi * np.arange(L, dtype=np.float64)[:, None] / L
    f = np.linspace(1e-4, bands - 1, bands, dtype=np.float64)[None, :]
    z = np.concatenate([t, np.cos(f * w), -np.sin(f * w)], axis=-1)
    zz = np.zeros((2 * L, LANES), np.float64)
    zz[:L, :z.shape[1]] = z
    zz[L + 1:, :z.shape[1]] = z[1:][::-1]
    zz[:, _MASK_COL] = 1.0
    zz[L, :] = 0.0
    return zz.astype(np.float32)


def _hyena_time_filters(L, w_f1, b_f1, freq1, w_f2, b_f2, freq2, w_f3, hw):
    emb, hid = w_f1.shape
    zz = jnp.asarray(_filter_positions(L))
    w1 = jnp.zeros((LANES, hid), F32).at[:emb].set(w_f1)
    w3 = w_f3.reshape(hid, 2, 2, hw).transpose(2, 0, 1, 3).reshape(2, hid, 2 * hw)
    max_decay = math.log(HYENA_TARGET) / FAST_DECAY_PCT
    min_decay = math.log(HYENA_TARGET) / SLOW_DECAY_PCT
    deltas = np.abs(np.linspace(min_decay, max_decay, hw, dtype=np.float64))
    dl = jnp.asarray(np.tile(deltas, 2)[None, :], F32)
    tr = min(512, L)
    per_dir = L // tr
    row = lambda a: a.reshape(1, -1)
    return pl.pallas_call(
        _filter_kernel,
        grid=(2 * L // tr,),
        in_specs=[pl.BlockSpec((tr, LANES), lambda i: (i, 0)),
                  pl.BlockSpec((LANES, hid), lambda i: (0, 0)),
                  pl.BlockSpec((1, hid), lambda i: (0, 0)),
                  pl.BlockSpec((1, hid), lambda i: (0, 0)),
                  pl.BlockSpec((hid, hid), lambda i: (0, 0)),
                  pl.BlockSpec((1, hid), lambda i: (0, 0)),
                  pl.BlockSpec((1, hid), lambda i: (0, 0)),
                  pl.BlockSpec((1, hid, 2 * hw), lambda i: (i // per_dir, 0, 0)),
                  pl.BlockSpec((1, 2 * hw), lambda i: (0, 0))],
        out_specs=pl.BlockSpec((tr, 2 * hw), lambda i: (i, 0)),
        out_shape=jax.ShapeDtypeStruct((2 * L, 2 * hw), F32),
        compiler_params=_cparams(("parallel",)),
        name="hyena_filter",
    )(zz, w1, row(b_f1), row(freq1), w_f2, row(b_f2), row(freq2), w3, dl)


def _dft_matrices(L):
    N = 2 * L
    N1 = DFT_N1
    N2 = N // N1
    h = N2 // 2
    ang = 2.0 * np.pi * np.outer(np.arange(N2), np.arange(N2)) / N2
    c, s = np.cos(ang), np.sin(ang)
    m1 = np.zeros((N2, 2, N2))
    m1[:, 0, :h], m1[:, 0, h:] = c[:, :h], s[:, :h]
    m1[:, 1, :h], m1[:, 1, h:] = -s[:, :h], c[:, :h]
    m1 = m1.reshape(2 * N2, N2)
    m1f = np.stack([c, -s], axis=1).reshape(2 * N2, N2)
    m3 = np.zeros((2, h, N2, 2))
    m3[0, :, :, 0], m3[0, :, :, 1] = c[:h], -s[:h]
    m3[1, :, :, 0], m3[1, :, :, 1] = s[:h], c[:h]
    m3 = (m3 / N).reshape(N2, 2 * N2)
    k2 = jnp.arange(N2, dtype=jnp.int32)[:, None, None]
    k1 = jnp.arange(N1, dtype=jnp.int32)[None, :, None]
    n1 = jnp.arange(N1, dtype=jnp.int32)[None, None, :]
    m = (n1 * (N2 * k1 + k2)) % N
    th = m.astype(F32) * F32(2.0 * math.pi / N)
    C, S = jnp.cos(th), jnp.sin(th)
    m2 = jnp.concatenate([jnp.concatenate([C, S], axis=2),
                          jnp.concatenate([-S, C], axis=2)], axis=1)
    return (jnp.asarray(m1, F32), jnp.asarray(m1f, F32), m2, jnp.asarray(m3, F32))


def _dft_a_kernel(m_ref, z_ref, o_ref, *, precision):
    o_ref[0] = jnp.dot(m_ref[...], z_ref[0], precision=precision,
                       preferred_element_type=F32).astype(o_ref.dtype)


def _dft_a(m, z, out_dtype, precision=None):
    P, K, LN = z.shape
    R = m.shape[0]
    lt = min(8192, LN)
    return pl.pallas_call(
        functools.partial(_dft_a_kernel, precision=precision),
        grid=(P, LN // lt),
        in_specs=[pl.BlockSpec((R, K), lambda p, j: (0, 0)),
                  pl.BlockSpec((1, K, lt), lambda p, j: (p, 0, j))],
        out_specs=pl.BlockSpec((1, R, lt), lambda p, j: (p, 0, j)),
        out_shape=jax.ShapeDtypeStruct((P, R, LN), out_dtype),
        compiler_params=_cparams(("parallel", "parallel")),
        name="dft_a",
    )(m, z)


def _dft_filt_kernel(a_ref, m_ref, o_ref):
    o_ref[0] = jnp.dot(m_ref[0], a_ref[0], precision=HIGHEST, preferred_element_type=F32)


def _dft_filt(a, m2):
    N2, R, W = a.shape
    return pl.pallas_call(
        _dft_filt_kernel,
        grid=(N2,),
        in_specs=[pl.BlockSpec((1, R, W), lambda k: (k, 0, 0)),
                  pl.BlockSpec((1, R, R), lambda k: (k, 0, 0))],
        out_specs=pl.BlockSpec((1, R, W), lambda k: (k, 0, 0)),
        out_shape=jax.ShapeDtypeStruct((N2, R, W), F32),
        compiler_params=_cparams(("parallel",)),
        name="dft_filter_mid",
    )(a, m2)


def _packed_rows(ref, lead, start, size, stride):
    cols = [jnp.concatenate([ref[(c,) + l + (pl.ds(start, size, stride=stride), slice(None))] for l in lead], axis=0)
            for c in range(ref.shape[0])]
    return pltpu.bitcast(jnp.concatenate(cols, axis=1), BF16)


def _dft_a_pk_kernel(m_ref, z_ref, o_ref, *, tm, nh, npairs):
    j0 = pl.program_id(2) * tm
    rows = m_ref.shape[0] // 2
    for j in range(tm):
        rhs = _packed_rows(z_ref, [(0,), (1,)], j0 + j, nh, npairs)
        res = jnp.dot(m_ref[...], rhs, preferred_element_type=F32)
        pk = pltpu.bitcast(res.astype(BF16), U32)
        for c in range(o_ref.shape[0]):
            o_ref[c, 0, 0, pl.ds(j, rows, stride=tm), :] = pk[:, c * LANES:(c + 1) * LANES]


def _dft_a_pk(m1x, z, slot):
    _, NC, B, L2, _ = z.shape
    npairs = DFT_N1 // 2
    nh = L2 // npairs
    tm, ncb = DFT_PAIRS, DFT_LANE_BLOCKS
    rows = m1x.shape[0] // 2
    P = B // 2
    return pl.pallas_call(
        functools.partial(_dft_a_pk_kernel, tm=tm, nh=nh, npairs=npairs),
        grid=(P, NC // ncb, npairs // tm),
        in_specs=[pl.BlockSpec(m1x.shape, lambda p, h, j: (0, 0)),
                  pl.BlockSpec((None, ncb, 2, L2, LANES), lambda p, h, j: (slot, h, p, 0, 0))],
        out_specs=pl.BlockSpec((ncb, 1, 1, rows * tm, LANES), lambda p, h, j: (h, p, j, 0, 0)),
        out_shape=jax.ShapeDtypeStruct((NC, P, npairs // tm, rows * tm, LANES), U32),
        compiler_params=_cparams(("parallel", "parallel", "arbitrary")),
        name="dft_a",
    )(m1x, z)


def _dft_mid_kernel(a_ref, m_ref, mt_ref, kf_ref, o_ref):
    m = m_ref[0]
    mt = mt_ref[0]
    n1 = m.shape[0] // 2
    kr = kf_ref[0, :n1]
    ki = kf_ref[0, n1:]
    nc, npair = a_ref.shape[0], a_ref.shape[1]
    blk = a_ref.shape[2], a_ref.shape[4], LANES
    for p in range(npair):
        cols = [a_ref[c, p, :, 0].reshape(n1, LANES) for c in range(nc)]
        a = pltpu.bitcast(jnp.concatenate(cols, axis=1), BF16)
        x = jnp.dot(m, a, preferred_element_type=F32)
        xr = x[:n1]
        xi = x[n1:]
        y = jnp.concatenate([xr * kr - xi * ki, xr * ki + xi * kr], axis=0).astype(BF16)
        pk = pltpu.bitcast(jnp.dot(mt, y, preferred_element_type=F32).astype(BF16), U32)
        for c in range(nc):
            o_ref[c, p, :, 0] = pk[:, c * LANES:(c + 1) * LANES].reshape(blk)


def _dft_mid(a, m2, m2t, kf, order):
    NC, P, G, N2, W, _ = a.shape
    R = m2.shape[1]
    C = NC * LANES
    blk = pl.BlockSpec((NC, P, G, 1, W, LANES), lambda k: (0, 0, 0, k, 0, 0))
    return pl.pallas_call(
        _dft_mid_kernel,
        grid=(N2,),
        in_specs=[blk,
                  pl.BlockSpec((1, R, R), lambda k: (k, 0, 0)),
                  pl.BlockSpec((1, R, R), lambda k: (k, 0, 0)),
                  pl.BlockSpec((1, R, C), lambda k: (k, 0, order))],
        out_specs=blk,
        out_shape=jax.ShapeDtypeStruct(a.shape, U32),
        compiler_params=_cparams(("parallel",)),
        name="dft_mid",
    )(a, m2, m2t, kf)


def _dft_c_pk_kernel(m_ref, b_ref, z_ref, g_ref, bias_ref, o_ref, *, tm, nh, npairs):
    j0 = pl.program_id(2) * tm
    rows = b_ref.shape[3] // tm
    bias = bias_ref[...]
    both = [(0,), (1,)]
    for j in range(tm):
        rhs = _packed_rows(b_ref, [(0, 0)], j, rows, tm)
        y = jnp.dot(m_ref[...], rhs, preferred_element_type=F32)
        z = _packed_rows(z_ref, both, j0 + j, nh, npairs).astype(F32)
        g = _packed_rows(g_ref, both, j0 + j, nh, npairs).astype(F32)
        pk = pltpu.bitcast((g * (y + z * bias)).astype(BF16), U32)
        for c in range(o_ref.shape[0]):
            for b in range(2):
                o_ref[c, b, pl.ds(j0 + j, nh, stride=npairs), :] = pk[b * nh:(b + 1) * nh, c * LANES:(c + 1) * LANES]


def _dft_c_pk(m3x, b, z, zslot, g, gslot, bias):
    NC, P, G, W, _ = b.shape
    L2 = z.shape[3]
    npairs = DFT_N1 // 2
    nh = L2 // npairs
    tm, ncb = DFT_PAIRS, DFT_LANE_BLOCKS
    seq = lambda slot: pl.BlockSpec((None, ncb, 2, L2, LANES), lambda p, h, j: (slot, h, p, 0, 0))
    return pl.pallas_call(
        functools.partial(_dft_c_pk_kernel, tm=tm, nh=nh, npairs=npairs),
        grid=(P, NC // ncb, G),
        in_specs=[pl.BlockSpec(m3x.shape, lambda p, h, j: (0, 0)),
                  pl.BlockSpec((ncb, 1, 1, W, LANES), lambda p, h, j: (h, p, j, 0, 0)),
                  seq(zslot), seq(gslot),
                  pl.BlockSpec((1, ncb * LANES), lambda p, h, j: (0, h))],
        out_specs=seq(0),
        out_shape=jax.ShapeDtypeStruct((1, NC, 2 * P, L2, LANES), U32),
        compiler_params=_cparams(("parallel", "parallel", "arbitrary")),
        name="dft_c",
    )(m3x, b, z, g, bias)


def _mid_row_order(tm):
    N1 = DFT_N1
    idx = []
    for mt in range(N1 // 2 // tm):
        for ri in range(2):
            for jj in range(tm):
                for parity in range(2):
                    idx.append(ri * N1 + 2 * (mt * tm + jj) + parity)
    return np.asarray(idx, np.int32)


def _hyena(u3, kern, hy_bias, mats):
    _, NC, B, L2, _ = u3.shape
    C = NC * LANES
    m1, m1f, m2, m3 = mats
    N1 = DFT_N1
    N2 = 4 * L2 // N1
    tm = DFT_PAIRS
    af = _dft_a(m1f, kern.reshape(1, N2, N1 * 2 * C), F32, precision=HIGHEST)
    kf = _dft_filt(af.reshape(N2, 2 * N1, 2 * C), m2)
    eye2 = jnp.eye(2, dtype=F32)
    m1x = jnp.kron(m1, eye2).astype(BF16)
    m3x = jnp.kron(m3, eye2).astype(BF16)
    m2p = jnp.take(m2, jnp.asarray(_mid_row_order(tm)), axis=2).astype(BF16)
    m2tp = jnp.swapaxes(m2p, 1, 2)
    bias = hy_bias.astype(F32)

    def long_conv(z, zslot, gslot, order):
        a = _dft_a_pk(m1x, z, zslot)
        NCa, P, G, W, _ = a.shape
        b = _dft_mid(a.reshape(NCa, P, G, N2, W // N2, LANES), m2p, m2tp, kf, order)
        return _dft_c_pk(m3x, b.reshape(a.shape), z, zslot, u3, gslot, bias[order][None, :])

    z = long_conv(u3, 2, 0, 0)
    return long_conv(z, 0, 1, 1)


def _qk_kernel(q_ref, k_ref, cos_ref, sin_ref, qg_ref, kg_ref, qo_ref, ko_ref, *, scale):
    cos = cos_ref[...]
    sin = sin_ref[...]
    lane = lax.broadcasted_iota(jnp.int32, cos.shape, 1)
    lo = lane < HEAD_DIM
    first = (lane % HEAD_DIM) < (HEAD_DIM // 2)

    def one(x_ref, g_ref, o_ref, mult):
        for h in range(x_ref.shape[2] // LANES):
            x = x_ref[0, :, h * LANES:(h + 1) * LANES].astype(F32)
            x2 = x * x
            s_lo = jnp.sum(jnp.where(lo, x2, 0.0), axis=-1, keepdims=True)
            s_hi = jnp.sum(jnp.where(lo, 0.0, x2), axis=-1, keepdims=True)
            r = lax.rsqrt(jnp.where(lo, s_lo, s_hi) * (1.0 / HEAD_DIM) + RMS_EPS)
            xn = x * r * g_ref[...]
            partner = jnp.where(first, pltpu.roll(xn, LANES - HEAD_DIM // 2, 1),
                                pltpu.roll(xn, HEAD_DIM // 2, 1))
            o_ref[0, :, h * LANES:(h + 1) * LANES] = ((xn * cos + partner * sin) * mult).astype(o_ref.dtype)

    one(q_ref, qg_ref, qo_ref, scale)
    one(k_ref, kg_ref, ko_ref, 1.0)


def _rope_tables(L):
    half = HEAD_DIM // 2
    inv = 1.0 / (ROPE_THETA ** (np.arange(0, HEAD_DIM, 2, dtype=np.float64) / HEAD_DIM))
    ang = np.arange(L, dtype=np.float64)[:, None] * inv[None, :]
    cos, sin = np.cos(ang), np.sin(ang)
    cos_t = np.tile(cos, (1, LANES // half))
    sin_t = np.tile(np.concatenate([-sin, sin], axis=1), (1, LANES // HEAD_DIM))
    return jnp.asarray(cos_t, F32), jnp.asarray(sin_t, F32)


def _qk_prep(proj3, q_norm, k_norm, qcol, kcol, width):
    B, L, _ = proj3.shape
    tl = min(512, L)
    cos_t, sin_t = _rope_tables(L)
    qg = jnp.tile(q_norm.astype(F32), LANES // HEAD_DIM)[None, :]
    kg = jnp.tile(k_norm.astype(F32), LANES // HEAD_DIM)[None, :]
    blk = lambda col: pl.BlockSpec((1, tl, width), lambda b, i: (b, i, col))
    tab = pl.BlockSpec((tl, LANES), lambda b, i: (i, 0))
    vec = pl.BlockSpec((1, LANES), lambda b, i: (0, 0))
    out = pl.BlockSpec((1, tl, width), lambda b, i: (b, i, 0))
    return pl.pallas_call(
        functools.partial(_qk_kernel, scale=HEAD_DIM ** -0.5 * math.log2(math.e)),
        grid=(B, L // tl),
        in_specs=[blk(qcol), blk(kcol), tab, tab, vec, vec],
        out_specs=[out, out],
        out_shape=[jax.ShapeDtypeStruct((B, L, width), BF16)] * 2,
        compiler_params=_cparams(("parallel", "parallel")),
        name="qk_norm_rope",
    )(proj3, proj3, cos_t, sin_t, qg, kg)


def _attn_kernel(lam_ref, q_ref, k_ref, v_ref, g_ref, o_ref, *, post):
    q = q_ref[0]
    k = k_ref[0]
    v = v_ref[0]
    lane = lax.broadcasted_iota(jnp.int32, q.shape, 1)
    zero = jnp.zeros_like(q)
    outs = []
    for c in range(2):
        qc = jnp.where(lane < HEAD_DIM, q, zero) if c == 0 else jnp.where(lane < HEAD_DIM, zero, q)
        s = lax.dot_general(qc, k, (((1,), (1,)), ((), ())), preferred_element_type=F32)
        m = jnp.max(s, axis=-1, keepdims=True)
        p = jnp.exp2(s - m)
        l = jnp.sum(p, axis=-1, keepdims=True)
        outs.append(jnp.dot(p.astype(BF16), v, preferred_element_type=F32) / l)
    a = outs[0] - lam_ref[0] * outs[1]
    ms = jnp.mean(a * a, axis=-1, keepdims=True)
    o_ref[0] = (a * lax.rsqrt(ms + RMS_EPS) * g_ref[...] * post).astype(o_ref.dtype)


def _attention(qn, kn, proj3, vcol, lam, subln_g, post):
    B, L, W = qn.shape
    tq = min(256, L)
    return pl.pallas_call(
        functools.partial(_attn_kernel, post=post),
        grid=(B, W // V_DIM, L // tq),
        in_specs=[pl.BlockSpec(memory_space=pltpu.SMEM),
                  pl.BlockSpec((1, tq, V_DIM), lambda b, h, i: (b, i, h)),
                  pl.BlockSpec((1, L, V_DIM), lambda b, h, i: (b, 0, h)),
                  pl.BlockSpec((1, L, V_DIM), lambda b, h, i: (b, 0, vcol + h)),
                  pl.BlockSpec((1, V_DIM), lambda b, h, i: (0, 0))],
        out_specs=pl.BlockSpec((1, tq, V_DIM), lambda b, h, i: (b, i, h)),
        out_shape=jax.ShapeDtypeStruct((B, L, W), BF16),
        compiler_params=_cparams(("parallel", "parallel", "arbitrary")),
        name="diff_attention",
    )(lam.reshape(1).astype(F32), qn, kn, proj3, subln_g.astype(F32).reshape(1, V_DIM))


def _merge_kernel(x_ref, yh_ref, ya_ref, gh_ref, ga_ref, who_ref, wao_ref, wout_ref, g_ref, *rest,
                  with_router):
    if with_router:
        wr_ref, xo_ref, hn_ref, lg_ref = rest
    else:
        xo_ref, hn_ref = rest
    yh = jnp.concatenate([pltpu.bitcast(yh_ref[c], BF16) for c in range(yh_ref.shape[0])], axis=1)
    th = jnp.dot(yh, who_ref[...], preferred_element_type=F32)
    ta = jnp.dot(ya_ref[...], wao_ref[...], preferred_element_type=F32)
    merged = (jax.nn.sigmoid(gh_ref[...].astype(F32)) * th
              + jax.nn.sigmoid(ga_ref[...].astype(F32)) * ta)
    xn = x_ref[...] + jnp.dot(merged.astype(BF16), wout_ref[...], preferred_element_type=F32)
    xo_ref[...] = xn
    ms = jnp.mean(xn * xn, axis=-1, keepdims=True)
    h = xn * lax.rsqrt(ms + RMS_EPS) * g_ref[...]
    hn_ref[...] = h.astype(hn_ref.dtype)
    if with_router:
        lane = lax.broadcasted_iota(jnp.int32, lg_ref.shape, 1)
        lg = jnp.zeros(lg_ref.shape, F32)
        for e in range(wr_ref.shape[0]):
            lg = jnp.where(lane == e, jnp.sum(h * wr_ref[e:e + 1, :], axis=-1, keepdims=True), lg)
        lg_ref[...] = lg


def _merge(x2, yh, ya, proj, gcol, who, wao, wout, g_ffn, w_router):
    T, D = x2.shape
    W = ya.shape[1]
    tm = min(512, T)
    with_router = w_router is not None
    row = lambda w: pl.BlockSpec((tm, w), lambda i: (i, 0))
    full = lambda a: pl.BlockSpec(a.shape, lambda i: (0, 0))
    in_specs = [row(D), pl.BlockSpec((yh.shape[0], tm // 2, LANES), lambda i: (0, i, 0)), row(W),
                pl.BlockSpec((tm, D), lambda i: (i, gcol)),
                pl.BlockSpec((tm, D), lambda i: (i, gcol + 1)),
                full(who), full(wao), full(wout), pl.BlockSpec((1, D), lambda i: (0, 0))]
    args = [x2, yh, ya, proj, proj, who, wao, wout, g_ffn.astype(F32).reshape(1, D)]
    out_specs = [row(D), row(D)]
    out_shape = [jax.ShapeDtypeStruct((T, D), F32),
                 jax.ShapeDtypeStruct((T, D), F32 if with_router else BF16)]
    if with_router:
        wr = w_router.astype(F32).T
        in_specs.append(full(wr))
        args.append(wr)
        out_specs.append(row(LANES))
        out_shape.append(jax.ShapeDtypeStruct((T, LANES), F32))
    return pl.pallas_call(
        functools.partial(_merge_kernel, with_router=with_router),
        grid=(T // tm,),
        in_specs=in_specs,
        out_specs=out_specs,
        out_shape=out_shape,
        compiler_params=_cparams(("parallel",)),
        name="merge_outproj",
    )(*args)


def _swiglu(h, wg_ref, wu_ref, wd_ref, chunks):
    F = wg_ref.shape[-1]
    fc = F // chunks
    acc = None
    for c in range(chunks):
        sl = slice(c * fc, (c + 1) * fc)
        g = jnp.dot(h, wg_ref[:, sl], preferred_element_type=F32)
        u = jnp.dot(h, wu_ref[:, sl], preferred_element_type=F32)
        a = (g * jax.nn.sigmoid(g) * u).astype(BF16)
        d = jnp.dot(a, wd_ref[sl, :], preferred_element_type=F32)
        acc = d if acc is None else acc + d
    return acc


def _ffn_kernel(h_ref, x_ref, wg_ref, wu_ref, wd_ref, o_ref, *, chunks):
    o_ref[...] = x_ref[...] + _swiglu(h_ref[...], wg_ref, wu_ref, wd_ref, chunks)


def _ff_chunks(F):
    return 2 if F % (2 * LANES) == 0 else 1


def _dense_ffn(hn, x2, wg, wu, wd):
    T, D = x2.shape
    F = wg.shape[1]
    tm = min(256, T)
    row = pl.BlockSpec((tm, D), lambda i: (i, 0))
    return pl.pallas_call(
        functools.partial(_ffn_kernel, chunks=_ff_chunks(F)),
        grid=(T // tm,),
        in_specs=[row, row,
                  pl.BlockSpec((D, F), lambda i: (0, 0)),
                  pl.BlockSpec((D, F), lambda i: (0, 0)),
                  pl.BlockSpec((F, D), lambda i: (0, 0))],
        out_specs=row,
        out_shape=jax.ShapeDtypeStruct((T, D), F32),
        compiler_params=_cparams(("parallel",)),
        name="dense_ffn",
    )(hn, x2, wg, wu, wd)


def _row_copy(src_hbm, row, dst, r, sem):
    return pltpu.make_async_copy(src_hbm.at[pl.ds(row, 1)], dst.at[pl.ds(r, 1)], sem)


def _expert_kernel(be_ref, nb_ref, tok_ref, hn_hbm, wg_ref, wu_ref, wd_ref, o_ref, xbuf, sem, *, chunks):
    i = pl.program_id(0)
    n_used = nb_ref[0]
    rows = xbuf.shape[1]

    def gather(blk, slot):
        def issue(r, carry):
            _row_copy(hn_hbm, tok_ref[blk * rows + r], xbuf.at[slot], r, sem.at[slot]).start()
            return carry

        lax.fori_loop(0, rows, issue, 0, unroll=8)

    @pl.when(i == 0)
    def _():
        gather(0, 0)

    @pl.when(i + 1 < n_used)
    def _():
        gather(i + 1, (i + 1) % 2)

    @pl.when(i < n_used)
    def _():
        slot = i % 2

        def wait(r, carry):
            _row_copy(hn_hbm, 0, xbuf.at[slot], r, sem.at[slot]).wait()
            return carry

        lax.fori_loop(0, rows, wait, 0, unroll=8)
        o_ref[...] = _swiglu(xbuf[slot].astype(BF16), wg_ref.at[0], wu_ref.at[0], wd_ref.at[0], chunks)

    @pl.when(i >= n_used)
    def _():
        o_ref[...] = jnp.zeros_like(o_ref)


def _experts(block_e, n_used, tok_of_slot, hn, wg, wu, wd):
    T, D = hn.shape
    E, _, F = wg.shape
    P = tok_of_slot.shape[0]
    rows = MOE_ROWS
    wspec = lambda shape: pl.BlockSpec((1,) + shape, lambda i, be, nb, tok: (be[i], 0, 0))
    return pl.pallas_call(
        functools.partial(_expert_kernel, chunks=_ff_chunks(F)),
        grid_spec=pltpu.PrefetchScalarGridSpec(
            num_scalar_prefetch=3,
            grid=(P // rows,),
            in_specs=[pl.BlockSpec(memory_space=pl.ANY),
                      wspec((D, F)), wspec((D, F)), wspec((F, D))],
            out_specs=pl.BlockSpec((rows, D), lambda i, be, nb, tok: (i, 0)),
            scratch_shapes=[pltpu.VMEM((2, rows, D), F32), pltpu.SemaphoreType.DMA((2,))]),
        out_shape=jax.ShapeDtypeStruct((P, D), F32),
        compiler_params=_cparams(("arbitrary",)),
        name="moe_experts",
    )(block_e, n_used, tok_of_slot, hn, wg, wu, wd)


def _combine_kernel(dest_ref, yp_hbm, x_ref, w_ref, o_ref, buf0, buf1, sem):
    i = pl.program_id(0)
    rows = buf0.shape[1]

    def gather(tile, slot):
        base = tile * rows * TOP_K

        def issue(r, carry):
            _row_copy(yp_hbm, dest_ref[base + TOP_K * r], buf0.at[slot], r, sem.at[slot]).start()
            _row_copy(yp_hbm, dest_ref[base + TOP_K * r + 1], buf1.at[slot], r, sem.at[slot]).start()
            return carry

        lax.fori_loop(0, rows, issue, 0, unroll=8)

    @pl.when(i == 0)
    def _():
        gather(0, 0)

    @pl.when(i + 1 < pl.num_programs(0))
    def _():
        gather(i + 1, (i + 1) % 2)

    slot = i % 2

    def wait(r, carry):
        _row_copy(yp_hbm, 0, buf0.at[slot], r, sem.at[slot]).wait()
        _row_copy(yp_hbm, 0, buf1.at[slot], r, sem.at[slot]).wait()
        return carry

    lax.fori_loop(0, rows, wait, 0, unroll=8)
    w = w_ref[...]
    o_ref[...] = x_ref[...] + (buf0[slot] * w[:, 0:1] + buf1[slot] * w[:, 1:2])


def _combine(dest, yp, x2, top_w):
    T, D = x2.shape
    tm = min(256, T)
    return pl.pallas_call(
        _combine_kernel,
        grid_spec=pltpu.PrefetchScalarGridSpec(
            num_scalar_prefetch=1,
            grid=(T // tm,),
            in_specs=[pl.BlockSpec(memory_space=pl.ANY),
                      pl.BlockSpec((tm, D), lambda i, d: (i, 0)),
                      pl.BlockSpec((tm, TOP_K), lambda i, d: (i, 0))],
            out_specs=pl.BlockSpec((tm, D), lambda i, d: (i, 0)),
            scratch_shapes=[pltpu.VMEM((2, tm, D), F32), pltpu.VMEM((2, tm, D), F32),
                            pltpu.SemaphoreType.DMA((2,))]),
        out_shape=jax.ShapeDtypeStruct((T, D), F32),
        compiler_params=_cparams(("arbitrary",)),
        name="moe_combine",
    )(dest, yp, x2, top_w)


def _route(logits, n_experts, rows):
    T = logits.shape[0]
    top_val, top_idx = lax.top_k(logits[:, :n_experts], TOP_K)
    top_w = jax.nn.softmax(top_val, axis=-1)
    A = T * TOP_K
    e_flat = top_idx.reshape(A).astype(jnp.int32)
    onehot = (e_flat[:, None] == jnp.arange(n_experts, dtype=jnp.int32)[None, :]).astype(jnp.int32)
    csum = jnp.cumsum(onehot, axis=0)
    counts = csum[-1]
    rank = jnp.sum((csum - onehot) * onehot, axis=1)
    padded = (counts + rows - 1) // rows * rows
    pad_end = jnp.cumsum(padded)
    pad_start = pad_end - padded
    dest = pad_start[e_flat] + rank
    n_blocks = -(-A // rows) + n_experts
    P = n_blocks * rows
    order = jnp.argsort(e_flat)
    starts = jnp.cumsum(counts) - counts
    slot = jnp.arange(P, dtype=jnp.int32)
    slot_e = jnp.minimum(jnp.sum((slot[:, None] >= pad_end[None, :]).astype(jnp.int32), axis=1),
                         n_experts - 1)
    within = slot - pad_start[slot_e]
    valid = within < counts[slot_e]
    src = jnp.clip(starts[slot_e] + within, 0, A - 1)
    tok_of_slot = jnp.where(valid, order[src] // TOP_K, 0).astype(jnp.int32)
    block_e = slot_e[::rows]
    n_used = (pad_end[-1] // rows).astype(jnp.int32).reshape(1)
    return top_w.astype(F32), dest.astype(jnp.int32), tok_of_slot, block_e, n_used


def _pad_ff(w, axis):
    F = w.shape[axis]
    pad = [(0, 0)] * w.ndim
    pad[axis] = (0, _round_up(F, 2 * LANES) - F)
    return jnp.pad(w.astype(BF16), pad)


def kernel(x, norm_mix, w_in, conv_w, w_f1, b_f1, freq1, w_f2, b_f2, freq2, w_f3, hy_bias, q_norm, k_norm, lam_q1, lam_k1, lam_q2, lam_k2, subln_g, w_hy_o, w_at_o, w_out, norm_ffn, w_dense_gate, w_dense_up, w_dense_down, w_router, w_moe_gate, w_moe_up, w_moe_down):
    B, L, D = x.shape
    T = B * L
    depth = w_in.shape[0]
    hw = w_hy_o.shape[1]
    qk_w = N_HEADS * 2 * HEAD_DIM
    at_w = N_HEADS * V_DIM
    qcol = 3 * hw // qk_w
    kcol = qcol + 1
    vcol = (3 * hw + 2 * qk_w) // V_DIM
    gcol = (3 * hw + 2 * qk_w + at_w) // D
    mats = _dft_matrices(L)

    x2 = x.reshape(T, D)
    for layer in range(depth):
        lambda_init = 0.8 - 0.6 * math.exp(-0.3 * layer)
        proj = _inproj(x2, norm_mix[layer].astype(F32), w_in[layer].astype(BF16))
        proj3 = proj.reshape(B, L, -1)
        u3 = _shortconv(proj3, conv_w[layer].astype(F32), hw)
        kern = _hyena_time_filters(L, w_f1[layer], b_f1[layer], freq1[layer], w_f2[layer], b_f2[layer],
                                   freq2[layer], w_f3[layer], hw)
        y_h = _hyena(u3, kern, hy_bias[layer], mats)
        qn, kn = _qk_prep(proj3, q_norm[layer], k_norm[layer], qcol, kcol, qk_w)
        lam = (jnp.exp(jnp.sum(lam_q1[layer].astype(F32) * lam_k1[layer].astype(F32)))
               - jnp.exp(jnp.sum(lam_q2[layer].astype(F32) * lam_k2[layer].astype(F32))) + lambda_init)
        y_a = _attention(qn, kn, proj3, vcol, lam, subln_g[layer], 1.0 - lambda_init)
        i = layer // 2
        moe = layer % 2 == 1
        outs = _merge(x2, y_h.reshape(hw // LANES, T // 2, LANES), y_a.reshape(T, at_w), proj, gcol,
                      w_hy_o[layer].astype(BF16), w_at_o[layer].astype(BF16), w_out[layer].astype(BF16),
                      norm_ffn[layer], w_router[i] if moe else None)
        if not moe:
            x2, hn = outs
            x2 = _dense_ffn(hn, x2, _pad_ff(w_dense_gate[i], 1), _pad_ff(w_dense_up[i], 1),
                            _pad_ff(w_dense_down[i], 0))
        else:
            x2, hn, logits = outs
            n_experts = w_router.shape[2]
            top_w, dest, tok_of_slot, block_e, n_used = _route(logits, n_experts, MOE_ROWS)
            yp = _experts(block_e, n_used, tok_of_slot, hn, _pad_ff(w_moe_gate[i], 2),
                          _pad_ff(w_moe_up[i], 2), _pad_ff(w_moe_down[i], 1))
            x2 = _combine(dest, yp, x2, top_w)
    return x2.reshape(B, L, D)
```

```python
import functools
import math

import numpy as np
import jax
import jax.numpy as jnp
from jax import lax
from jax.experimental import pallas as pl
from jax.experimental.pallas import tpu as pltpu

F32 = jnp.float32
BF16 = jnp.bfloat16
U32 = jnp.uint32
HIGHEST = lax.Precision.HIGHEST

N_HEADS = 4
HEAD_DIM = 64
V_DIM = 2 * HEAD_DIM
ROPE_THETA = 10000.0
FILTER_BANDS = 16
HYENA_TARGET = 1e-2
FAST_DECAY_PCT = 0.3
SLOW_DECAY_PCT = 1.5
TOP_K = 2
RMS_EPS = 1e-6

LANES = 128
BF16_SUBLANES = 16
VMEM_LIMIT = 56 * 1024 * 1024

DFT_N1 = 128
DFT_PAIRS = 16
DFT_LANE_BLOCKS = 2
MOE_ROWS = 512


def _cparams(sem):
    return pltpu.CompilerParams(dimension_semantics=sem, vmem_limit_bytes=VMEM_LIMIT)


def _round_up(a, b):
    return (a + b - 1) // b * b


def _inproj_kernel(x_ref, g_ref, w_ref, o_ref, hn_ref):
    @pl.when(pl.program_id(1) == 0)
    def _():
        x = x_ref[...]
        ms = jnp.mean(x * x, axis=-1, keepdims=True)
        hn_ref[...] = (x * lax.rsqrt(ms + RMS_EPS) * g_ref[...]).astype(BF16)

    o_ref[...] = jnp.dot(hn_ref[...], w_ref[...], preferred_element_type=F32).astype(o_ref.dtype)


def _inproj(x2, g, w):
    T, D = x2.shape
    N = w.shape[1]
    tm = min(1024, T)
    tn = 1024
    return pl.pallas_call(
        _inproj_kernel,
        grid=(T // tm, N // tn),
        in_specs=[pl.BlockSpec((tm, D), lambda i, j: (i, 0)),
                  pl.BlockSpec((1, D), lambda i, j: (0, 0)),
                  pl.BlockSpec((D, tn), lambda i, j: (0, j))],
        out_specs=pl.BlockSpec((tm, tn), lambda i, j: (i, j)),
        out_shape=jax.ShapeDtypeStruct((T, N), BF16),
        scratch_shapes=[pltpu.VMEM((tm, D), BF16)],
        compiler_params=_cparams(("parallel", "arbitrary")),
        name="inproj",
    )(x2, g.reshape(1, D), w)


def _shortconv_kernel(u_ref, p_ref, n_ref, w_ref, o_ref):
    i = pl.program_id(1)
    last = pl.num_programs(1) - 1
    u = u_ref[0].astype(F32)
    tl = u.shape[0]
    prev_row = p_ref[0].astype(F32)[BF16_SUBLANES - 1:BF16_SUBLANES]
    next_row = n_ref[0].astype(F32)[0:1]
    prev_row = jnp.where(i > 0, prev_row, 0.0)
    next_row = jnp.where(i < last, next_row, 0.0)
    row = lax.broadcasted_iota(jnp.int32, u.shape, 0)
    up = jnp.where(row == 0, prev_row, pltpu.roll(u, 1, 0))
    un = jnp.where(row == tl - 1, next_row, pltpu.roll(u, tl - 1, 0))
    w = w_ref[...]
    res = (up * w[0:1] + u * w[1:2] + un * w[2:3]).astype(BF16)
    pk = pltpu.bitcast(res, U32)
    for c in range(o_ref.shape[1]):
        o_ref[0, c, 0] = pk[:, c * LANES:(c + 1) * LANES]


def _shortconv(proj3, conv_w, hw):
    B, L, _ = proj3.shape
    tl = min(1024, L)
    hb = tl // BF16_SUBLANES
    nhb = L // BF16_SUBLANES
    nc = hw // LANES
    return pl.pallas_call(
        _shortconv_kernel,
        grid=(B, L // tl, 3),
        in_specs=[pl.BlockSpec((1, tl, hw), lambda b, i, c: (b, i, c)),
                  pl.BlockSpec((1, BF16_SUBLANES, hw),
                               lambda b, i, c: (b, jnp.maximum(i * hb - 1, 0), c)),
                  pl.BlockSpec((1, BF16_SUBLANES, hw),
                               lambda b, i, c: (b, jnp.minimum((i + 1) * hb, nhb - 1), c)),
                  pl.BlockSpec((3, hw), lambda b, i, c: (0, c))],
        out_specs=pl.BlockSpec((1, nc, 1, tl // 2, LANES), lambda b, i, c: (c, 0, b, i, 0)),
        out_shape=jax.ShapeDtypeStruct((3, nc, B, L // 2, LANES), U32),
        compiler_params=_cparams(("parallel", "parallel", "arbitrary")),
        name="shortconv",
    )(proj3, proj3, proj3, conv_w)


def _filter_kernel(zz_ref, w1_ref, b1_ref, f1_ref, w2_ref, b2_ref, f2_ref, w3_ref, dl_ref, o_ref):
    zz = zz_ref[...]
    h = jnp.dot(zz, w1_ref[...], precision=HIGHEST, preferred_element_type=F32)
    h = jnp.sin(f1_ref[...] * (h + b1_ref[...]))
    h = jnp.dot(h, w2_ref[...], precision=HIGHEST, preferred_element_type=F32)
    h = jnp.sin(f2_ref[...] * (h + b2_ref[...]))
    hf = jnp.dot(h, w3_ref[0], precision=HIGHEST, preferred_element_type=F32)
    t = zz[:, 0:1]
    mask = zz[:, _MASK_COL:_MASK_COL + 1]
    o_ref[...] = hf * (jnp.exp(-t * dl_ref[...]) * mask)


_MASK_COL = 2 * FILTER_BANDS + 1


def _filter_positions(L):
    bands = FILTER_BANDS
    t = np.linspace(0.0, 1.0, L, dtype=np.float64)[:, None]
    w = 2.0 * math.pi * np.arange(L, dtype=np.float64)[:, None] / L
    f = np.linspace(1e-4, bands - 1, bands, dtype=np.float64)[None, :]
    z = np.concatenate([t, np.cos(f * w), -np.sin(f * w)], axis=-1)
    zz = np.zeros((2 * L, LANES), np.float64)
    zz[:L, :z.shape[1]] = z
    zz[L + 1:, :z.shape[1]] = z[1:][::-1]
    zz[:, _MASK_COL] = 1.0
    zz[L, :] = 0.0
    return zz.astype(np.float32)


def _hyena_time_filters(L, w_f1, b_f1, freq1, w_f2, b_f2, freq2, w_f3, hw):
    emb, hid = w_f1.shape
    zz = jnp.asarray(_filter_positions(L))
    w1 = jnp.zeros((LANES, hid), F32).at[:emb].set(w_f1)
    w3 = w_f3.reshape(hid, 2, 2, hw).transpose(2, 0, 1, 3).reshape(2, hid, 2 * hw)
    max_decay = math.log(HYENA_TARGET) / FAST_DECAY_PCT
    min_decay = math.log(HYENA_TARGET) / SLOW_DECAY_PCT
    deltas = np.abs(np.linspace(min_decay, max_decay, hw, dtype=np.float64))
    dl = jnp.asarray(np.tile(deltas, 2)[None, :], F32)
    tr = min(512, L)
    per_dir = L // tr
    row = lambda a: a.reshape(1, -1)
    return pl.pallas_call(
        _filter_kernel,
        grid=(2 * L // tr,),
        in_specs=[pl.BlockSpec((tr, LANES), lambda i: (i, 0)),
                  pl.BlockSpec((LANES, hid), lambda i: (0, 0)),
                  pl.BlockSpec((1, hid), lambda i: (0, 0)),
                  pl.BlockSpec((1, hid), lambda i: (0, 0)),
                  pl.BlockSpec((hid, hid), lambda i: (0, 0)),
                  pl.BlockSpec((1, hid), lambda i: (0, 0)),
                  pl.BlockSpec((1, hid), lambda i: (0, 0)),
                  pl.BlockSpec((1, hid, 2 * hw), lambda i: (i // per_dir, 0, 0)),
                  pl.BlockSpec((1, 2 * hw), lambda i: (0, 0))],
        out_specs=pl.BlockSpec((tr, 2 * hw), lambda i: (i, 0)),
        out_shape=jax.ShapeDtypeStruct((2 * L, 2 * hw), F32),
        compiler_params=_cparams(("parallel",)),
        name="hyena_filter",
    )(zz, w1, row(b_f1), row(freq1), w_f2, row(b_f2), row(freq2), w3, dl)


def _dft_matrices(L):
    N = 2 * L
    N1 = DFT_N1
    N2 = N // N1
    h = N2 // 2
    ang = 2.0 * np.pi * np.outer(np.arange(N2), np.arange(N2)) / N2
    c, s = np.cos(ang), np.sin(ang)
    m1 = np.zeros((N2, 2, N2))
    m1[:, 0, :h], m1[:, 0, h:] = c[:, :h], s[:, :h]
    m1[:, 1, :h], m1[:, 1, h:] = -s[:, :h], c[:, :h]
    m1 = m1.reshape(2 * N2, N2)
    m1f = np.stack([c, -s], axis=1).reshape(2 * N2, N2)
    m3 = np.zeros((2, h, N2, 2))
    m3[0, :, :, 0], m3[0, :, :, 1] = c[:h], -s[:h]
    m3[1, :, :, 0], m3[1, :, :, 1] = s[:h], c[:h]
    m3 = (m3 / N).reshape(N2, 2 * N2)
    k2 = jnp.arange(N2, dtype=jnp.int32)[:, None, None]
    k1 = jnp.arange(N1, dtype=jnp.int32)[None, :, None]
    n1 = jnp.arange(N1, dtype=jnp.int32)[None, None, :]
    m = (n1 * (N2 * k1 + k2)) % N
    th = m.astype(F32) * F32(2.0 * math.pi / N)
    C, S = jnp.cos(th), jnp.sin(th)
    m2 = jnp.concatenate([jnp.concatenate([C, S], axis=2),
                          jnp.concatenate([-S, C], axis=2)], axis=1)
    return (jnp.asarray(m1, F32), jnp.asarray(m1f, F32), m2, jnp.asarray(m3, F32))


def _dft_a_kernel(m_ref, z_ref, o_ref, *, precision):
    o_ref[0] = jnp.dot(m_ref[...], z_ref[0], precision=precision,
                       preferred_element_type=F32).astype(o_ref.dtype)


def _dft_a(m, z, out_dtype, precision=None):
    P, K, LN = z.shape
    R = m.shape[0]
    lt = min(8192, LN)
    return pl.pallas_call(
        functools.partial(_dft_a_kernel, precision=precision),
        grid=(P, LN // lt),
        in_specs=[pl.BlockSpec((R, K), lambda p, j: (0, 0)),
                  pl.BlockSpec((1, K, lt), lambda p, j: (p, 0, j))],
        out_specs=pl.BlockSpec((1, R, lt), lambda p, j: (p, 0, j)),
        out_shape=jax.ShapeDtypeStruct((P, R, LN), out_dtype),
        compiler_params=_cparams(("parallel", "parallel")),
        name="dft_a",
    )(m, z)


def _dft_filt_kernel(a_ref, m_ref, o_ref):
    o_ref[0] = jnp.dot(m_ref[0], a_ref[0], precision=HIGHEST, preferred_element_type=F32)


def _dft_filt(a, m2):
    N2, R, W = a.shape
    return pl.pallas_call(
        _dft_filt_kernel,
        grid=(N2,),
        in_specs=[pl.BlockSpec((1, R, W), lambda k: (k, 0, 0)),
                  pl.BlockSpec((1, R, R), lambda k: (k, 0, 0))],
        out_specs=pl.BlockSpec((1, R, W), lambda k: (k, 0, 0)),
        out_shape=jax.ShapeDtypeStruct((N2, R, W), F32),
        compiler_params=_cparams(("parallel",)),
        name="dft_filter_mid",
    )(a, m2)


def _packed_rows(ref, lead, start, size, stride):
    cols = [jnp.concatenate([ref[(c,) + l + (pl.ds(start, size, stride=stride), slice(None))] for l in lead], axis=0)
            for c in range(ref.shape[0])]
    return pltpu.bitcast(jnp.concatenate(cols, axis=1), BF16)


def _dft_a_pk_kernel(m_ref, z_ref, o_ref, *, tm, nh, npairs):
    j0 = pl.program_id(2) * tm
    rows = m_ref.shape[0] // 2
    for j in range(tm):
        rhs = _packed_rows(z_ref, [(0,), (1,)], j0 + j, nh, npairs)
        res = jnp.dot(m_ref[...], rhs, preferred_element_type=F32)
        pk = pltpu.bitcast(res.astype(BF16), U32)
        for c in range(o_ref.shape[0]):
            o_ref[c, 0, 0, pl.ds(j, rows, stride=tm), :] = pk[:, c * LANES:(c + 1) * LANES]


def _dft_a_pk(m1x, z, slot):
    _, NC, B, L2, _ = z.shape
    npairs = DFT_N1 // 2
    nh = L2 // npairs
    tm, ncb = DFT_PAIRS, DFT_LANE_BLOCKS
    rows = m1x.shape[0] // 2
    P = B // 2
    return pl.pallas_call(
        functools.partial(_dft_a_pk_kernel, tm=tm, nh=nh, npairs=npairs),
        grid=(P, NC // ncb, npairs // tm),
        in_specs=[pl.BlockSpec(m1x.shape, lambda p, h, j: (0, 0)),
                  pl.BlockSpec((None, ncb, 2, L2, LANES), lambda p, h, j: (slot, h, p, 0, 0))],
        out_specs=pl.BlockSpec((ncb, 1, 1, rows * tm, LANES), lambda p, h, j: (h, p, j, 0, 0)),
        out_shape=jax.ShapeDtypeStruct((NC, P, npairs // tm, rows * tm, LANES), U32),
        compiler_params=_cparams(("parallel", "parallel", "arbitrary")),
        name="dft_a",
    )(m1x, z)


def _dft_mid_kernel(a_ref, m_ref, mt_ref, kf_ref, o_ref):
    m = m_ref[0]
    mt = mt_ref[0]
    n1 = m.shape[0] // 2
    kr = kf_ref[0, :n1]
    ki = kf_ref[0, n1:]
    nc, npair = a_ref.shape[0], a_ref.shape[1]
    blk = a_ref.shape[2], a_ref.shape[4], LANES
    for p in range(npair):
        cols = [a_ref[c, p, :, 0].reshape(n1, LANES) for c in range(nc)]
        a = pltpu.bitcast(jnp.concatenate(cols, axis=1), BF16)
        x = jnp.dot(m, a, preferred_element_type=F32)
        xr = x[:n1]
        xi = x[n1:]
        y = jnp.concatenate([xr * kr - xi * ki, xr * ki + xi * kr], axis=0).astype(BF16)
        pk = pltpu.bitcast(jnp.dot(mt, y, preferred_element_type=F32).astype(BF16), U32)
        for c in range(nc):
            o_ref[c, p, :, 0] = pk[:, c * LANES:(c + 1) * LANES].reshape(blk)


def _dft_mid(a, m2, m2t, kf, order):
    NC, P, G, N2, W, _ = a.shape
    R = m2.shape[1]
    C = NC * LANES
    blk = pl.BlockSpec((NC, P, G, 1, W, LANES), lambda k: (0, 0, 0, k, 0, 0))
    return pl.pallas_call(
        _dft_mid_kernel,
        grid=(N2,),
        in_specs=[blk,
                  pl.BlockSpec((1, R, R), lambda k: (k, 0, 0)),
                  pl.BlockSpec((1, R, R), lambda k: (k, 0, 0)),
                  pl.BlockSpec((1, R, C), lambda k: (k, 0, order))],
        out_specs=blk,
        out_shape=jax.ShapeDtypeStruct(a.shape, U32),
        compiler_params=_cparams(("parallel",)),
        name="dft_mid",
    )(a, m2, m2t, kf)


def _dft_c_pk_kernel(m_ref, b_ref, z_ref, g_ref, bias_ref, o_ref, *, tm, nh, npairs):
    j0 = pl.program_id(2) * tm
    rows = b_ref.shape[3] // tm
    bias = bias_ref[...]
    both = [(0,), (1,)]
    for j in range(tm):
        rhs = _packed_rows(b_ref, [(0, 0)], j, rows, tm)
        y = jnp.dot(m_ref[...], rhs, preferred_element_type=F32)
        z = _packed_rows(z_ref, both, j0 + j, nh, npairs).astype(F32)
        g = _packed_rows(g_ref, both, j0 + j, nh, npairs).astype(F32)
        pk = pltpu.bitcast((g * (y + z * bias)).astype(BF16), U32)
        for c in range(o_ref.shape[0]):
            for b in range(2):
                o_ref[c, b, pl.ds(j0 + j, nh, stride=npairs), :] = pk[b * nh:(b + 1) * nh, c * LANES:(c + 1) * LANES]


def _dft_c_pk(m3x, b, z, zslot, g, gslot, bias):
    NC, P, G, W, _ = b.shape
    L2 = z.shape[3]
    npairs = DFT_N1 // 2
    nh = L2 // npairs
    tm, ncb = DFT_PAIRS, DFT_LANE_BLOCKS
    seq = lambda slot: pl.BlockSpec((None, ncb, 2, L2, LANES), lambda p, h, j: (slot, h, p, 0, 0))
    return pl.pallas_call(
        functools.partial(_dft_c_pk_kernel, tm=tm, nh=nh, npairs=npairs),
        grid=(P, NC // ncb, G),
        in_specs=[pl.BlockSpec(m3x.shape, lambda p, h, j: (0, 0)),
                  pl.BlockSpec((ncb, 1, 1, W, LANES), lambda p, h, j: (h, p, j, 0, 0)),
                  seq(zslot), seq(gslot),
                  pl.BlockSpec((1, ncb * LANES), lambda p, h, j: (0, h))],
        out_specs=seq(0),
        out_shape=jax.ShapeDtypeStruct((1, NC, 2 * P, L2, LANES), U32),
        compiler_params=_cparams(("parallel", "parallel", "arbitrary")),
        name="dft_c",
    )(m3x, b, z, g, bias)


def _mid_row_order(tm):
    N1 = DFT_N1
    idx = []
    for mt in range(N1 // 2 // tm):
        for ri in range(2):
            for jj in range(tm):
                for parity in range(2):
                    idx.append(ri * N1 + 2 * (mt * tm + jj) + parity)
    return np.asarray(idx, np.int32)


def _hyena(u3, kern, hy_bias, mats):
    _, NC, B, L2, _ = u3.shape
    C = NC * LANES
    m1, m1f, m2, m3 = mats
    N1 = DFT_N1
    N2 = 4 * L2 // N1
    tm = DFT_PAIRS
    af = _dft_a(m1f, kern.reshape(1, N2, N1 * 2 * C), F32, precision=HIGHEST)
    kf = _dft_filt(af.reshape(N2, 2 * N1, 2 * C), m2)
    eye2 = jnp.eye(2, dtype=F32)
    m1x = jnp.kron(m1, eye2).astype(BF16)
    m3x = jnp.kron(m3, eye2).astype(BF16)
    m2p = jnp.take(m2, jnp.asarray(_mid_row_order(tm)), axis=2).astype(BF16)
    m2tp = jnp.swapaxes(m2p, 1, 2)
    bias = hy_bias.astype(F32)

    def long_conv(z, zslot, gslot, order):
        a = _dft_a_pk(m1x, z, zslot)
        NCa, P, G, W, _ = a.shape
        b = _dft_mid(a.reshape(NCa, P, G, N2, W // N2, LANES), m2p, m2tp, kf, order)
        return _dft_c_pk(m3x, b.reshape(a.shape), z, zslot, u3, gslot, bias[order][None, :])

    z = long_conv(u3, 2, 0, 0)
    return long_conv(z, 0, 1, 1)


def _qk_kernel(q_ref, k_ref, cos_ref, sin_ref, qg_ref, kg_ref, qo_ref, ko_ref, *, scale):
    cos = cos_ref[...]
    sin = sin_ref[...]
    lane = lax.broadcasted_iota(jnp.int32, cos.shape, 1)
    lo = lane < HEAD_DIM
    first = (lane % HEAD_DIM) < (HEAD_DIM // 2)

    def one(x_ref, g_ref, o_ref, mult):
        for h in range(x_ref.shape[2] // LANES):
            x = x_ref[0, :, h * LANES:(h + 1) * LANES].astype(F32)
            x2 = x * x
            s_lo = jnp.sum(jnp.where(lo, x2, 0.0), axis=-1, keepdims=True)
            s_hi = jnp.sum(jnp.where(lo, 0.0, x2), axis=-1, keepdims=True)
            r = lax.rsqrt(jnp.where(lo, s_lo, s_hi) * (1.0 / HEAD_DIM) + RMS_EPS)
            xn = x * r * g_ref[...]
            partner = jnp.where(first, pltpu.roll(xn, LANES - HEAD_DIM // 2, 1),
                                pltpu.roll(xn, HEAD_DIM // 2, 1))
            o_ref[0, :, h * LANES:(h + 1) * LANES] = ((xn * cos + partner * sin) * mult).astype(o_ref.dtype)

    one(q_ref, qg_ref, qo_ref, scale)
    one(k_ref, kg_ref, ko_ref, 1.0)


def _rope_tables(L):
    half = HEAD_DIM // 2
    inv = 1.0 / (ROPE_THETA ** (np.arange(0, HEAD_DIM, 2, dtype=np.float64) / HEAD_DIM))
    ang = np.arange(L, dtype=np.float64)[:, None] * inv[None, :]
    cos, sin = np.cos(ang), np.sin(ang)
    cos_t = np.tile(cos, (1, LANES // half))
    sin_t = np.tile(np.concatenate([-sin, sin], axis=1), (1, LANES // HEAD_DIM))
    return jnp.asarray(cos_t, F32), jnp.asarray(sin_t, F32)


def _qk_prep(proj3, q_norm, k_norm, qcol, kcol, width):
    B, L, _ = proj3.shape
    tl = min(512, L)
    cos_t, sin_t = _rope_tables(L)
    qg = jnp.tile(q_norm.astype(F32), LANES // HEAD_DIM)[None, :]
    kg = jnp.tile(k_norm.astype(F32), LANES // HEAD_DIM)[None, :]
    blk = lambda col: pl.BlockSpec((1, tl, width), lambda b, i: (b, i, col))
    tab = pl.BlockSpec((tl, LANES), lambda b, i: (i, 0))
    vec = pl.BlockSpec((1, LANES), lambda b, i: (0, 0))
    out = pl.BlockSpec((1, tl, width), lambda b, i: (b, i, 0))
    return pl.pallas_call(
        functools.partial(_qk_kernel, scale=HEAD_DIM ** -0.5 * math.log2(math.e)),
        grid=(B, L // tl),
        in_specs=[blk(qcol), blk(kcol), tab, tab, vec, vec],
        out_specs=[out, out],
        out_shape=[jax.ShapeDtypeStruct((B, L, width), BF16)] * 2,
        compiler_params=_cparams(("parallel", "parallel")),
        name="qk_norm_rope",
    )(proj3, proj3, cos_t, sin_t, qg, kg)


SHIFT_LIMIT = 50.0


def _attn_kernel(lam_ref, q_ref, k_ref, v_ref, g_ref, o_ref, kmax_ref, *, post):
    k = k_ref[0]
    v = v_ref[0]

    @pl.when(pl.program_id(2) == 0)
    def _():
        kmax_ref[...] = jnp.max(jnp.abs(k.astype(F32)), axis=0, keepdims=True)

    q = q_ref[0]
    lane = lax.broadcasted_iota(jnp.int32, q.shape, 1)
    lo = lane < HEAD_DIM
    zero = jnp.zeros_like(q)
    reach = jnp.abs(q.astype(F32)) * kmax_ref[...]
    shifts = [jnp.sum(jnp.where(lo, reach, 0.0), axis=-1, keepdims=True),
              jnp.sum(jnp.where(lo, 0.0, reach), axis=-1, keepdims=True)]
    worst = jnp.max(jnp.maximum(shifts[0], shifts[1]))

    def run(row_max):
        outs = []
        for c in range(2):
            qc = jnp.where(lo, q, zero) if c == 0 else jnp.where(lo, zero, q)
            s = lax.dot_general(qc, k, (((1,), (1,)), ((), ())), preferred_element_type=F32)
            m = jnp.max(s, axis=-1, keepdims=True) if row_max else shifts[c]
            p = jnp.exp2(s - m)
            l = jnp.sum(p, axis=-1, keepdims=True)
            outs.append(jnp.dot(p.astype(BF16), v, preferred_element_type=F32) / l)
        a = outs[0] - lam_ref[0] * outs[1]
        ms = jnp.mean(a * a, axis=-1, keepdims=True)
        o_ref[0] = (a * lax.rsqrt(ms + RMS_EPS) * g_ref[...] * post).astype(o_ref.dtype)

    @pl.when(worst <= SHIFT_LIMIT)
    def _():
        run(False)

    @pl.when(jnp.logical_not(worst <= SHIFT_LIMIT))
    def _():
        run(True)


def _attention(qn, kn, proj3, vcol, lam, subln_g, post):
    B, L, W = qn.shape
    tq = min(512, L)
    return pl.pallas_call(
        functools.partial(_attn_kernel, post=post),
        grid=(B, W // V_DIM, L // tq),
        in_specs=[pl.BlockSpec(memory_space=pltpu.SMEM),
                  pl.BlockSpec((1, tq, V_DIM), lambda b, h, i: (b, i, h)),
                  pl.BlockSpec((1, L, V_DIM), lambda b, h, i: (b, 0, h)),
                  pl.BlockSpec((1, L, V_DIM), lambda b, h, i: (b, 0, vcol + h)),
                  pl.BlockSpec((1, V_DIM), lambda b, h, i: (0, 0))],
        out_specs=pl.BlockSpec((1, tq, V_DIM), lambda b, h, i: (b, i, h)),
        out_shape=jax.ShapeDtypeStruct((B, L, W), BF16),
        scratch_shapes=[pltpu.VMEM((1, V_DIM), F32)],
        compiler_params=_cparams(("parallel", "parallel", "arbitrary")),
        name="diff_attention",
    )(lam.reshape(1).astype(F32), qn, kn, proj3, subln_g.astype(F32).reshape(1, V_DIM))


def _merge_kernel(x_ref, yh_ref, ya_ref, gh_ref, ga_ref, who_ref, wao_ref, wout_ref, g_ref, *rest,
                  with_router):
    if with_router:
        wr_ref, xo_ref, hn_ref, lg_ref = rest
    else:
        xo_ref, hn_ref = rest
    yh = jnp.concatenate([pltpu.bitcast(yh_ref[c], BF16) for c in range(yh_ref.shape[0])], axis=1)
    th = jnp.dot(yh, who_ref[...], preferred_element_type=F32)
    ta = jnp.dot(ya_ref[...], wao_ref[...], preferred_element_type=F32)
    merged = (jax.nn.sigmoid(gh_ref[...].astype(F32)) * th
              + jax.nn.sigmoid(ga_ref[...].astype(F32)) * ta)
    xn = x_ref[...] + jnp.dot(merged.astype(BF16), wout_ref[...], preferred_element_type=F32)
    xo_ref[...] = xn
    ms = jnp.mean(xn * xn, axis=-1, keepdims=True)
    h = xn * lax.rsqrt(ms + RMS_EPS) * g_ref[...]
    hn_ref[...] = h.astype(hn_ref.dtype)
    if with_router:
        lane = lax.broadcasted_iota(jnp.int32, lg_ref.shape, 1)
        lg = jnp.zeros(lg_ref.shape, F32)
        for e in range(wr_ref.shape[0]):
            lg = jnp.where(lane == e, jnp.sum(h * wr_ref[e:e + 1, :], axis=-1, keepdims=True), lg)
        lg_ref[...] = lg


def _merge(x2, yh, ya, proj, gcol, who, wao, wout, g_ffn, w_router):
    T, D = x2.shape
    W = ya.shape[1]
    tm = min(512, T)
    with_router = w_router is not None
    row = lambda w: pl.BlockSpec((tm, w), lambda i: (i, 0))
    full = lambda a: pl.BlockSpec(a.shape, lambda i: (0, 0))
    in_specs = [row(D), pl.BlockSpec((yh.shape[0], tm // 2, LANES), lambda i: (0, i, 0)), row(W),
                pl.BlockSpec((tm, D), lambda i: (i, gcol)),
                pl.BlockSpec((tm, D), lambda i: (i, gcol + 1)),
                full(who), full(wao), full(wout), pl.BlockSpec((1, D), lambda i: (0, 0))]
    args = [x2, yh, ya, proj, proj, who, wao, wout, g_ffn.astype(F32).reshape(1, D)]
    out_specs = [row(D), row(D)]
    out_shape = [jax.ShapeDtypeStruct((T, D), F32),
                 jax.ShapeDtypeStruct((T, D), F32 if with_router else BF16)]
    if with_router:
        wr = w_router.astype(F32).T
        in_specs.append(full(wr))
        args.append(wr)
        out_specs.append(row(LANES))
        out_shape.append(jax.ShapeDtypeStruct((T, LANES), F32))
    return pl.pallas_call(
        functools.partial(_merge_kernel, with_router=with_router),
        grid=(T // tm,),
        in_specs=in_specs,
        out_specs=out_specs,
        out_shape=out_shape,
        compiler_params=_cparams(("parallel",)),
        name="merge_outproj",
    )(*args)


def _swiglu(h, wg_ref, wu_ref, wd_ref, chunks):
    F = wg_ref.shape[-1]
    fc = F // chunks
    acc = None
    for c in range(chunks):
        sl = slice(c * fc, (c + 1) * fc)
        g = jnp.dot(h, wg_ref[:, sl], preferred_element_type=F32)
        u = jnp.dot(h, wu_ref[:, sl], preferred_element_type=F32)
        a = (g * jax.nn.sigmoid(g) * u).astype(BF16)
        d = jnp.dot(a, wd_ref[sl, :], preferred_element_type=F32)
        acc = d if acc is None else acc + d
    return acc


def _ffn_kernel(h_ref, x_ref, wg_ref, wu_ref, wd_ref, o_ref, *, chunks):
    o_ref[...] = x_ref[...] + _swiglu(h_ref[...], wg_ref, wu_ref, wd_ref, chunks)


def _ff_chunks(F):
    return F // (2 * LANES) if F % (2 * LANES) == 0 else 1


def _dense_ffn(hn, x2, wg, wu, wd):
    T, D = x2.shape
    F = wg.shape[1]
    tm = min(512, T)
    row = pl.BlockSpec((tm, D), lambda i: (i, 0))
    return pl.pallas_call(
        functools.partial(_ffn_kernel, chunks=_ff_chunks(F)),
        grid=(T // tm,),
        in_specs=[row, row,
                  pl.BlockSpec((D, F), lambda i: (0, 0)),
                  pl.BlockSpec((D, F), lambda i: (0, 0)),
                  pl.BlockSpec((F, D), lambda i: (0, 0))],
        out_specs=row,
        out_shape=jax.ShapeDtypeStruct((T, D), F32),
        compiler_params=_cparams(("parallel",)),
        name="dense_ffn",
    )(hn, x2, wg, wu, wd)


def _row_copy(src_hbm, row, dst, r, sem):
    return pltpu.make_async_copy(src_hbm.at[pl.ds(row, 1)], dst.at[pl.ds(r, 1)], sem)


def _rows_done(src_hbm, dst, sem):
    return pltpu.make_async_copy(src_hbm.at[pl.ds(0, dst.shape[0])], dst, sem)


def _expert_kernel(be_ref, tok_ref, hn_hbm, wg_ref, wu_ref, wd_ref, o_ref, xbuf, sem, *, chunks):
    i = pl.program_id(0)
    last = pl.num_programs(0) - 1
    rows = xbuf.shape[1]
    slot = i % 2
    nxt = 1 - slot

    @pl.when(i == 0)
    def _():
        def issue(r, carry):
            _row_copy(hn_hbm, tok_ref[r], xbuf.at[0], r, sem.at[0]).start()
            return carry

        lax.fori_loop(0, rows, issue, 0, unroll=8)

    _rows_done(hn_hbm, xbuf.at[slot], sem.at[slot]).wait()
    base = jnp.minimum(i + 1, last) * rows
    for r in range(rows):
        _row_copy(hn_hbm, tok_ref[base + r], xbuf.at[nxt], r, sem.at[nxt]).start()
    o_ref[...] = _swiglu(xbuf[slot].astype(BF16), wg_ref.at[0], wu_ref.at[0], wd_ref.at[0], chunks)

    @pl.when(i == last)
    def _():
        _rows_done(hn_hbm, xbuf.at[nxt], sem.at[nxt]).wait()


def _experts(block_e, tok_of_slot, hn, wg, wu, wd):
    T, D = hn.shape
    E, _, F = wg.shape
    P = tok_of_slot.shape[0]
    rows = MOE_ROWS
    wspec = lambda shape: pl.BlockSpec((1,) + shape, lambda i, be, tok: (be[i], 0, 0))
    return pl.pallas_call(
        functools.partial(_expert_kernel, chunks=_ff_chunks(F)),
        grid_spec=pltpu.PrefetchScalarGridSpec(
            num_scalar_prefetch=2,
            grid=(P // rows,),
            in_specs=[pl.BlockSpec(memory_space=pl.ANY),
                      wspec((D, F)), wspec((D, F)), wspec((F, D))],
            out_specs=pl.BlockSpec((rows, D), lambda i, be, tok: (i, 0)),
            scratch_shapes=[pltpu.VMEM((2, rows, D), F32), pltpu.SemaphoreType.DMA((2,))]),
        out_shape=jax.ShapeDtypeStruct((P, D), F32),
        compiler_params=_cparams(("arbitrary",)),
        name="moe_experts",
    )(block_e, tok_of_slot, hn, wg, wu, wd)


def _combine_kernel(dest_ref, yp_hbm, x_ref, w_ref, o_ref, buf0, buf1, sem):
    i = pl.program_id(0)
    rows = buf0.shape[1]

    def gather(tile, slot):
        base = tile * rows * TOP_K

        def issue(r, carry):
            _row_copy(yp_hbm, dest_ref[base + TOP_K * r], buf0.at[slot], r, sem.at[slot]).start()
            _row_copy(yp_hbm, dest_ref[base + TOP_K * r + 1], buf1.at[slot], r, sem.at[slot]).start()
            return carry

        lax.fori_loop(0, rows, issue, 0, unroll=8)

    @pl.when(i == 0)
    def _():
        gather(0, 0)

    @pl.when(i + 1 < pl.num_programs(0))
    def _():
        gather(i + 1, (i + 1) % 2)

    slot = i % 2
    _rows_done(yp_hbm, buf0.at[slot], sem.at[slot]).wait()
    _rows_done(yp_hbm, buf1.at[slot], sem.at[slot]).wait()
    w = w_ref[...]
    o_ref[...] = x_ref[...] + (buf0[slot] * w[:, 0:1] + buf1[slot] * w[:, 1:2])


def _combine(dest, yp, x2, top_w):
    T, D = x2.shape
    tm = min(256, T)
    return pl.pallas_call(
        _combine_kernel,
        grid_spec=pltpu.PrefetchScalarGridSpec(
            num_scalar_prefetch=1,
            grid=(T // tm,),
            in_specs=[pl.BlockSpec(memory_space=pl.ANY),
                      pl.BlockSpec((tm, D), lambda i, d: (i, 0)),
                      pl.BlockSpec((tm, TOP_K), lambda i, d: (i, 0))],
            out_specs=pl.BlockSpec((tm, D), lambda i, d: (i, 0)),
            scratch_shapes=[pltpu.VMEM((2, tm, D), F32), pltpu.VMEM((2, tm, D), F32),
                            pltpu.SemaphoreType.DMA((2,))]),
        out_shape=jax.ShapeDtypeStruct((T, D), F32),
        compiler_params=_cparams(("arbitrary",)),
        name="moe_combine",
    )(dest, yp, x2, top_w)


def _route(logits, n_experts, rows):
    T = logits.shape[0]
    top_val, top_idx = lax.top_k(logits[:, :n_experts], TOP_K)
    top_w = jax.nn.softmax(top_val, axis=-1)
    A = T * TOP_K
    e_flat = top_idx.reshape(A).astype(jnp.int32)
    onehot = (e_flat[:, None] == jnp.arange(n_experts, dtype=jnp.int32)[None, :]).astype(jnp.int32)
    csum = jnp.cumsum(onehot, axis=0)
    counts = csum[-1]
    rank = jnp.sum((csum - onehot) * onehot, axis=1)
    padded = (counts + rows - 1) // rows * rows
    pad_end = jnp.cumsum(padded)
    pad_start = pad_end - padded
    dest = pad_start[e_flat] + rank
    n_blocks = -(-A // rows) + n_experts
    P = n_blocks * rows
    order = jnp.argsort(e_flat)
    starts = jnp.cumsum(counts) - counts
    slot = jnp.arange(P, dtype=jnp.int32)
    slot_e = jnp.minimum(jnp.sum((slot[:, None] >= pad_end[None, :]).astype(jnp.int32), axis=1),
                         n_experts - 1)
    within = slot - pad_start[slot_e]
    valid = within < counts[slot_e]
    src = jnp.clip(starts[slot_e] + within, 0, A - 1)
    tok_of_slot = jnp.where(valid, order[src] // TOP_K, 0).astype(jnp.int32)
    block_e = slot_e[::rows]
    return top_w.astype(F32), dest.astype(jnp.int32), tok_of_slot, block_e


def _pad_ff(w, axis):
    F = w.shape[axis]
    pad = [(0, 0)] * w.ndim
    pad[axis] = (0, _round_up(F, 2 * LANES) - F)
    return jnp.pad(w.astype(BF16), pad)


def kernel(x, norm_mix, w_in, conv_w, w_f1, b_f1, freq1, w_f2, b_f2, freq2, w_f3, hy_bias, q_norm, k_norm, lam_q1, lam_k1, lam_q2, lam_k2, subln_g, w_hy_o, w_at_o, w_out, norm_ffn, w_dense_gate, w_dense_up, w_dense_down, w_router, w_moe_gate, w_moe_up, w_moe_down):
    B, L, D = x.shape
    T = B * L
    depth = w_in.shape[0]
    hw = w_hy_o.shape[1]
    qk_w = N_HEADS * 2 * HEAD_DIM
    at_w = N_HEADS * V_DIM
    qcol = 3 * hw // qk_w
    kcol = qcol + 1
    vcol = (3 * hw + 2 * qk_w) // V_DIM
    gcol = (3 * hw + 2 * qk_w + at_w) // D
    mats = _dft_matrices(L)

    x2 = x.reshape(T, D)
    for layer in range(depth):
        lambda_init = 0.8 - 0.6 * math.exp(-0.3 * layer)
        proj = _inproj(x2, norm_mix[layer].astype(F32), w_in[layer].astype(BF16))
        proj3 = proj.reshape(B, L, -1)
        u3 = _shortconv(proj3, conv_w[layer].astype(F32), hw)
        kern = _hyena_time_filters(L, w_f1[layer], b_f1[layer], freq1[layer], w_f2[layer], b_f2[layer],
                                   freq2[layer], w_f3[layer], hw)
        y_h = _hyena(u3, kern, hy_bias[layer], mats)
        qn, kn = _qk_prep(proj3, q_norm[layer], k_norm[layer], qcol, kcol, qk_w)
        lam = (jnp.exp(jnp.sum(lam_q1[layer].astype(F32) * lam_k1[layer].astype(F32)))
               - jnp.exp(jnp.sum(lam_q2[layer].astype(F32) * lam_k2[layer].astype(F32))) + lambda_init)
        y_a = _attention(qn, kn, proj3, vcol, lam, subln_g[layer], 1.0 - lambda_init)
        i = layer // 2
        moe = layer % 2 == 1
        outs = _merge(x2, y_h.reshape(hw // LANES, T // 2, LANES), y_a.reshape(T, at_w), proj, gcol,
                      w_hy_o[layer].astype(BF16), w_at_o[layer].astype(BF16), w_out[layer].astype(BF16),
                      norm_ffn[layer], w_router[i] if moe else None)
        if not moe:
            x2, hn = outs
            x2 = _dense_ffn(hn, x2, _pad_ff(w_dense_gate[i], 1), _pad_ff(w_dense_up[i], 1),
                            _pad_ff(w_dense_down[i], 0))
        else:
            x2, hn, logits = outs
            n_experts = w_router.shape[2]
            top_w, dest, tok_of_slot, block_e = _route(logits, n_experts, MOE_ROWS)
            yp = _experts(block_e, tok_of_slot, hn, _pad_ff(w_moe_gate[i], 2),
                          _pad_ff(w_moe_up[i], 2), _pad_ff(w_moe_down[i], 1))
            x2 = _combine(dest, yp, x2, top_w)
    return x2.reshape(B, L, D)
```

```python
import functools
import math

import numpy as np
import jax
import jax.numpy as jnp
from jax import lax
from jax.experimental import pallas as pl
from jax.experimental.pallas import tpu as pltpu

F32 = jnp.float32
BF16 = jnp.bfloat16
U32 = jnp.uint32

N_HEADS = 4
HEAD_DIM = 64
V_DIM = 2 * HEAD_DIM
ROPE_THETA = 10000.0
FILTER_BANDS = 16
HYENA_TARGET = 1e-2
FAST_DECAY_PCT = 0.3
SLOW_DECAY_PCT = 1.5
TOP_K = 2
RMS_EPS = 1e-6

LANES = 128
BF16_SUBLANES = 16
VMEM_LIMIT = 56 * 1024 * 1024

DFT_N1 = 128
DFT_PAIRS = 16
DFT_FILTER_STEP = 16
DFT_LANE_BLOCKS = 2
MOE_ROWS = 512


def _cparams(sem):
    return pltpu.CompilerParams(dimension_semantics=sem, vmem_limit_bytes=VMEM_LIMIT)


def _round_up(a, b):
    return (a + b - 1) // b * b


def _inproj_kernel(x_ref, g_ref, w_ref, o_ref, hn_ref):
    @pl.when(pl.program_id(1) == 0)
    def _():
        x = x_ref[...]
        ms = jnp.mean(x * x, axis=-1, keepdims=True)
        hn_ref[...] = (x * lax.rsqrt(ms + RMS_EPS) * g_ref[...]).astype(BF16)

    o_ref[...] = jnp.dot(hn_ref[...], w_ref[...], preferred_element_type=F32).astype(o_ref.dtype)


def _inproj(x2, g, w):
    T, D = x2.shape
    N = w.shape[1]
    tm = min(512, T)
    tn = N
    return pl.pallas_call(
        _inproj_kernel,
        grid=(T // tm, N // tn),
        in_specs=[pl.BlockSpec((tm, D), lambda i, j: (i, 0)),
                  pl.BlockSpec((1, D), lambda i, j: (0, 0)),
                  pl.BlockSpec((D, tn), lambda i, j: (0, j))],
        out_specs=pl.BlockSpec((tm, tn), lambda i, j: (i, j)),
        out_shape=jax.ShapeDtypeStruct((T, N), BF16),
        scratch_shapes=[pltpu.VMEM((tm, D), BF16)],
        compiler_params=_cparams(("parallel", "arbitrary")),
        name="inproj",
    )(x2, g.reshape(1, D), w)


def _shortconv_kernel(u_ref, p_ref, n_ref, w_ref, o_ref):
    i = pl.program_id(1)
    last = pl.num_programs(1) - 1
    tl = u_ref.shape[1]
    nc = o_ref.shape[1]
    hw = nc * LANES
    row = lax.broadcasted_iota(jnp.int32, (tl, hw), 0)
    for s in range(o_ref.shape[0]):
        cols = slice(s * hw, (s + 1) * hw)
        u = u_ref[0, :, cols].astype(F32)
        prev_row = p_ref[0, :, cols].astype(F32)[BF16_SUBLANES - 1:BF16_SUBLANES]
        next_row = n_ref[0, :, cols].astype(F32)[0:1]
        prev_row = jnp.where(i > 0, prev_row, 0.0)
        next_row = jnp.where(i < last, next_row, 0.0)
        up = jnp.where(row == 0, prev_row, pltpu.roll(u, 1, 0))
        un = jnp.where(row == tl - 1, next_row, pltpu.roll(u, tl - 1, 0))
        w = w_ref[:, cols]
        res = (up * w[0:1] + u * w[1:2] + un * w[2:3]).astype(BF16)
        pk = pltpu.bitcast(res, U32)
        for c in range(nc):
            o_ref[s, c, 0] = pk[:, c * LANES:(c + 1) * LANES]


def _shortconv(proj3, conv_w, hw):
    B, L, _ = proj3.shape
    tl = min(512, L)
    hb = tl // BF16_SUBLANES
    nhb = L // BF16_SUBLANES
    nc = hw // LANES
    return pl.pallas_call(
        _shortconv_kernel,
        grid=(B, L // tl),
        in_specs=[pl.BlockSpec((1, tl, 3 * hw), lambda b, i: (b, i, 0)),
                  pl.BlockSpec((1, BF16_SUBLANES, 3 * hw), lambda b, i: (b, jnp.maximum(i * hb - 1, 0), 0)),
                  pl.BlockSpec((1, BF16_SUBLANES, 3 * hw), lambda b, i: (b, jnp.minimum((i + 1) * hb, nhb - 1), 0)),
                  pl.BlockSpec((3, 3 * hw), lambda b, i: (0, 0))],
        out_specs=pl.BlockSpec((3, nc, 1, tl // 2, LANES), lambda b, i: (0, 0, b, i, 0)),
        out_shape=jax.ShapeDtypeStruct((3, nc, B, L // 2, LANES), U32),
        compiler_params=_cparams(("parallel", "parallel")),
        name="shortconv",
    )(proj3, proj3, proj3, conv_w)


def _split_bf16(x):
    hi = x.astype(BF16)
    return hi, (x - hi.astype(F32)).astype(BF16)


def _dot3(a, b):
    ah, al = _split_bf16(a)
    bh, bl = _split_bf16(b)
    d = lambda x, y: jnp.dot(x, y, preferred_element_type=F32)
    return d(ah, bh) + (d(ah, bl) + d(al, bh))


def _filter_kernel(zz_ref, w1_ref, b1_ref, f1_ref, w2_ref, b2_ref, f2_ref, w3_ref, dl_ref, o_ref):
    zz = zz_ref[...]
    h = _dot3(zz, w1_ref[...])
    h = jnp.sin(f1_ref[...] * (h + b1_ref[...]))
    h = _dot3(h, w2_ref[...])
    h = jnp.sin(f2_ref[...] * (h + b2_ref[...]))
    hf = _dot3(h, w3_ref[0])
    t = zz[:, 0:1]
    mask = zz[:, _MASK_COL:_MASK_COL + 1]
    kern = hf * (jnp.exp(-t * dl_ref[...]) * mask)
    for c in range(o_ref.shape[0]):
        o_ref[c] = kern[:, c * LANES:(c + 1) * LANES]


_MASK_COL = 2 * FILTER_BANDS + 1


def _filter_positions(L):
    bands = FILTER_BANDS
    t = np.linspace(0.0, 1.0, L, dtype=np.float64)[:, None]
    w = 2.0 * math.pi * np.arange(L, dtype=np.float64)[:, None] / L
    f = np.linspace(1e-4, bands - 1, bands, dtype=np.float64)[None, :]
    z = np.concatenate([t, np.cos(f * w), -np.sin(f * w)], axis=-1)
    zz = np.zeros((2 * L, LANES), np.float64)
    zz[:L, :z.shape[1]] = z
    zz[L + 1:, :z.shape[1]] = z[1:][::-1]
    zz[:, _MASK_COL] = 1.0
    zz[L, :] = 0.0
    return zz.astype(np.float32)


def _hyena_time_filters(L, w_f1, b_f1, freq1, w_f2, b_f2, freq2, w_f3, hw):
    emb, hid = w_f1.shape
    zz = jnp.asarray(_filter_positions(L))
    w1 = jnp.zeros((LANES, hid), F32).at[:emb].set(w_f1)
    w3 = w_f3.reshape(hid, 2, 2, hw).transpose(2, 0, 1, 3).reshape(2, hid, 2 * hw)
    max_decay = math.log(HYENA_TARGET) / FAST_DECAY_PCT
    min_decay = math.log(HYENA_TARGET) / SLOW_DECAY_PCT
    deltas = np.abs(np.linspace(min_decay, max_decay, hw, dtype=np.float64))
    dl = jnp.asarray(np.tile(deltas, 2)[None, :], F32)
    tr = min(512, L)
    per_dir = L // tr
    row = lambda a: a.reshape(1, -1)
    return pl.pallas_call(
        _filter_kernel,
        grid=(2 * L // tr,),
        in_specs=[pl.BlockSpec((tr, LANES), lambda i: (i, 0)),
                  pl.BlockSpec((LANES, hid), lambda i: (0, 0)),
                  pl.BlockSpec((1, hid), lambda i: (0, 0)),
                  pl.BlockSpec((1, hid), lambda i: (0, 0)),
                  pl.BlockSpec((hid, hid), lambda i: (0, 0)),
                  pl.BlockSpec((1, hid), lambda i: (0, 0)),
                  pl.BlockSpec((1, hid), lambda i: (0, 0)),
                  pl.BlockSpec((1, hid, 2 * hw), lambda i: (i // per_dir, 0, 0)),
                  pl.BlockSpec((1, 2 * hw), lambda i: (0, 0))],
        out_specs=pl.BlockSpec((2 * hw // LANES, tr, LANES), lambda i: (0, i, 0)),
        out_shape=jax.ShapeDtypeStruct((2 * hw // LANES, 2 * L, LANES), F32),
        compiler_params=_cparams(("parallel",)),
        name="hyena_filter",
    )(zz, w1, row(b_f1), row(freq1), w_f2, row(b_f2), row(freq2), w3, dl)


def _dft_matrices(L):
    N = 2 * L
    N1 = DFT_N1
    N2 = N // N1
    h = N2 // 2
    ang = 2.0 * np.pi * np.outer(np.arange(N2), np.arange(N2)) / N2
    c, s = np.cos(ang), np.sin(ang)
    m1 = np.zeros((N2, 2, N2))
    m1[:, 0, :h], m1[:, 0, h:] = c[:, :h], s[:, :h]
    m1[:, 1, :h], m1[:, 1, h:] = -s[:, :h], c[:, :h]
    m1 = m1.reshape(2 * N2, N2)
    m1f = np.stack([c, -s], axis=1).reshape(2 * N2, N2)
    m3 = np.zeros((2, h, N2, 2))
    m3[0, :, :, 0], m3[0, :, :, 1] = c[:h], -s[:h]
    m3[1, :, :, 0], m3[1, :, :, 1] = s[:h], c[:h]
    m3 = (m3 / N).reshape(N2, 2 * N2)
    def mid(order, transposed):
        freq = jnp.arange(2 * N1, dtype=jnp.int32)
        data = jnp.asarray(order, jnp.int32)
        f, d = (freq[None, None, :], data[None, :, None]) if transposed else (freq[None, :, None], data[None, None, :])
        k2 = jnp.arange(N2, dtype=jnp.int32)[:, None, None]
        th = (((d % N1) * (N2 * (f % N1) + k2)) % N).astype(F32) * F32(2.0 * math.pi / N)
        out_im, in_im = f // N1, d // N1
        return jnp.where(out_im == in_im, jnp.cos(th), jnp.where(out_im < in_im, jnp.sin(th), -jnp.sin(th)))

    packed = _mid_row_order(DFT_PAIRS, 2)
    eye2 = np.eye(2)
    return dict(m1x=jnp.asarray(np.kron(m1, eye2), BF16),
                m3x=jnp.asarray(np.kron(m3, eye2), BF16),
                m1f=jnp.asarray(m1f, F32),
                m2f=mid(_mid_row_order(DFT_FILTER_STEP, 1), False),
                m2p=mid(packed, False).astype(BF16),
                m2tp=mid(packed, True).astype(BF16))


def _dft_filter_a_kernel(m_ref, k_ref, o_ref, *, step, n2):
    j0 = pl.program_id(1) * step
    rows = m_ref.shape[0]
    for jj in range(step):
        rhs = jnp.concatenate([k_ref[c, pl.ds(j0 + jj, n2, stride=DFT_N1), :] for c in range(k_ref.shape[0])],
                              axis=1)
        res = _dot3(m_ref[...], rhs)
        for c in range(o_ref.shape[0]):
            o_ref[c, 0, pl.ds(jj, rows, stride=step), :] = res[:, c * LANES:(c + 1) * LANES]


def _dft_filter_a(m1f, kern):
    NB, N, _ = kern.shape
    n2 = N // DFT_N1
    step, ncb = DFT_FILTER_STEP, DFT_LANE_BLOCKS
    rows = m1f.shape[0]
    return pl.pallas_call(
        functools.partial(_dft_filter_a_kernel, step=step, n2=n2),
        grid=(NB // ncb, DFT_N1 // step),
        in_specs=[pl.BlockSpec(m1f.shape, lambda h, j: (0, 0)),
                  pl.BlockSpec((ncb, N, LANES), lambda h, j: (h, 0, 0))],
        out_specs=pl.BlockSpec((ncb, 1, rows * step, LANES), lambda h, j: (h, j, 0, 0)),
        out_shape=jax.ShapeDtypeStruct((NB, DFT_N1 // step, rows * step, LANES), F32),
        compiler_params=_cparams(("parallel", "arbitrary")),
        name="dft_filter_a",
    )(m1f, kern)


def _dft_filt_kernel(a_ref, m_ref, o_ref):
    rows = m_ref.shape[1]
    a = jnp.concatenate([a_ref[c, :, 0].reshape(rows, LANES) for c in range(a_ref.shape[0])], axis=1)
    o_ref[0] = _dot3(m_ref[0], a)


def _dft_filt(a, m2):
    NB, G, N2, W, _ = a.shape
    R = m2.shape[1]
    return pl.pallas_call(
        _dft_filt_kernel,
        grid=(N2,),
        in_specs=[pl.BlockSpec((NB, G, 1, W, LANES), lambda k: (0, 0, k, 0, 0)),
                  pl.BlockSpec((1, R, R), lambda k: (k, 0, 0))],
        out_specs=pl.BlockSpec((1, R, NB * LANES), lambda k: (k, 0, 0)),
        out_shape=jax.ShapeDtypeStruct((N2, R, NB * LANES), F32),
        compiler_params=_cparams(("parallel",)),
        name="dft_filter_mid",
    )(a, m2)


def _packed_rows(ref, lead, start, size, stride):
    cols = [jnp.concatenate([ref[(c,) + l + (pl.ds(start, size, stride=stride), slice(None))] for l in lead], axis=0)
            for c in range(ref.shape[0])]
    return pltpu.bitcast(jnp.concatenate(cols, axis=1), BF16)


def _dft_a_pk_kernel(m_ref, z_ref, o_ref, *, tm, nh, npairs):
    j0 = pl.program_id(2) * tm
    rows = m_ref.shape[0] // 2
    for j in range(tm):
        rhs = _packed_rows(z_ref, [(0,), (1,)], j0 + j, nh, npairs)
        res = jnp.dot(m_ref[...], rhs, preferred_element_type=F32)
        pk = pltpu.bitcast(res.astype(BF16), U32)
        for c in range(o_ref.shape[0]):
            o_ref[c, 0, 0, pl.ds(j, rows, stride=tm), :] = pk[:, c * LANES:(c + 1) * LANES]


def _dft_a_pk(m1x, z, slot):
    _, NC, B, L2, _ = z.shape
    npairs = DFT_N1 // 2
    nh = L2 // npairs
    tm, ncb = DFT_PAIRS, DFT_LANE_BLOCKS
    rows = m1x.shape[0] // 2
    P = B // 2
    return pl.pallas_call(
        functools.partial(_dft_a_pk_kernel, tm=tm, nh=nh, npairs=npairs),
        grid=(P, NC // ncb, npairs // tm),
        in_specs=[pl.BlockSpec(m1x.shape, lambda p, h, j: (0, 0)),
                  pl.BlockSpec((None, ncb, 2, L2, LANES), lambda p, h, j: (slot, h, p, 0, 0))],
        out_specs=pl.BlockSpec((ncb, 1, 1, rows * tm, LANES), lambda p, h, j: (h, p, j, 0, 0)),
        out_shape=jax.ShapeDtypeStruct((NC, P, npairs // tm, rows * tm, LANES), U32),
        compiler_params=_cparams(("parallel", "parallel", "arbitrary")),
        name="dft_a",
    )(m1x, z)


def _dft_mid_kernel(a_ref, m_ref, mt_ref, kf_ref, o_ref):
    m = m_ref[0]
    mt = mt_ref[0]
    n1 = m.shape[0] // 2
    kr = kf_ref[0, :n1]
    ki = kf_ref[0, n1:]
    nc, npair = a_ref.shape[0], a_ref.shape[1]
    blk = a_ref.shape[2], a_ref.shape[4], LANES
    for p in range(npair):
        cols = [a_ref[c, p, :, 0].reshape(n1, LANES) for c in range(nc)]
        a = pltpu.bitcast(jnp.concatenate(cols, axis=1), BF16)
        x = jnp.dot(m, a, preferred_element_type=F32)
        xr = x[:n1]
        xi = x[n1:]
        y = jnp.concatenate([xr * kr - xi * ki, xr * ki + xi * kr], axis=0).astype(BF16)
        pk = pltpu.bitcast(jnp.dot(mt, y, preferred_element_type=F32).astype(BF16), U32)
        for c in range(nc):
            o_ref[c, p, :, 0] = pk[:, c * LANES:(c + 1) * LANES].reshape(blk)


def _dft_mid(a, m2, m2t, kf, order):
    NC, P, G, N2, W, _ = a.shape
    R = m2.shape[1]
    C = NC * LANES
    blk = pl.BlockSpec((NC, P, G, 1, W, LANES), lambda k: (0, 0, 0, k, 0, 0))
    return pl.pallas_call(
        _dft_mid_kernel,
        grid=(N2,),
        in_specs=[blk,
                  pl.BlockSpec((1, R, R), lambda k: (k, 0, 0)),
                  pl.BlockSpec((1, R, R), lambda k: (k, 0, 0)),
                  pl.BlockSpec((1, R, C), lambda k: (k, 0, order))],
        out_specs=blk,
        out_shape=jax.ShapeDtypeStruct(a.shape, U32),
        compiler_params=_cparams(("parallel",)),
        name="dft_mid",
    )(a, m2, m2t, kf)


def _dft_c_pk_kernel(m_ref, b_ref, z_ref, g_ref, bias_ref, o_ref, *, tm, nh, npairs):
    j0 = pl.program_id(2) * tm
    rows = b_ref.shape[3] // tm
    bias = bias_ref[...]
    both = [(0,), (1,)]
    for j in range(tm):
        rhs = _packed_rows(b_ref, [(0, 0)], j, rows, tm)
        y = jnp.dot(m_ref[...], rhs, preferred_element_type=F32)
        z = _packed_rows(z_ref, both, j0 + j, nh, npairs).astype(F32)
        g = _packed_rows(g_ref, both, j0 + j, nh, npairs).astype(F32)
        pk = pltpu.bitcast((g * (y + z * bias)).astype(BF16), U32)
        for c in range(o_ref.shape[0]):
            for b in range(2):
                o_ref[c, b, pl.ds(j0 + j, nh, stride=npairs), :] = pk[b * nh:(b + 1) * nh, c * LANES:(c + 1) * LANES]


def _dft_c_pk(m3x, b, z, zslot, g, gslot, bias):
    NC, P, G, W, _ = b.shape
    L2 = z.shape[3]
    npairs = DFT_N1 // 2
    nh = L2 // npairs
    tm, ncb = DFT_PAIRS, DFT_LANE_BLOCKS
    seq = lambda slot: pl.BlockSpec((None, ncb, 2, L2, LANES), lambda p, h, j: (slot, h, p, 0, 0))
    return pl.pallas_call(
        functools.partial(_dft_c_pk_kernel, tm=tm, nh=nh, npairs=npairs),
        grid=(P, NC // ncb, G),
        in_specs=[pl.BlockSpec(m3x.shape, lambda p, h, j: (0, 0)),
                  pl.BlockSpec((ncb, 1, 1, W, LANES), lambda p, h, j: (h, p, j, 0, 0)),
                  seq(zslot), seq(gslot),
                  pl.BlockSpec((1, ncb * LANES), lambda p, h, j: (0, h))],
        out_specs=seq(0),
        out_shape=jax.ShapeDtypeStruct((1, NC, 2 * P, L2, LANES), U32),
        compiler_params=_cparams(("parallel", "parallel", "arbitrary")),
        name="dft_c",
    )(m3x, b, z, g, bias)


def _mid_row_order(step, per):
    N1 = DFT_N1
    idx = []
    for mt in range(N1 // per // step):
        for ri in range(2):
            for jj in range(step):
                for sub in range(per):
                    idx.append(ri * N1 + per * (mt * step + jj) + sub)
    return np.asarray(idx, np.int32)


def _hyena(u3, kern, hy_bias, mats):
    _, NC, B, L2, _ = u3.shape
    C = NC * LANES
    N1 = DFT_N1
    N2 = 4 * L2 // N1
    af = _dft_filter_a(mats["m1f"], kern)
    NB, G, W, _ = af.shape
    kf = _dft_filt(af.reshape(NB, G, N2, W // N2, LANES), mats["m2f"])
    bias = hy_bias.astype(F32)

    def long_conv(z, zslot, gslot, order):
        a = _dft_a_pk(mats["m1x"], z, zslot)
        NCa, P, G, W, _ = a.shape
        b = _dft_mid(a.reshape(NCa, P, G, N2, W // N2, LANES), mats["m2p"], mats["m2tp"], kf, order)
        return _dft_c_pk(mats["m3x"], b.reshape(a.shape), z, zslot, u3, gslot, bias[order][None, :])

    z = long_conv(u3, 2, 0, 0)
    return long_conv(z, 0, 1, 1)


def _qk_kernel(q_ref, k_ref, cos_ref, sin_ref, qg_ref, kg_ref, grp_ref, qo_ref, ko_ref):
    cos = cos_ref[...]
    sin = sin_ref[...]
    lane = lax.broadcasted_iota(jnp.int32, cos.shape, 1)
    first = (lane % HEAD_DIM) < (HEAD_DIM // 2)
    grp = grp_ref[...]

    def one(x_ref, g_ref, o_ref):
        for h in range(x_ref.shape[2] // LANES):
            x = x_ref[0, :, h * LANES:(h + 1) * LANES].astype(F32)
            sq_hi, sq_lo = _split_bf16(x * x)
            ss = (jnp.dot(sq_hi, grp, preferred_element_type=F32)
                  + jnp.dot(sq_lo, grp, preferred_element_type=F32))
            xn = x * lax.rsqrt(ss * (1.0 / HEAD_DIM) + RMS_EPS) * g_ref[...]
            partner = jnp.where(first, pltpu.roll(xn, LANES - HEAD_DIM // 2, 1),
                                pltpu.roll(xn, HEAD_DIM // 2, 1))
            o_ref[0, :, h * LANES:(h + 1) * LANES] = (xn * cos + partner * sin).astype(o_ref.dtype)

    one(q_ref, qg_ref, qo_ref)
    one(k_ref, kg_ref, ko_ref)


def _rope_tables(L):
    half = HEAD_DIM // 2
    inv = 1.0 / (ROPE_THETA ** (np.arange(0, HEAD_DIM, 2, dtype=np.float64) / HEAD_DIM))
    ang = np.arange(L, dtype=np.float64)[:, None] * inv[None, :]
    cos, sin = np.cos(ang), np.sin(ang)
    cos_t = np.tile(cos, (1, LANES // half))
    sin_t = np.tile(np.concatenate([-sin, sin], axis=1), (1, LANES // HEAD_DIM))
    return jnp.asarray(cos_t, F32), jnp.asarray(sin_t, F32)


def _qk_prep(proj3, q_norm, k_norm, qcol, kcol, width):
    B, L, _ = proj3.shape
    tl = min(512, L)
    cos_t, sin_t = _rope_tables(L)
    qg = jnp.tile(q_norm.astype(F32), LANES // HEAD_DIM)[None, :] * F32(HEAD_DIM ** -0.5 * math.log2(math.e))
    kg = jnp.tile(k_norm.astype(F32), LANES // HEAD_DIM)[None, :]
    lanes = np.arange(LANES)
    grp = jnp.asarray(lanes[:, None] // HEAD_DIM == lanes[None, :] // HEAD_DIM, BF16)
    blk = lambda col: pl.BlockSpec((1, tl, width), lambda b, i: (b, i, col))
    tab = pl.BlockSpec((tl, LANES), lambda b, i: (i, 0))
    vec = pl.BlockSpec((1, LANES), lambda b, i: (0, 0))
    out = pl.BlockSpec((1, tl, width), lambda b, i: (b, i, 0))
    return pl.pallas_call(
        _qk_kernel,
        grid=(B, L // tl),
        in_specs=[blk(qcol), blk(kcol), tab, tab, vec, vec, pl.BlockSpec((LANES, LANES), lambda b, i: (0, 0))],
        out_specs=[out, out],
        out_shape=[jax.ShapeDtypeStruct((B, L, width), BF16)] * 2,
        compiler_params=_cparams(("parallel", "parallel")),
        name="qk_norm_rope",
    )(proj3, proj3, cos_t, sin_t, qg, kg, grp)


SHIFT_LIMIT = 50.0


def _attn_kernel(lam_ref, q_ref, k_ref, v_ref, g_ref, o_ref, kmax_ref, *, post):
    k = k_ref[0]
    v = v_ref[0]

    @pl.when(pl.program_id(2) == 0)
    def _():
        kmax_ref[...] = jnp.max(jnp.abs(k.astype(F32)), axis=0, keepdims=True)

    q = q_ref[0]
    lane = lax.broadcasted_iota(jnp.int32, q.shape, 1)
    lo = lane < HEAD_DIM
    zero = jnp.zeros_like(q)
    reach = jnp.abs(q.astype(F32)) * kmax_ref[...]
    shifts = [jnp.sum(jnp.where(lo, reach, 0.0), axis=-1, keepdims=True),
              jnp.sum(jnp.where(lo, 0.0, reach), axis=-1, keepdims=True)]
    worst = jnp.max(jnp.maximum(shifts[0], shifts[1]))

    def run(row_max):
        outs = []
        for c in range(2):
            qc = jnp.where(lo, q, zero) if c == 0 else jnp.where(lo, zero, q)
            s = lax.dot_general(qc, k, (((1,), (1,)), ((), ())), preferred_element_type=F32)
            m = jnp.max(s, axis=-1, keepdims=True) if row_max else shifts[c]
            p = jnp.exp2(s - m)
            l = jnp.sum(p, axis=-1, keepdims=True)
            outs.append(jnp.dot(p.astype(BF16), v, preferred_element_type=F32) / l)
        a = outs[0] - lam_ref[0] * outs[1]
        ms = jnp.mean(a * a, axis=-1, keepdims=True)
        o_ref[0] = (a * lax.rsqrt(ms + RMS_EPS) * g_ref[...] * post).astype(o_ref.dtype)

    @pl.when(worst <= SHIFT_LIMIT)
    def _():
        run(False)

    @pl.when(jnp.logical_not(worst <= SHIFT_LIMIT))
    def _():
        run(True)


def _attention(qn, kn, proj3, vcol, lam, subln_g, post):
    B, L, W = qn.shape
    tq = min(512, L)
    return pl.pallas_call(
        functools.partial(_attn_kernel, post=post),
        grid=(B, W // V_DIM, L // tq),
        in_specs=[pl.BlockSpec(memory_space=pltpu.SMEM),
                  pl.BlockSpec((1, tq, V_DIM), lambda b, h, i: (b, i, h)),
                  pl.BlockSpec((1, L, V_DIM), lambda b, h, i: (b, 0, h)),
                  pl.BlockSpec((1, L, V_DIM), lambda b, h, i: (b, 0, vcol + h)),
                  pl.BlockSpec((1, V_DIM), lambda b, h, i: (0, 0))],
        out_specs=pl.BlockSpec((1, tq, V_DIM), lambda b, h, i: (b, i, h)),
        out_shape=jax.ShapeDtypeStruct((B, L, W), BF16),
        scratch_shapes=[pltpu.VMEM((1, V_DIM), F32)],
        compiler_params=_cparams(("parallel", "parallel", "arbitrary")),
        name="diff_attention",
    )(lam.reshape(1).astype(F32), qn, kn, proj3, subln_g.astype(F32).reshape(1, V_DIM))


def _merge_kernel(x_ref, yh_ref, ya_ref, gh_ref, ga_ref, who_ref, wao_ref, wout_ref, g_ref, *rest,
                  with_router):
    if with_router:
        wr_ref, xo_ref, hn_ref, lg_ref = rest
    else:
        xo_ref, hn_ref = rest
    yh = jnp.concatenate([pltpu.bitcast(yh_ref[c], BF16) for c in range(yh_ref.shape[0])], axis=1)
    th = jnp.dot(yh, who_ref[...], preferred_element_type=F32)
    ta = jnp.dot(ya_ref[...], wao_ref[...], preferred_element_type=F32)
    merged = (jax.nn.sigmoid(gh_ref[...].astype(F32)) * th
              + jax.nn.sigmoid(ga_ref[...].astype(F32)) * ta)
    xn = x_ref[...] + jnp.dot(merged.astype(BF16), wout_ref[...], preferred_element_type=F32)
    xo_ref[...] = xn
    ms = jnp.mean(xn * xn, axis=-1, keepdims=True)
    h = xn * lax.rsqrt(ms + RMS_EPS) * g_ref[...]
    hn_ref[...] = h.astype(hn_ref.dtype)
    if with_router:
        lane = lax.broadcasted_iota(jnp.int32, lg_ref.shape, 1)
        lg = jnp.zeros(lg_ref.shape, F32)
        for e in range(wr_ref.shape[0]):
            lg = jnp.where(lane == e, jnp.sum(h * wr_ref[e:e + 1, :], axis=-1, keepdims=True), lg)
        lg_ref[...] = lg


def _merge(x2, yh, ya, proj, gcol, who, wao, wout, g_ffn, w_router):
    T, D = x2.shape
    W = ya.shape[1]
    tm = min(512, T)
    with_router = w_router is not None
    row = lambda w: pl.BlockSpec((tm, w), lambda i: (i, 0))
    full = lambda a: pl.BlockSpec(a.shape, lambda i: (0, 0))
    in_specs = [row(D), pl.BlockSpec((yh.shape[0], tm // 2, LANES), lambda i: (0, i, 0)), row(W),
                pl.BlockSpec((tm, D), lambda i: (i, gcol)),
                pl.BlockSpec((tm, D), lambda i: (i, gcol + 1)),
                full(who), full(wao), full(wout), pl.BlockSpec((1, D), lambda i: (0, 0))]
    args = [x2, yh, ya, proj, proj, who, wao, wout, g_ffn.astype(F32).reshape(1, D)]
    out_specs = [row(D), row(D)]
    out_shape = [jax.ShapeDtypeStruct((T, D), F32),
                 jax.ShapeDtypeStruct((T, D), F32 if with_router else BF16)]
    if with_router:
        wr = w_router.astype(F32).T
        in_specs.append(full(wr))
        args.append(wr)
        out_specs.append(row(LANES))
        out_shape.append(jax.ShapeDtypeStruct((T, LANES), F32))
    return pl.pallas_call(
        functools.partial(_merge_kernel, with_router=with_router),
        grid=(T // tm,),
        in_specs=in_specs,
        out_specs=out_specs,
        out_shape=out_shape,
        compiler_params=_cparams(("parallel",)),
        name="merge_outproj",
    )(*args)


def _swiglu(h, wg_ref, wu_ref, wd_ref, chunks):
    F = wg_ref.shape[-1]
    fc = F // chunks
    acc = None
    for c in range(chunks):
        sl = slice(c * fc, (c + 1) * fc)
        g = jnp.dot(h, wg_ref[:, sl], preferred_element_type=F32)
        u = jnp.dot(h, wu_ref[:, sl], preferred_element_type=F32)
        a = (g * jax.nn.sigmoid(g) * u).astype(BF16)
        d = jnp.dot(a, wd_ref[sl, :], preferred_element_type=F32)
        acc = d if acc is None else acc + d
    return acc


def _ffn_kernel(h_ref, x_ref, wg_ref, wu_ref, wd_ref, o_ref, *, chunks):
    o_ref[...] = x_ref[...] + _swiglu(h_ref[...], wg_ref, wu_ref, wd_ref, chunks)


def _ff_chunks(F):
    return F // (2 * LANES) if F % (2 * LANES) == 0 else 1


def _dense_ffn(hn, x2, wg, wu, wd):
    T, D = x2.shape
    F = wg.shape[1]
    tm = min(512, T)
    row = pl.BlockSpec((tm, D), lambda i: (i, 0))
    return pl.pallas_call(
        functools.partial(_ffn_kernel, chunks=_ff_chunks(F)),
        grid=(T // tm,),
        in_specs=[row, row,
                  pl.BlockSpec((D, F), lambda i: (0, 0)),
                  pl.BlockSpec((D, F), lambda i: (0, 0)),
                  pl.BlockSpec((F, D), lambda i: (0, 0))],
        out_specs=row,
        out_shape=jax.ShapeDtypeStruct((T, D), F32),
        compiler_params=_cparams(("parallel",)),
        name="dense_ffn",
    )(hn, x2, wg, wu, wd)


def _row_copy(src_hbm, row, dst, r, sem):
    return pltpu.make_async_copy(src_hbm.at[pl.ds(row, 1)], dst.at[pl.ds(r, 1)], sem)


def _rows_done(src_hbm, dst, sem):
    return pltpu.make_async_copy(src_hbm.at[pl.ds(0, dst.shape[0])], dst, sem)


def _expert_kernel(be_ref, tok_ref, hn_hbm, wg_ref, wu_ref, wd_ref, o_ref, xbuf, sem, *, chunks):
    i = pl.program_id(0)
    last = pl.num_programs(0) - 1
    rows = xbuf.shape[1]
    slot = i % 2
    nxt = 1 - slot

    @pl.when(i == 0)
    def _():
        def issue(r, carry):
            _row_copy(hn_hbm, tok_ref[r], xbuf.at[0], r, sem.at[0]).start()
            return carry

        lax.fori_loop(0, rows, issue, 0, unroll=8)

    _rows_done(hn_hbm, xbuf.at[slot], sem.at[slot]).wait()
    base = jnp.minimum(i + 1, last) * rows
    for r in range(rows):
        _row_copy(hn_hbm, tok_ref[base + r], xbuf.at[nxt], r, sem.at[nxt]).start()
    o_ref[...] = _swiglu(xbuf[slot].astype(BF16), wg_ref.at[0], wu_ref.at[0], wd_ref.at[0], chunks)

    @pl.when(i == last)
    def _():
        _rows_done(hn_hbm, xbuf.at[nxt], sem.at[nxt]).wait()


def _experts(block_e, tok_of_slot, hn, wg, wu, wd):
    T, D = hn.shape
    E, _, F = wg.shape
    P = tok_of_slot.shape[0]
    rows = MOE_ROWS
    wspec = lambda shape: pl.BlockSpec((1,) + shape, lambda i, be, tok: (be[i], 0, 0))
    return pl.pallas_call(
        functools.partial(_expert_kernel, chunks=_ff_chunks(F)),
        grid_spec=pltpu.PrefetchScalarGridSpec(
            num_scalar_prefetch=2,
            grid=(P // rows,),
            in_specs=[pl.BlockSpec(memory_space=pl.ANY),
                      wspec((D, F)), wspec((D, F)), wspec((F, D))],
            out_specs=pl.BlockSpec((rows, D), lambda i, be, tok: (i, 0)),
            scratch_shapes=[pltpu.VMEM((2, rows, D), F32), pltpu.SemaphoreType.DMA((2,))]),
        out_shape=jax.ShapeDtypeStruct((P, D), F32),
        compiler_params=_cparams(("arbitrary",)),
        name="moe_experts",
    )(block_e, tok_of_slot, hn, wg, wu, wd)


def _combine_kernel(dest_ref, yp_hbm, x_ref, w_ref, o_ref, buf0, buf1, sem):
    i = pl.program_id(0)
    rows = buf0.shape[1]

    def gather(tile, slot):
        base = tile * rows * TOP_K

        def issue(r, carry):
            _row_copy(yp_hbm, dest_ref[base + TOP_K * r], buf0.at[slot], r, sem.at[slot]).start()
            _row_copy(yp_hbm, dest_ref[base + TOP_K * r + 1], buf1.at[slot], r, sem.at[slot]).start()
            return carry

        lax.fori_loop(0, rows, issue, 0, unroll=8)

    @pl.when(i == 0)
    def _():
        gather(0, 0)

    @pl.when(i + 1 < pl.num_programs(0))
    def _():
        gather(i + 1, (i + 1) % 2)

    slot = i % 2
    _rows_done(yp_hbm, buf0.at[slot], sem.at[slot]).wait()
    _rows_done(yp_hbm, buf1.at[slot], sem.at[slot]).wait()
    w = w_ref[...]
    o_ref[...] = x_ref[...] + (buf0[slot] * w[:, 0:1] + buf1[slot] * w[:, 1:2])


def _combine(dest, yp, x2, top_w):
    T, D = x2.shape
    tm = min(256, T)
    return pl.pallas_call(
        _combine_kernel,
        grid_spec=pltpu.PrefetchScalarGridSpec(
            num_scalar_prefetch=1,
            grid=(T // tm,),
            in_specs=[pl.BlockSpec(memory_space=pl.ANY),
                      pl.BlockSpec((tm, D), lambda i, d: (i, 0)),
                      pl.BlockSpec((tm, TOP_K), lambda i, d: (i, 0))],
            out_specs=pl.BlockSpec((tm, D), lambda i, d: (i, 0)),
            scratch_shapes=[pltpu.VMEM((2, tm, D), F32), pltpu.VMEM((2, tm, D), F32),
                            pltpu.SemaphoreType.DMA((2,))]),
        out_shape=jax.ShapeDtypeStruct((T, D), F32),
        compiler_params=_cparams(("arbitrary",)),
        name="moe_combine",
    )(dest, yp, x2, top_w)


def _route(logits, n_experts, rows):
    T = logits.shape[0]
    top_val, top_idx = lax.top_k(logits[:, :n_experts], TOP_K)
    top_w = jax.nn.softmax(top_val, axis=-1)
    A = T * TOP_K
    e_flat = top_idx.reshape(A).astype(jnp.int32)
    onehot = (e_flat[:, None] == jnp.arange(n_experts, dtype=jnp.int32)[None, :]).astype(jnp.int32)
    csum = jnp.cumsum(onehot, axis=0)
    counts = csum[-1]
    rank = jnp.sum((csum - onehot) * onehot, axis=1)
    padded = (counts + rows - 1) // rows * rows
    pad_end = jnp.cumsum(padded)
    pad_start = pad_end - padded
    dest = pad_start[e_flat] + rank
    n_blocks = -(-A // rows) + n_experts
    P = n_blocks * rows
    order = jnp.argsort(e_flat)
    starts = jnp.cumsum(counts) - counts
    slot = jnp.arange(P, dtype=jnp.int32)
    slot_e = jnp.minimum(jnp.sum((slot[:, None] >= pad_end[None, :]).astype(jnp.int32), axis=1),
                         n_experts - 1)
    within = slot - pad_start[slot_e]
    valid = within < counts[slot_e]
    src = jnp.clip(starts[slot_e] + within, 0, A - 1)
    tok_of_slot = jnp.where(valid, order[src] // TOP_K, 0).astype(jnp.int32)
    block_e = slot_e[::rows]
    return top_w.astype(F32), dest.astype(jnp.int32), tok_of_slot, block_e


def _pad_ff(w, axis):
    F = w.shape[axis]
    pad = [(0, 0)] * w.ndim
    pad[axis] = (0, _round_up(F, 2 * LANES) - F)
    return jnp.pad(w.astype(BF16), pad)


def kernel(x, norm_mix, w_in, conv_w, w_f1, b_f1, freq1, w_f2, b_f2, freq2, w_f3, hy_bias, q_norm, k_norm, lam_q1, lam_k1, lam_q2, lam_k2, subln_g, w_hy_o, w_at_o, w_out, norm_ffn, w_dense_gate, w_dense_up, w_dense_down, w_router, w_moe_gate, w_moe_up, w_moe_down):
    B, L, D = x.shape
    T = B * L
    depth = w_in.shape[0]
    hw = w_hy_o.shape[1]
    qk_w = N_HEADS * 2 * HEAD_DIM
    at_w = N_HEADS * V_DIM
    qcol = 3 * hw // qk_w
    kcol = qcol + 1
    vcol = (3 * hw + 2 * qk_w) // V_DIM
    gcol = (3 * hw + 2 * qk_w + at_w) // D
    mats = _dft_matrices(L)

    x2 = x.reshape(T, D)
    for layer in range(depth):
        lambda_init = 0.8 - 0.6 * math.exp(-0.3 * layer)
        proj = _inproj(x2, norm_mix[layer].astype(F32), w_in[layer].astype(BF16))
        proj3 = proj.reshape(B, L, -1)
        u3 = _shortconv(proj3, conv_w[layer].astype(F32), hw)
        kern = _hyena_time_filters(L, w_f1[layer], b_f1[layer], freq1[layer], w_f2[layer], b_f2[layer],
                                   freq2[layer], w_f3[layer], hw)
        y_h = _hyena(u3, kern, hy_bias[layer], mats)
        qn, kn = _qk_prep(proj3, q_norm[layer], k_norm[layer], qcol, kcol, qk_w)
        lam = (jnp.exp(jnp.sum(lam_q1[layer].astype(F32) * lam_k1[layer].astype(F32)))
               - jnp.exp(jnp.sum(lam_q2[layer].astype(F32) * lam_k2[layer].astype(F32))) + lambda_init)
        y_a = _attention(qn, kn, proj3, vcol, lam, subln_g[layer], 1.0 - lambda_init)
        i = layer // 2
        moe = layer % 2 == 1
        outs = _merge(x2, y_h.reshape(hw // LANES, T // 2, LANES), y_a.reshape(T, at_w), proj, gcol,
                      w_hy_o[layer].astype(BF16), w_at_o[layer].astype(BF16), w_out[layer].astype(BF16),
                      norm_ffn[layer], w_router[i] if moe else None)
        if not moe:
            x2, hn = outs
            x2 = _dense_ffn(hn, x2, _pad_ff(w_dense_gate[i], 1), _pad_ff(w_dense_up[i], 1),
                            _pad_ff(w_dense_down[i], 0))
        else:
            x2, hn, logits = outs
            n_experts = w_router.shape[2]
            top_w, dest, tok_of_slot, block_e = _route(logits, n_experts, MOE_ROWS)
            yp = _experts(block_e, tok_of_slot, hn, _pad_ff(w_moe_gate[i], 2),
                          _pad_ff(w_moe_up[i], 2), _pad_ff(w_moe_down[i], 1))
            x2 = _combine(dest, yp, x2, top_w)
    return x2.reshape(B, L, D)
```

```python
import functools
import math

import numpy as np
import jax
import jax.numpy as jnp
from jax import lax
from jax.experimental import pallas as pl
from jax.experimental.pallas import tpu as pltpu

F32 = jnp.float32
BF16 = jnp.bfloat16
U32 = jnp.uint32

N_HEADS = 4
HEAD_DIM = 64
V_DIM = 2 * HEAD_DIM
ROPE_THETA = 10000.0
FILTER_BANDS = 16
HYENA_TARGET = 1e-2
FAST_DECAY_PCT = 0.3
SLOW_DECAY_PCT = 1.5
TOP_K = 2
RMS_EPS = 1e-6

LANES = 128
BF16_SUBLANES = 16
VMEM_LIMIT = 56 * 1024 * 1024

DFT_N1 = 128
DFT_PAIRS = 16
DFT_FILTER_STEP = 16
DFT_LANE_BLOCKS = 2
MOE_ROWS = 512


def _cparams(sem):
    return pltpu.CompilerParams(dimension_semantics=sem, vmem_limit_bytes=VMEM_LIMIT)


def _round_up(a, b):
    return (a + b - 1) // b * b


def _inproj_kernel(x_ref, g_ref, w_ref, o_ref, hn_ref):
    @pl.when(pl.program_id(1) == 0)
    def _():
        x = x_ref[...]
        ms = jnp.mean(x * x, axis=-1, keepdims=True)
        hn_ref[...] = (x * lax.rsqrt(ms + RMS_EPS) * g_ref[...]).astype(BF16)

    o_ref[...] = jnp.dot(hn_ref[...], w_ref[...], preferred_element_type=F32).astype(o_ref.dtype)


def _inproj(x2, g, w):
    T, D = x2.shape
    N = w.shape[1]
    tm = min(512, T)
    tn = N
    return pl.pallas_call(
        _inproj_kernel,
        grid=(T // tm, N // tn),
        in_specs=[pl.BlockSpec((tm, D), lambda i, j: (i, 0)),
                  pl.BlockSpec((1, D), lambda i, j: (0, 0)),
                  pl.BlockSpec((D, tn), lambda i, j: (0, j))],
        out_specs=pl.BlockSpec((tm, tn), lambda i, j: (i, j)),
        out_shape=jax.ShapeDtypeStruct((T, N), BF16),
        scratch_shapes=[pltpu.VMEM((tm, D), BF16)],
        compiler_params=_cparams(("parallel", "arbitrary")),
        name="inproj",
    )(x2, g.reshape(1, D), w)


def _shortconv_kernel(u_ref, p_ref, n_ref, w_ref, o_ref):
    i = pl.program_id(1)
    last = pl.num_programs(1) - 1
    tl = u_ref.shape[1]
    nc = o_ref.shape[1]
    hw = nc * LANES
    row = lax.broadcasted_iota(jnp.int32, (tl, hw), 0)
    for s in range(o_ref.shape[0]):
        cols = slice(s * hw, (s + 1) * hw)
        u = u_ref[0, :, cols].astype(F32)
        prev_row = p_ref[0, :, cols].astype(F32)[BF16_SUBLANES - 1:BF16_SUBLANES]
        next_row = n_ref[0, :, cols].astype(F32)[0:1]
        prev_row = jnp.where(i > 0, prev_row, 0.0)
        next_row = jnp.where(i < last, next_row, 0.0)
        up = jnp.where(row == 0, prev_row, pltpu.roll(u, 1, 0))
        un = jnp.where(row == tl - 1, next_row, pltpu.roll(u, tl - 1, 0))
        w = w_ref[:, cols]
        res = (up * w[0:1] + u * w[1:2] + un * w[2:3]).astype(BF16)
        pk = pltpu.bitcast(res, U32)
        for c in range(nc):
            o_ref[s, c, 0] = pk[:, c * LANES:(c + 1) * LANES]


def _shortconv(proj3, conv_w, hw):
    B, L, _ = proj3.shape
    tl = min(512, L)
    hb = tl // BF16_SUBLANES
    nhb = L // BF16_SUBLANES
    nc = hw // LANES
    return pl.pallas_call(
        _shortconv_kernel,
        grid=(B, L // tl),
        in_specs=[pl.BlockSpec((1, tl, 3 * hw), lambda b, i: (b, i, 0)),
                  pl.BlockSpec((1, BF16_SUBLANES, 3 * hw), lambda b, i: (b, jnp.maximum(i * hb - 1, 0), 0)),
                  pl.BlockSpec((1, BF16_SUBLANES, 3 * hw), lambda b, i: (b, jnp.minimum((i + 1) * hb, nhb - 1), 0)),
                  pl.BlockSpec((3, 3 * hw), lambda b, i: (0, 0))],
        out_specs=pl.BlockSpec((3, nc, 1, tl // 2, LANES), lambda b, i: (0, 0, b, i, 0)),
        out_shape=jax.ShapeDtypeStruct((3, nc, B, L // 2, LANES), U32),
        compiler_params=_cparams(("parallel", "parallel")),
        name="shortconv",
    )(proj3, proj3, proj3, conv_w)


def _split_bf16(x):
    hi = x.astype(BF16)
    return hi, (x - hi.astype(F32)).astype(BF16)


def _dot3(a, b):
    ah, al = _split_bf16(a)
    bh, bl = _split_bf16(b)
    d = lambda x, y: jnp.dot(x, y, preferred_element_type=F32)
    return d(ah, bh) + (d(ah, bl) + d(al, bh))


def _filter_kernel(zz_ref, w1_ref, b1_ref, f1_ref, w2_ref, b2_ref, f2_ref, w3_ref, dl_ref, o_ref):
    zz = zz_ref[...]
    h = _dot3(zz, w1_ref[...])
    h = jnp.sin(f1_ref[...] * (h + b1_ref[...]))
    h = _dot3(h, w2_ref[...])
    h = jnp.sin(f2_ref[...] * (h + b2_ref[...]))
    hf = _dot3(h, w3_ref[0])
    t = zz[:, 0:1]
    mask = zz[:, _MASK_COL:_MASK_COL + 1]
    kern = hf * (jnp.exp(-t * dl_ref[...]) * mask)
    for c in range(o_ref.shape[0]):
        o_ref[c] = kern[:, c * LANES:(c + 1) * LANES]


_MASK_COL = 2 * FILTER_BANDS + 1


def _filter_positions(L):
    bands = FILTER_BANDS
    t = np.linspace(0.0, 1.0, L, dtype=np.float64)[:, None]
    w = 2.0 * math.pi * np.arange(L, dtype=np.float64)[:, None] / L
    f = np.linspace(1e-4, bands - 1, bands, dtype=np.float64)[None, :]
    z = np.concatenate([t, np.cos(f * w), -np.sin(f * w)], axis=-1)
    zz = np.zeros((2 * L, LANES), np.float64)
    zz[:L, :z.shape[1]] = z
    zz[L + 1:, :z.shape[1]] = z[1:][::-1]
    zz[:, _MASK_COL] = 1.0
    zz[L, :] = 0.0
    return zz.astype(np.float32)


def _hyena_time_filters(L, w_f1, b_f1, freq1, w_f2, b_f2, freq2, w_f3, hw):
    emb, hid = w_f1.shape
    zz = jnp.asarray(_filter_positions(L))
    w1 = jnp.zeros((LANES, hid), F32).at[:emb].set(w_f1)
    w3 = w_f3.reshape(hid, 2, 2, hw).transpose(2, 0, 1, 3).reshape(2, hid, 2 * hw)
    max_decay = math.log(HYENA_TARGET) / FAST_DECAY_PCT
    min_decay = math.log(HYENA_TARGET) / SLOW_DECAY_PCT
    deltas = np.abs(np.linspace(min_decay, max_decay, hw, dtype=np.float64))
    dl = jnp.asarray(np.tile(deltas, 2)[None, :], F32)
    tr = min(512, L)
    per_dir = L // tr
    row = lambda a: a.reshape(1, -1)
    return pl.pallas_call(
        _filter_kernel,
        grid=(2 * L // tr,),
        in_specs=[pl.BlockSpec((tr, LANES), lambda i: (i, 0)),
                  pl.BlockSpec((LANES, hid), lambda i: (0, 0)),
                  pl.BlockSpec((1, hid), lambda i: (0, 0)),
                  pl.BlockSpec((1, hid), lambda i: (0, 0)),
                  pl.BlockSpec((hid, hid), lambda i: (0, 0)),
                  pl.BlockSpec((1, hid), lambda i: (0, 0)),
                  pl.BlockSpec((1, hid), lambda i: (0, 0)),
                  pl.BlockSpec((1, hid, 2 * hw), lambda i: (i // per_dir, 0, 0)),
                  pl.BlockSpec((1, 2 * hw), lambda i: (0, 0))],
        out_specs=pl.BlockSpec((2 * hw // LANES, tr, LANES), lambda i: (0, i, 0)),
        out_shape=jax.ShapeDtypeStruct((2 * hw // LANES, 2 * L, LANES), F32),
        compiler_params=_cparams(("parallel",)),
        name="hyena_filter",
    )(zz, w1, row(b_f1), row(freq1), w_f2, row(b_f2), row(freq2), w3, dl)


def _dft_matrices(L):
    N = 2 * L
    N1 = DFT_N1
    N2 = N // N1
    h = N2 // 2
    ang = 2.0 * np.pi * np.outer(np.arange(N2), np.arange(N2)) / N2
    c, s = np.cos(ang), np.sin(ang)
    m1 = np.zeros((N2, 2, N2))
    m1[:, 0, :h], m1[:, 0, h:] = c[:, :h], s[:, :h]
    m1[:, 1, :h], m1[:, 1, h:] = -s[:, :h], c[:, :h]
    m1 = m1.reshape(2 * N2, N2)
    m1f = np.stack([c, -s], axis=1).reshape(2 * N2, N2)
    m3 = np.zeros((2, h, N2, 2))
    m3[0, :, :, 0], m3[0, :, :, 1] = c[:h], -s[:h]
    m3[1, :, :, 0], m3[1, :, :, 1] = s[:h], c[:h]
    m3 = (m3 / N).reshape(N2, 2 * N2)
    def mid(order, transposed):
        f = np.arange(2 * N1)[:, None]
        d = np.asarray(order)[None, :]
        a = 2.0 * np.pi * ((d % N1) * (f % N1) % N1) / N1
        same, up = (f // N1 == d // N1), (f // N1 < d // N1)
        pc = np.where(same, 1.0, 0.0)
        ps = np.where(same, 0.0, np.where(up, 1.0, -1.0))
        u = pc * np.cos(a) + ps * np.sin(a)
        v = ps * np.cos(a) - pc * np.sin(a)
        b = 2.0 * np.pi * (np.arange(N2)[:, None] * (d % N1) % N) / N
        if transposed:
            u, v = u.T, v.T
            cb, sb = jnp.asarray(np.cos(b), F32)[:, :, None], jnp.asarray(np.sin(b), F32)[:, :, None]
        else:
            cb, sb = jnp.asarray(np.cos(b), F32)[:, None, :], jnp.asarray(np.sin(b), F32)[:, None, :]
        return jnp.asarray(u, F32)[None] * cb + jnp.asarray(v, F32)[None] * sb

    packed = _mid_row_order(DFT_PAIRS, 2)
    eye2 = np.eye(2)
    return dict(m1x=jnp.asarray(np.kron(m1, eye2), BF16),
                m3x=jnp.asarray(np.kron(m3, eye2), BF16),
                m1f=jnp.asarray(m1f, F32),
                m2f=mid(_mid_row_order(DFT_FILTER_STEP, 1), False),
                m2p=mid(packed, False).astype(BF16),
                m2tp=mid(packed, True).astype(BF16))


def _dft_filter_a_kernel(m_ref, k_ref, o_ref, *, step, n2):
    j0 = pl.program_id(1) * step
    rows = m_ref.shape[0]
    for jj in range(step):
        rhs = jnp.concatenate([k_ref[c, pl.ds(j0 + jj, n2, stride=DFT_N1), :] for c in range(k_ref.shape[0])],
                              axis=1)
        res = _dot3(m_ref[...], rhs)
        for c in range(o_ref.shape[0]):
            o_ref[c, 0, pl.ds(jj, rows, stride=step), :] = res[:, c * LANES:(c + 1) * LANES]


def _dft_filter_a(m1f, kern):
    NB, N, _ = kern.shape
    n2 = N // DFT_N1
    step, ncb = DFT_FILTER_STEP, DFT_LANE_BLOCKS
    rows = m1f.shape[0]
    return pl.pallas_call(
        functools.partial(_dft_filter_a_kernel, step=step, n2=n2),
        grid=(NB // ncb, DFT_N1 // step),
        in_specs=[pl.BlockSpec(m1f.shape, lambda h, j: (0, 0)),
                  pl.BlockSpec((ncb, N, LANES), lambda h, j: (h, 0, 0))],
        out_specs=pl.BlockSpec((ncb, 1, rows * step, LANES), lambda h, j: (h, j, 0, 0)),
        out_shape=jax.ShapeDtypeStruct((NB, DFT_N1 // step, rows * step, LANES), F32),
        compiler_params=_cparams(("parallel", "arbitrary")),
        name="dft_filter_a",
    )(m1f, kern)


def _dft_filt_kernel(a_ref, m_ref, o_ref):
    rows = m_ref.shape[1]
    a = jnp.concatenate([a_ref[c, :, 0].reshape(rows, LANES) for c in range(a_ref.shape[0])], axis=1)
    o_ref[0] = _dot3(m_ref[0], a)


def _dft_filt(a, m2):
    NB, G, N2, W, _ = a.shape
    R = m2.shape[1]
    return pl.pallas_call(
        _dft_filt_kernel,
        grid=(N2,),
        in_specs=[pl.BlockSpec((NB, G, 1, W, LANES), lambda k: (0, 0, k, 0, 0)),
                  pl.BlockSpec((1, R, R), lambda k: (k, 0, 0))],
        out_specs=pl.BlockSpec((1, R, NB * LANES), lambda k: (k, 0, 0)),
        out_shape=jax.ShapeDtypeStruct((N2, R, NB * LANES), F32),
        compiler_params=_cparams(("parallel",)),
        name="dft_filter_mid",
    )(a, m2)


def _packed_rows(ref, lead, start, size, stride):
    cols = [jnp.concatenate([ref[(c,) + l + (pl.ds(start, size, stride=stride), slice(None))] for l in lead], axis=0)
            for c in range(ref.shape[0])]
    return pltpu.bitcast(jnp.concatenate(cols, axis=1), BF16)


def _dft_a_pk_kernel(m_ref, z_ref, o_ref, *, tm, nh, npairs):
    j0 = pl.program_id(2) * tm
    rows = m_ref.shape[0] // 2
    for j in range(tm):
        rhs = _packed_rows(z_ref, [(0,), (1,)], j0 + j, nh, npairs)
        res = jnp.dot(m_ref[...], rhs, preferred_element_type=F32)
        pk = pltpu.bitcast(res.astype(BF16), U32)
        for c in range(o_ref.shape[0]):
            o_ref[c, 0, 0, pl.ds(j, rows, stride=tm), :] = pk[:, c * LANES:(c + 1) * LANES]


def _dft_a_pk(m1x, z, slot):
    _, NC, B, L2, _ = z.shape
    npairs = DFT_N1 // 2
    nh = L2 // npairs
    tm, ncb = DFT_PAIRS, DFT_LANE_BLOCKS
    rows = m1x.shape[0] // 2
    P = B // 2
    return pl.pallas_call(
        functools.partial(_dft_a_pk_kernel, tm=tm, nh=nh, npairs=npairs),
        grid=(P, NC // ncb, npairs // tm),
        in_specs=[pl.BlockSpec(m1x.shape, lambda p, h, j: (0, 0)),
                  pl.BlockSpec((None, ncb, 2, L2, LANES), lambda p, h, j: (slot, h, p, 0, 0))],
        out_specs=pl.BlockSpec((ncb, 1, 1, rows * tm, LANES), lambda p, h, j: (h, p, j, 0, 0)),
        out_shape=jax.ShapeDtypeStruct((NC, P, npairs // tm, rows * tm, LANES), U32),
        compiler_params=_cparams(("parallel", "parallel", "arbitrary")),
        name="dft_a",
    )(m1x, z)


def _dft_mid_kernel(a_ref, m_ref, mt_ref, kf_ref, o_ref):
    m = m_ref[0]
    mt = mt_ref[0]
    n1 = m.shape[0] // 2
    kr = kf_ref[0, :n1]
    ki = kf_ref[0, n1:]
    nc, npair = a_ref.shape[0], a_ref.shape[1]
    blk = a_ref.shape[2], a_ref.shape[4], LANES
    for p in range(npair):
        cols = [a_ref[c, p, :, 0].reshape(n1, LANES) for c in range(nc)]
        a = pltpu.bitcast(jnp.concatenate(cols, axis=1), BF16)
        x = jnp.dot(m, a, preferred_element_type=F32)
        xr = x[:n1]
        xi = x[n1:]
        y = jnp.concatenate([xr * kr - xi * ki, xr * ki + xi * kr], axis=0).astype(BF16)
        pk = pltpu.bitcast(jnp.dot(mt, y, preferred_element_type=F32).astype(BF16), U32)
        for c in range(nc):
            o_ref[c, p, :, 0] = pk[:, c * LANES:(c + 1) * LANES].reshape(blk)


def _dft_mid(a, m2, m2t, kf, order):
    NC, P, G, N2, W, _ = a.shape
    R = m2.shape[1]
    C = NC * LANES
    blk = pl.BlockSpec((NC, P, G, 1, W, LANES), lambda k: (0, 0, 0, k, 0, 0))
    return pl.pallas_call(
        _dft_mid_kernel,
        grid=(N2,),
        in_specs=[blk,
                  pl.BlockSpec((1, R, R), lambda k: (k, 0, 0)),
                  pl.BlockSpec((1, R, R), lambda k: (k, 0, 0)),
                  pl.BlockSpec((1, R, C), lambda k: (k, 0, order))],
        out_specs=blk,
        out_shape=jax.ShapeDtypeStruct(a.shape, U32),
        compiler_params=_cparams(("parallel",)),
        name="dft_mid",
    )(a, m2, m2t, kf)


def _dft_c_pk_kernel(m_ref, b_ref, z_ref, g_ref, bias_ref, o_ref, *, tm, nh, npairs):
    j0 = pl.program_id(2) * tm
    rows = b_ref.shape[3] // tm
    bias = bias_ref[...]
    both = [(0,), (1,)]
    for j in range(tm):
        rhs = _packed_rows(b_ref, [(0, 0)], j, rows, tm)
        y = jnp.dot(m_ref[...], rhs, preferred_element_type=F32)
        z = _packed_rows(z_ref, both, j0 + j, nh, npairs).astype(F32)
        g = _packed_rows(g_ref, both, j0 + j, nh, npairs).astype(F32)
        pk = pltpu.bitcast((g * (y + z * bias)).astype(BF16), U32)
        for c in range(o_ref.shape[0]):
            for b in range(2):
                o_ref[c, b, pl.ds(j0 + j, nh, stride=npairs), :] = pk[b * nh:(b + 1) * nh, c * LANES:(c + 1) * LANES]


def _dft_c_pk(m3x, b, z, zslot, g, gslot, bias):
    NC, P, G, W, _ = b.shape
    L2 = z.shape[3]
    npairs = DFT_N1 // 2
    nh = L2 // npairs
    tm, ncb = DFT_PAIRS, DFT_LANE_BLOCKS
    seq = lambda slot: pl.BlockSpec((None, ncb, 2, L2, LANES), lambda p, h, j: (slot, h, p, 0, 0))
    return pl.pallas_call(
        functools.partial(_dft_c_pk_kernel, tm=tm, nh=nh, npairs=npairs),
        grid=(P, NC // ncb, G),
        in_specs=[pl.BlockSpec(m3x.shape, lambda p, h, j: (0, 0)),
                  pl.BlockSpec((ncb, 1, 1, W, LANES), lambda p, h, j: (h, p, j, 0, 0)),
                  seq(zslot), seq(gslot),
                  pl.BlockSpec((1, ncb * LANES), lambda p, h, j: (0, h))],
        out_specs=seq(0),
        out_shape=jax.ShapeDtypeStruct((1, NC, 2 * P, L2, LANES), U32),
        compiler_params=_cparams(("parallel", "parallel", "arbitrary")),
        name="dft_c",
    )(m3x, b, z, g, bias)


def _mid_row_order(step, per):
    N1 = DFT_N1
    idx = []
    for mt in range(N1 // per // step):
        for ri in range(2):
            for jj in range(step):
                for sub in range(per):
                    idx.append(ri * N1 + per * (mt * step + jj) + sub)
    return np.asarray(idx, np.int32)


def _hyena(u3, kern, hy_bias, mats):
    _, NC, B, L2, _ = u3.shape
    C = NC * LANES
    N1 = DFT_N1
    N2 = 4 * L2 // N1
    af = _dft_filter_a(mats["m1f"], kern)
    NB, G, W, _ = af.shape
    kf = _dft_filt(af.reshape(NB, G, N2, W // N2, LANES), mats["m2f"])
    bias = hy_bias.astype(F32)

    def long_conv(z, zslot, gslot, order):
        a = _dft_a_pk(mats["m1x"], z, zslot)
        NCa, P, G, W, _ = a.shape
        b = _dft_mid(a.reshape(NCa, P, G, N2, W // N2, LANES), mats["m2p"], mats["m2tp"], kf, order)
        return _dft_c_pk(mats["m3x"], b.reshape(a.shape), z, zslot, u3, gslot, bias[order][None, :])

    z = long_conv(u3, 2, 0, 0)
    return long_conv(z, 0, 1, 1)


def _qk_kernel(q_ref, k_ref, cos_ref, sin_ref, qg_ref, kg_ref, grp_ref, qo_ref, ko_ref):
    cos = cos_ref[...]
    sin = sin_ref[...]
    lane = lax.broadcasted_iota(jnp.int32, cos.shape, 1)
    first = (lane % HEAD_DIM) < (HEAD_DIM // 2)
    grp = grp_ref[...]

    def one(x_ref, g_ref, o_ref):
        for h in range(x_ref.shape[2] // LANES):
            x = x_ref[0, :, h * LANES:(h + 1) * LANES].astype(F32)
            sq_hi, sq_lo = _split_bf16(x * x)
            ss = (jnp.dot(sq_hi, grp, preferred_element_type=F32)
                  + jnp.dot(sq_lo, grp, preferred_element_type=F32))
            xn = x * lax.rsqrt(ss * (1.0 / HEAD_DIM) + RMS_EPS) * g_ref[...]
            partner = jnp.where(first, pltpu.roll(xn, LANES - HEAD_DIM // 2, 1),
                                pltpu.roll(xn, HEAD_DIM // 2, 1))
            o_ref[0, :, h * LANES:(h + 1) * LANES] = (xn * cos + partner * sin).astype(o_ref.dtype)

    one(q_ref, qg_ref, qo_ref)
    one(k_ref, kg_ref, ko_ref)


def _rope_tables(L):
    half = HEAD_DIM // 2
    inv = 1.0 / (ROPE_THETA ** (np.arange(0, HEAD_DIM, 2, dtype=np.float64) / HEAD_DIM))
    ang = np.arange(L, dtype=np.float64)[:, None] * inv[None, :]
    cos, sin = np.cos(ang), np.sin(ang)
    cos_t = np.tile(cos, (1, LANES // half))
    sin_t = np.tile(np.concatenate([-sin, sin], axis=1), (1, LANES // HEAD_DIM))
    return jnp.asarray(cos_t, F32), jnp.asarray(sin_t, F32)


def _qk_prep(proj3, q_norm, k_norm, qcol, kcol, width):
    B, L, _ = proj3.shape
    tl = min(512, L)
    cos_t, sin_t = _rope_tables(L)
    qg = jnp.tile(q_norm.astype(F32), LANES // HEAD_DIM)[None, :] * F32(HEAD_DIM ** -0.5 * math.log2(math.e))
    kg = jnp.tile(k_norm.astype(F32), LANES // HEAD_DIM)[None, :]
    lanes = np.arange(LANES)
    grp = jnp.asarray(lanes[:, None] // HEAD_DIM == lanes[None, :] // HEAD_DIM, BF16)
    blk = lambda col: pl.BlockSpec((1, tl, width), lambda b, i: (b, i, col))
    tab = pl.BlockSpec((tl, LANES), lambda b, i: (i, 0))
    vec = pl.BlockSpec((1, LANES), lambda b, i: (0, 0))
    out = pl.BlockSpec((1, tl, width), lambda b, i: (b, i, 0))
    return pl.pallas_call(
        _qk_kernel,
        grid=(B, L // tl),
        in_specs=[blk(qcol), blk(kcol), tab, tab, vec, vec, pl.BlockSpec((LANES, LANES), lambda b, i: (0, 0))],
        out_specs=[out, out],
        out_shape=[jax.ShapeDtypeStruct((B, L, width), BF16)] * 2,
        compiler_params=_cparams(("parallel", "parallel")),
        name="qk_norm_rope",
    )(proj3, proj3, cos_t, sin_t, qg, kg, grp)


SHIFT_LIMIT = 50.0


def _attn_kernel(lam_ref, q_ref, k_ref, v_ref, g_ref, o_ref, kmax_ref, *, post):
    k = k_ref[0]
    v = v_ref[0]

    @pl.when(pl.program_id(2) == 0)
    def _():
        kmax_ref[...] = jnp.max(jnp.abs(k.astype(F32)), axis=0, keepdims=True)

    q = q_ref[0]
    lane = lax.broadcasted_iota(jnp.int32, q.shape, 1)
    lo = lane < HEAD_DIM
    zero = jnp.zeros_like(q)
    reach = jnp.abs(q.astype(F32)) * kmax_ref[...]
    shifts = [jnp.sum(jnp.where(lo, reach, 0.0), axis=-1, keepdims=True),
              jnp.sum(jnp.where(lo, 0.0, reach), axis=-1, keepdims=True)]
    worst = jnp.max(jnp.maximum(shifts[0], shifts[1]))

    def run(row_max):
        outs = []
        for c in range(2):
            qc = jnp.where(lo, q, zero) if c == 0 else jnp.where(lo, zero, q)
            s = lax.dot_general(qc, k, (((1,), (1,)), ((), ())), preferred_element_type=F32)
            m = jnp.max(s, axis=-1, keepdims=True) if row_max else shifts[c]
            p = jnp.exp2(s - m)
            l = jnp.sum(p, axis=-1, keepdims=True)
            outs.append(jnp.dot(p.astype(BF16), v, preferred_element_type=F32) / l)
        a = outs[0] - lam_ref[0] * outs[1]
        ms = jnp.mean(a * a, axis=-1, keepdims=True)
        o_ref[0] = (a * lax.rsqrt(ms + RMS_EPS) * g_ref[...] * post).astype(o_ref.dtype)

    @pl.when(worst <= SHIFT_LIMIT)
    def _():
        run(False)

    @pl.when(jnp.logical_not(worst <= SHIFT_LIMIT))
    def _():
        run(True)


def _attention(qn, kn, proj3, vcol, lam, subln_g, post):
    B, L, W = qn.shape
    tq = min(512, L)
    return pl.pallas_call(
        functools.partial(_attn_kernel, post=post),
        grid=(B, W // V_DIM, L // tq),
        in_specs=[pl.BlockSpec(memory_space=pltpu.SMEM),
                  pl.BlockSpec((1, tq, V_DIM), lambda b, h, i: (b, i, h)),
                  pl.BlockSpec((1, L, V_DIM), lambda b, h, i: (b, 0, h)),
                  pl.BlockSpec((1, L, V_DIM), lambda b, h, i: (b, 0, vcol + h)),
                  pl.BlockSpec((1, V_DIM), lambda b, h, i: (0, 0))],
        out_specs=pl.BlockSpec((1, tq, V_DIM), lambda b, h, i: (b, i, h)),
        out_shape=jax.ShapeDtypeStruct((B, L, W), BF16),
        scratch_shapes=[pltpu.VMEM((1, V_DIM), F32)],
        compiler_params=_cparams(("parallel", "parallel", "arbitrary")),
        name="diff_attention",
    )(lam.reshape(1).astype(F32), qn, kn, proj3, subln_g.astype(F32).reshape(1, V_DIM))


def _merge_kernel(x_ref, yh_ref, ya_ref, gh_ref, ga_ref, who_ref, wao_ref, wout_ref, g_ref, *rest,
                  with_router):
    if with_router:
        wr_ref, xo_ref, hn_ref, lg_ref = rest
    else:
        xo_ref, hn_ref = rest
    yh = jnp.concatenate([pltpu.bitcast(yh_ref[c], BF16) for c in range(yh_ref.shape[0])], axis=1)
    th = jnp.dot(yh, who_ref[...], preferred_element_type=F32)
    ta = jnp.dot(ya_ref[...], wao_ref[...], preferred_element_type=F32)
    merged = (jax.nn.sigmoid(gh_ref[...].astype(F32)) * th
              + jax.nn.sigmoid(ga_ref[...].astype(F32)) * ta)
    xn = x_ref[...] + jnp.dot(merged.astype(BF16), wout_ref[...], preferred_element_type=F32)
    xo_ref[...] = xn
    ms = jnp.mean(xn * xn, axis=-1, keepdims=True)
    h = xn * lax.rsqrt(ms + RMS_EPS) * g_ref[...]
    hn_ref[...] = h.astype(hn_ref.dtype)
    if with_router:
        lane = lax.broadcasted_iota(jnp.int32, lg_ref.shape, 1)
        lg = jnp.zeros(lg_ref.shape, F32)
        for e in range(wr_ref.shape[0]):
            lg = jnp.where(lane == e, jnp.sum(h * wr_ref[e:e + 1, :], axis=-1, keepdims=True), lg)
        lg_ref[...] = lg


def _merge(x2, yh, ya, proj, gcol, who, wao, wout, g_ffn, w_router):
    T, D = x2.shape
    W = ya.shape[1]
    tm = min(512, T)
    with_router = w_router is not None
    row = lambda w: pl.BlockSpec((tm, w), lambda i: (i, 0))
    full = lambda a: pl.BlockSpec(a.shape, lambda i: (0, 0))
    in_specs = [row(D), pl.BlockSpec((yh.shape[0], tm // 2, LANES), lambda i: (0, i, 0)), row(W),
                pl.BlockSpec((tm, D), lambda i: (i, gcol)),
                pl.BlockSpec((tm, D), lambda i: (i, gcol + 1)),
                full(who), full(wao), full(wout), pl.BlockSpec((1, D), lambda i: (0, 0))]
    args = [x2, yh, ya, proj, proj, who, wao, wout, g_ffn.astype(F32).reshape(1, D)]
    out_specs = [row(D), row(D)]
    out_shape = [jax.ShapeDtypeStruct((T, D), F32),
                 jax.ShapeDtypeStruct((T, D), F32 if with_router else BF16)]
    if with_router:
        wr = w_router.astype(F32).T
        in_specs.append(full(wr))
        args.append(wr)
        out_specs.append(row(LANES))
        out_shape.append(jax.ShapeDtypeStruct((T, LANES), F32))
    return pl.pallas_call(
        functools.partial(_merge_kernel, with_router=with_router),
        grid=(T // tm,),
        in_specs=in_specs,
        out_specs=out_specs,
        out_shape=out_shape,
        compiler_params=_cparams(("parallel",)),
        name="merge_outproj",
    )(*args)


def _swiglu(h, wg_ref, wu_ref, wd_ref, chunks):
    F = wg_ref.shape[-1]
    fc = F // chunks
    acc = None
    for c in range(chunks):
        sl = slice(c * fc, (c + 1) * fc)
        g = jnp.dot(h, wg_ref[:, sl], preferred_element_type=F32)
        u = jnp.dot(h, wu_ref[:, sl], preferred_element_type=F32)
        a = (g * jax.nn.sigmoid(g) * u).astype(BF16)
        d = jnp.dot(a, wd_ref[sl, :], preferred_element_type=F32)
        acc = d if acc is None else acc + d
    return acc


def _ffn_kernel(h_ref, x_ref, wg_ref, wu_ref, wd_ref, o_ref, *, chunks):
    o_ref[...] = x_ref[...] + _swiglu(h_ref[...], wg_ref, wu_ref, wd_ref, chunks)


def _ff_chunks(F):
    return F // (2 * LANES) if F % (2 * LANES) == 0 else 1


def _dense_ffn(hn, x2, wg, wu, wd):
    T, D = x2.shape
    F = wg.shape[1]
    tm = min(512, T)
    row = pl.BlockSpec((tm, D), lambda i: (i, 0))
    return pl.pallas_call(
        functools.partial(_ffn_kernel, chunks=_ff_chunks(F)),
        grid=(T // tm,),
        in_specs=[row, row,
                  pl.BlockSpec((D, F), lambda i: (0, 0)),
                  pl.BlockSpec((D, F), lambda i: (0, 0)),
                  pl.BlockSpec((F, D), lambda i: (0, 0))],
        out_specs=row,
        out_shape=jax.ShapeDtypeStruct((T, D), F32),
        compiler_params=_cparams(("parallel",)),
        name="dense_ffn",
    )(hn, x2, wg, wu, wd)


def _row_copy(src_hbm, row, dst, r, sem):
    return pltpu.make_async_copy(src_hbm.at[pl.ds(row, 1)], dst.at[pl.ds(r, 1)], sem)


def _rows_done(src_hbm, dst, sem):
    return pltpu.make_async_copy(src_hbm.at[pl.ds(0, dst.shape[0])], dst, sem)


def _expert_kernel(be_ref, tok_ref, hn_hbm, wg_ref, wu_ref, wd_ref, o_ref, xbuf, sem, *, chunks):
    i = pl.program_id(0)
    last = pl.num_programs(0) - 1
    rows = xbuf.shape[1]
    slot = i % 2
    nxt = 1 - slot

    @pl.when(i == 0)
    def _():
        def issue(r, carry):
            _row_copy(hn_hbm, tok_ref[r], xbuf.at[0], r, sem.at[0]).start()
            return carry

        lax.fori_loop(0, rows, issue, 0, unroll=8)

    _rows_done(hn_hbm, xbuf.at[slot], sem.at[slot]).wait()
    base = jnp.minimum(i + 1, last) * rows
    for r in range(rows):
        _row_copy(hn_hbm, tok_ref[base + r], xbuf.at[nxt], r, sem.at[nxt]).start()
    o_ref[...] = _swiglu(xbuf[slot].astype(BF16), wg_ref.at[0], wu_ref.at[0], wd_ref.at[0], chunks)

    @pl.when(i == last)
    def _():
        _rows_done(hn_hbm, xbuf.at[nxt], sem.at[nxt]).wait()


def _experts(block_e, tok_of_slot, hn, wg, wu, wd, layer):
    T, D = hn.shape
    F = wg.shape[3]
    P = tok_of_slot.shape[0]
    rows = MOE_ROWS
    wspec = lambda shape: pl.BlockSpec((None, 1) + shape, lambda i, be, tok: (layer, be[i], 0, 0))
    return pl.pallas_call(
        functools.partial(_expert_kernel, chunks=_ff_chunks(F)),
        grid_spec=pltpu.PrefetchScalarGridSpec(
            num_scalar_prefetch=2,
            grid=(P // rows,),
            in_specs=[pl.BlockSpec(memory_space=pl.ANY),
                      wspec((D, F)), wspec((D, F)), wspec((F, D))],
            out_specs=pl.BlockSpec((rows, D), lambda i, be, tok: (i, 0)),
            scratch_shapes=[pltpu.VMEM((2, rows, D), F32), pltpu.SemaphoreType.DMA((2,))]),
        out_shape=jax.ShapeDtypeStruct((P, D), F32),
        compiler_params=_cparams(("arbitrary",)),
        name="moe_experts",
    )(block_e, tok_of_slot, hn, wg, wu, wd)


def _combine_kernel(dest_ref, yp_hbm, x_ref, w_ref, o_ref, buf0, buf1, sem):
    i = pl.program_id(0)
    rows = buf0.shape[1]

    def gather(tile, slot):
        base = tile * rows * TOP_K

        def issue(r, carry):
            _row_copy(yp_hbm, dest_ref[base + TOP_K * r], buf0.at[slot], r, sem.at[slot]).start()
            _row_copy(yp_hbm, dest_ref[base + TOP_K * r + 1], buf1.at[slot], r, sem.at[slot]).start()
            return carry

        lax.fori_loop(0, rows, issue, 0, unroll=8)

    @pl.when(i == 0)
    def _():
        gather(0, 0)

    @pl.when(i + 1 < pl.num_programs(0))
    def _():
        gather(i + 1, (i + 1) % 2)

    slot = i % 2
    _rows_done(yp_hbm, buf0.at[slot], sem.at[slot]).wait()
    _rows_done(yp_hbm, buf1.at[slot], sem.at[slot]).wait()
    w = w_ref[...]
    o_ref[...] = x_ref[...] + (buf0[slot] * w[:, 0:1] + buf1[slot] * w[:, 1:2])


def _combine(dest, yp, x2, top_w):
    T, D = x2.shape
    tm = min(256, T)
    return pl.pallas_call(
        _combine_kernel,
        grid_spec=pltpu.PrefetchScalarGridSpec(
            num_scalar_prefetch=1,
            grid=(T // tm,),
            in_specs=[pl.BlockSpec(memory_space=pl.ANY),
                      pl.BlockSpec((tm, D), lambda i, d: (i, 0)),
                      pl.BlockSpec((tm, TOP_K), lambda i, d: (i, 0))],
            out_specs=pl.BlockSpec((tm, D), lambda i, d: (i, 0)),
            scratch_shapes=[pltpu.VMEM((2, tm, D), F32), pltpu.VMEM((2, tm, D), F32),
                            pltpu.SemaphoreType.DMA((2,))]),
        out_shape=jax.ShapeDtypeStruct((T, D), F32),
        compiler_params=_cparams(("arbitrary",)),
        name="moe_combine",
    )(dest, yp, x2, top_w)


def _route(logits, n_experts, rows):
    T = logits.shape[0]
    top_val, top_idx = lax.top_k(logits[:, :n_experts], TOP_K)
    top_w = jax.nn.softmax(top_val, axis=-1)
    A = T * TOP_K
    e_flat = top_idx.reshape(A).astype(jnp.int32)
    onehot = (e_flat[:, None] == jnp.arange(n_experts, dtype=jnp.int32)[None, :]).astype(jnp.int32)
    csum = jnp.cumsum(onehot, axis=0)
    counts = csum[-1]
    rank = jnp.sum((csum - onehot) * onehot, axis=1)
    padded = (counts + rows - 1) // rows * rows
    pad_end = jnp.cumsum(padded)
    pad_start = pad_end - padded
    dest = pad_start[e_flat] + rank
    n_blocks = -(-A // rows) + n_experts
    P = n_blocks * rows
    order = jnp.argsort(e_flat)
    starts = jnp.cumsum(counts) - counts
    slot = jnp.arange(P, dtype=jnp.int32)
    slot_e = jnp.minimum(jnp.sum((slot[:, None] >= pad_end[None, :]).astype(jnp.int32), axis=1),
                         n_experts - 1)
    within = slot - pad_start[slot_e]
    valid = within < counts[slot_e]
    src = jnp.clip(starts[slot_e] + within, 0, A - 1)
    tok_of_slot = jnp.where(valid, order[src] // TOP_K, 0).astype(jnp.int32)
    block_e = slot_e[::rows]
    return top_w.astype(F32), dest.astype(jnp.int32), tok_of_slot, block_e


def _pad_ff(w, axis):
    F = w.shape[axis]
    pad = [(0, 0)] * w.ndim
    pad[axis] = (0, _round_up(F, 2 * LANES) - F)
    return jnp.pad(w.astype(BF16), pad)


def kernel(x, norm_mix, w_in, conv_w, w_f1, b_f1, freq1, w_f2, b_f2, freq2, w_f3, hy_bias, q_norm, k_norm, lam_q1, lam_k1, lam_q2, lam_k2, subln_g, w_hy_o, w_at_o, w_out, norm_ffn, w_dense_gate, w_dense_up, w_dense_down, w_router, w_moe_gate, w_moe_up, w_moe_down):
    B, L, D = x.shape
    T = B * L
    depth = w_in.shape[0]
    hw = w_hy_o.shape[1]
    qk_w = N_HEADS * 2 * HEAD_DIM
    at_w = N_HEADS * V_DIM
    qcol = 3 * hw // qk_w
    kcol = qcol + 1
    vcol = (3 * hw + 2 * qk_w) // V_DIM
    gcol = (3 * hw + 2 * qk_w + at_w) // D
    mats = _dft_matrices(L)
    moe_w = (_pad_ff(w_moe_gate, 3), _pad_ff(w_moe_up, 3), _pad_ff(w_moe_down, 2))

    x2 = x.reshape(T, D)
    for layer in range(depth):
        lambda_init = 0.8 - 0.6 * math.exp(-0.3 * layer)
        proj = _inproj(x2, norm_mix[layer].astype(F32), w_in[layer].astype(BF16))
        proj3 = proj.reshape(B, L, -1)
        u3 = _shortconv(proj3, conv_w[layer].astype(F32), hw)
        kern = _hyena_time_filters(L, w_f1[layer], b_f1[layer], freq1[layer], w_f2[layer], b_f2[layer],
                                   freq2[layer], w_f3[layer], hw)
        y_h = _hyena(u3, kern, hy_bias[layer], mats)
        qn, kn = _qk_prep(proj3, q_norm[layer], k_norm[layer], qcol, kcol, qk_w)
        lam = (jnp.exp(jnp.sum(lam_q1[layer].astype(F32) * lam_k1[layer].astype(F32)))
               - jnp.exp(jnp.sum(lam_q2[layer].astype(F32) * lam_k2[layer].astype(F32))) + lambda_init)
        y_a = _attention(qn, kn, proj3, vcol, lam, subln_g[layer], 1.0 - lambda_init)
        i = layer // 2
        moe = layer % 2 == 1
        outs = _merge(x2, y_h.reshape(hw // LANES, T // 2, LANES), y_a.reshape(T, at_w), proj, gcol,
                      w_hy_o[layer].astype(BF16), w_at_o[layer].astype(BF16), w_out[layer].astype(BF16),
                      norm_ffn[layer], w_router[i] if moe else None)
        if not moe:
            x2, hn = outs
            x2 = _dense_ffn(hn, x2, _pad_ff(w_dense_gate[i], 1), _pad_ff(w_dense_up[i], 1),
                            _pad_ff(w_dense_down[i], 0))
        else:
            x2, hn, logits = outs
            n_experts = w_router.shape[2]
            top_w, dest, tok_of_slot, block_e = _route(logits, n_experts, MOE_ROWS)
            yp = _experts(block_e, tok_of_slot, hn, *moe_w, i)
            x2 = _combine(dest, yp, x2, top_w)
    return x2.reshape(B, L, D)
```

```python
import functools
import math

import numpy as np
import jax
import jax.numpy as jnp
from jax import lax
from jax.experimental import pallas as pl
from jax.experimental.pallas import tpu as pltpu

F32 = jnp.float32
BF16 = jnp.bfloat16
U32 = jnp.uint32

N_HEADS = 4
HEAD_DIM = 64
V_DIM = 2 * HEAD_DIM
ROPE_THETA = 10000.0
FILTER_BANDS = 16
HYENA_TARGET = 1e-2
FAST_DECAY_PCT = 0.3
SLOW_DECAY_PCT = 1.5
TOP_K = 2
RMS_EPS = 1e-6

LANES = 128
BF16_SUBLANES = 16
VMEM_LIMIT = 56 * 1024 * 1024

DFT_N1 = 128
DFT_PAIRS = 16
DFT_FILTER_STEP = 16
DFT_LANE_BLOCKS = 2
MOE_ROWS = 512


def _cparams(sem):
    return pltpu.CompilerParams(dimension_semantics=sem, vmem_limit_bytes=VMEM_LIMIT)


def _round_up(a, b):
    return (a + b - 1) // b * b


def _inproj_kernel(x_ref, xp_ref, xn_ref, g_ref, w_ref, cw_ref, cos_ref, sin_ref, qg_ref, kg_ref, grp_ref,
                   u_ref, q_ref, k_ref, v_ref, gate_ref, *, hw, qk_w, tiles_per_seq):
    pos = pl.program_id(0) % tiles_per_seq
    tm = x_ref.shape[0]
    halo = xp_ref.shape[0]

    def normed(x):
        ms = jnp.mean(x * x, axis=-1, keepdims=True)
        return (x * lax.rsqrt(ms + RMS_EPS) * g_ref[...]).astype(BF16)

    hn = normed(x_ref[...])
    hext = jnp.concatenate([normed(xp_ref[...]), hn, normed(xn_ref[...])], axis=0)
    row = lax.broadcasted_iota(jnp.int32, (tm + 2 * halo, hw), 0)
    outside = jnp.logical_or(jnp.logical_and(row < halo, pos == 0),
                             jnp.logical_and(row >= tm + halo, pos == tiles_per_seq - 1))
    hy_all = jnp.dot(hext, w_ref[:, :3 * hw], preferred_element_type=F32)
    rest = jnp.dot(hn, w_ref[:, 3 * hw:], preferred_element_type=F32)

    nc = u_ref.shape[1]
    for s in range(u_ref.shape[0]):
        cols = slice(s * hw, (s + 1) * hw)
        hy = jnp.where(outside, 0.0, hy_all[:, cols])
        up = pltpu.roll(hy, 1, 0)[halo:halo + tm]
        dn = pltpu.roll(hy, tm + 2 * halo - 1, 0)[halo:halo + tm]
        cw = cw_ref[:, cols]
        res = (up * cw[0:1] + hy[halo:halo + tm] * cw[1:2] + dn * cw[2:3]).astype(BF16)
        pk = pltpu.bitcast(res, U32)
        for c in range(nc):
            u_ref[s, c] = pk[:, c * LANES:(c + 1) * LANES]

    cos = cos_ref[...]
    sin = sin_ref[...]
    lane = lax.broadcasted_iota(jnp.int32, cos.shape, 1)
    first = (lane % HEAD_DIM) < (HEAD_DIM // 2)
    grp = grp_ref[...]
    base = 0
    for o_ref, gain_ref in ((q_ref, qg_ref), (k_ref, kg_ref)):
        for h in range(qk_w // LANES):
            x = rest[:, base + h * LANES:base + (h + 1) * LANES]
            sq_hi, sq_lo = _split_bf16(x * x)
            ss = (jnp.dot(sq_hi, grp, preferred_element_type=F32)
                  + jnp.dot(sq_lo, grp, preferred_element_type=F32))
            xn = x * lax.rsqrt(ss * (1.0 / HEAD_DIM) + RMS_EPS) * gain_ref[...]
            partner = jnp.where(first, pltpu.roll(xn, LANES - HEAD_DIM // 2, 1),
                                pltpu.roll(xn, HEAD_DIM // 2, 1))
            o_ref[:, h * LANES:(h + 1) * LANES] = (xn * cos + partner * sin).astype(o_ref.dtype)
        base += qk_w
    v_ref[...] = rest[:, base:base + v_ref.shape[1]].astype(v_ref.dtype)
    gate_ref[...] = rest[:, base + v_ref.shape[1]:].astype(gate_ref.dtype)


def _inproj(x2, L, g, w, conv_w, q_norm, k_norm, hw, qk_w, at_w):
    T, D = x2.shape
    tm = min(512, L)
    halo = BF16_SUBLANES
    hb = tm // halo
    nhb = T // halo
    nc = hw // LANES
    gw = w.shape[1] - 3 * hw - 2 * qk_w - at_w
    cos_t, sin_t = _rope_tables(L)
    qg = jnp.tile(q_norm.astype(F32), LANES // HEAD_DIM)[None, :] * F32(HEAD_DIM ** -0.5 * math.log2(math.e))
    kg = jnp.tile(k_norm.astype(F32), LANES // HEAD_DIM)[None, :]
    lanes = np.arange(LANES)
    grp = jnp.asarray(lanes[:, None] // HEAD_DIM == lanes[None, :] // HEAD_DIM, BF16)
    tps = L // tm
    const = lambda a: pl.BlockSpec(a.shape, lambda i: (0,) * a.ndim)
    rows = lambda width: pl.BlockSpec((tm, width), lambda i: (i, 0))
    tab = pl.BlockSpec((tm, LANES), lambda i: (i % tps, 0))
    gr = g.reshape(1, D)
    return pl.pallas_call(
        functools.partial(_inproj_kernel, hw=hw, qk_w=qk_w, tiles_per_seq=tps),
        grid=(T // tm,),
        in_specs=[rows(D),
                  pl.BlockSpec((halo, D), lambda i: (jnp.maximum(i * hb - 1, 0), 0)),
                  pl.BlockSpec((halo, D), lambda i: (jnp.minimum((i + 1) * hb, nhb - 1), 0)),
                  const(gr), const(w), const(conv_w), tab, tab, const(qg), const(kg), const(grp)],
        out_specs=[pl.BlockSpec((3, nc, tm // 2, LANES), lambda i: (0, 0, i, 0)),
                   rows(qk_w), rows(qk_w), rows(at_w), rows(gw)],
        out_shape=[jax.ShapeDtypeStruct((3, nc, T // 2, LANES), U32),
                   jax.ShapeDtypeStruct((T, qk_w), BF16), jax.ShapeDtypeStruct((T, qk_w), BF16),
                   jax.ShapeDtypeStruct((T, at_w), BF16), jax.ShapeDtypeStruct((T, gw), BF16)],
        compiler_params=_cparams(("parallel",)),
        name="inproj",
    )(x2, x2, x2, gr, w, conv_w, cos_t, sin_t, qg, kg, grp)


def _rope_tables(L):
    half = HEAD_DIM // 2
    inv = 1.0 / (ROPE_THETA ** (np.arange(0, HEAD_DIM, 2, dtype=np.float64) / HEAD_DIM))
    ang = np.arange(L, dtype=np.float64)[:, None] * inv[None, :]
    cos, sin = np.cos(ang), np.sin(ang)
    cos_t = np.tile(cos, (1, LANES // half))
    sin_t = np.tile(np.concatenate([-sin, sin], axis=1), (1, LANES // HEAD_DIM))
    return jnp.asarray(cos_t, F32), jnp.asarray(sin_t, F32)


def _split_bf16(x):
    hi = x.astype(BF16)
    return hi, (x - hi.astype(F32)).astype(BF16)


def _dot3(a, b):
    ah, al = _split_bf16(a)
    bh, bl = _split_bf16(b)
    d = lambda x, y: jnp.dot(x, y, preferred_element_type=F32)
    return d(ah, bh) + (d(ah, bl) + d(al, bh))


def _filter_kernel(zz_ref, w1_ref, b1_ref, f1_ref, w2_ref, b2_ref, f2_ref, w3_ref, dl_ref, o_ref):
    zz = zz_ref[...]
    h = _dot3(zz, w1_ref[...])
    h = jnp.sin(f1_ref[...] * (h + b1_ref[...]))
    h = _dot3(h, w2_ref[...])
    h = jnp.sin(f2_ref[...] * (h + b2_ref[...]))
    hf = _dot3(h, w3_ref[0])
    t = zz[:, 0:1]
    mask = zz[:, _MASK_COL:_MASK_COL + 1]
    kern = hf * (jnp.exp(-t * dl_ref[...]) * mask)
    for c in range(o_ref.shape[0]):
        o_ref[c] = kern[:, c * LANES:(c + 1) * LANES]


_MASK_COL = 2 * FILTER_BANDS + 1


def _filter_positions(L):
    bands = FILTER_BANDS
    t = np.linspace(0.0, 1.0, L, dtype=np.float64)[:, None]
    w = 2.0 * math.pi * np.arange(L, dtype=np.float64)[:, None] / L
    f = np.linspace(1e-4, bands - 1, bands, dtype=np.float64)[None, :]
    z = np.concatenate([t, np.cos(f * w), -np.sin(f * w)], axis=-1)
    zz = np.zeros((2 * L, LANES), np.float64)
    zz[:L, :z.shape[1]] = z
    zz[L + 1:, :z.shape[1]] = z[1:][::-1]
    zz[:, _MASK_COL] = 1.0
    zz[L, :] = 0.0
    return zz.astype(np.float32)


def _hyena_time_filters(L, w_f1, b_f1, freq1, w_f2, b_f2, freq2, w_f3, hw):
    emb, hid = w_f1.shape
    zz = jnp.asarray(_filter_positions(L))
    w1 = jnp.zeros((LANES, hid), F32).at[:emb].set(w_f1)
    w3 = w_f3.reshape(hid, 2, 2, hw).transpose(2, 0, 1, 3).reshape(2, hid, 2 * hw)
    max_decay = math.log(HYENA_TARGET) / FAST_DECAY_PCT
    min_decay = math.log(HYENA_TARGET) / SLOW_DECAY_PCT
    deltas = np.abs(np.linspace(min_decay, max_decay, hw, dtype=np.float64))
    dl = jnp.asarray(np.tile(deltas, 2)[None, :], F32)
    tr = min(512, L)
    per_dir = L // tr
    row = lambda a: a.reshape(1, -1)
    return pl.pallas_call(
        _filter_kernel,
        grid=(2 * L // tr,),
        in_specs=[pl.BlockSpec((tr, LANES), lambda i: (i, 0)),
                  pl.BlockSpec((LANES, hid), lambda i: (0, 0)),
                  pl.BlockSpec((1, hid), lambda i: (0, 0)),
                  pl.BlockSpec((1, hid), lambda i: (0, 0)),
                  pl.BlockSpec((hid, hid), lambda i: (0, 0)),
                  pl.BlockSpec((1, hid), lambda i: (0, 0)),
                  pl.BlockSpec((1, hid), lambda i: (0, 0)),
                  pl.BlockSpec((1, hid, 2 * hw), lambda i: (i // per_dir, 0, 0)),
                  pl.BlockSpec((1, 2 * hw), lambda i: (0, 0))],
        out_specs=pl.BlockSpec((2 * hw // LANES, tr, LANES), lambda i: (0, i, 0)),
        out_shape=jax.ShapeDtypeStruct((2 * hw // LANES, 2 * L, LANES), F32),
        compiler_params=_cparams(("parallel",)),
        name="hyena_filter",
    )(zz, w1, row(b_f1), row(freq1), w_f2, row(b_f2), row(freq2), w3, dl)


def _dft_matrices(L):
    N = 2 * L
    N1 = DFT_N1
    N2 = N // N1
    h = N2 // 2
    ang = 2.0 * np.pi * np.outer(np.arange(N2), np.arange(N2)) / N2
    c, s = np.cos(ang), np.sin(ang)
    m1 = np.zeros((N2, 2, N2))
    m1[:, 0, :h], m1[:, 0, h:] = c[:, :h], s[:, :h]
    m1[:, 1, :h], m1[:, 1, h:] = -s[:, :h], c[:, :h]
    m1 = m1.reshape(2 * N2, N2)
    m1f = np.stack([c, -s], axis=1).reshape(2 * N2, N2)
    m3 = np.zeros((2, h, N2, 2))
    m3[0, :, :, 0], m3[0, :, :, 1] = c[:h], -s[:h]
    m3[1, :, :, 0], m3[1, :, :, 1] = s[:h], c[:h]
    m3 = (m3 / N).reshape(N2, 2 * N2)
    def mid(order, transposed):
        f = np.arange(2 * N1)[:, None]
        d = np.asarray(order)[None, :]
        a = 2.0 * np.pi * ((d % N1) * (f % N1) % N1) / N1
        same, up = (f // N1 == d // N1), (f // N1 < d // N1)
        pc = np.where(same, 1.0, 0.0)
        ps = np.where(same, 0.0, np.where(up, 1.0, -1.0))
        u = pc * np.cos(a) + ps * np.sin(a)
        v = ps * np.cos(a) - pc * np.sin(a)
        b = 2.0 * np.pi * (np.arange(N2)[:, None] * (d % N1) % N) / N
        if transposed:
            u, v = u.T, v.T
            cb, sb = jnp.asarray(np.cos(b), F32)[:, :, None], jnp.asarray(np.sin(b), F32)[:, :, None]
        else:
            cb, sb = jnp.asarray(np.cos(b), F32)[:, None, :], jnp.asarray(np.sin(b), F32)[:, None, :]
        return jnp.asarray(u, F32)[None] * cb + jnp.asarray(v, F32)[None] * sb

    packed = _mid_row_order(DFT_PAIRS, 2)
    eye2 = np.eye(2)
    return dict(m1x=jnp.asarray(np.kron(m1, eye2), BF16),
                m3x=jnp.asarray(np.kron(m3, eye2), BF16),
                m1f=jnp.asarray(m1f, F32),
                m2f=mid(_mid_row_order(DFT_FILTER_STEP, 1), False),
                m2p=mid(packed, False).astype(BF16),
                m2tp=mid(packed, True).astype(BF16))


def _dft_filter_a_kernel(m_ref, k_ref, o_ref, *, step, n2):
    j0 = pl.program_id(1) * step
    rows = m_ref.shape[0]
    for jj in range(step):
        rhs = jnp.concatenate([k_ref[c, pl.ds(j0 + jj, n2, stride=DFT_N1), :] for c in range(k_ref.shape[0])],
                              axis=1)
        res = _dot3(m_ref[...], rhs)
        for c in range(o_ref.shape[0]):
            o_ref[c, 0, pl.ds(jj, rows, stride=step), :] = res[:, c * LANES:(c + 1) * LANES]


def _dft_filter_a(m1f, kern):
    NB, N, _ = kern.shape
    n2 = N // DFT_N1
    step, ncb = DFT_FILTER_STEP, DFT_LANE_BLOCKS
    rows = m1f.shape[0]
    return pl.pallas_call(
        functools.partial(_dft_filter_a_kernel, step=step, n2=n2),
        grid=(NB // ncb, DFT_N1 // step),
        in_specs=[pl.BlockSpec(m1f.shape, lambda h, j: (0, 0)),
                  pl.BlockSpec((ncb, N, LANES), lambda h, j: (h, 0, 0))],
        out_specs=pl.BlockSpec((ncb, 1, rows * step, LANES), lambda h, j: (h, j, 0, 0)),
        out_shape=jax.ShapeDtypeStruct((NB, DFT_N1 // step, rows * step, LANES), F32),
        compiler_params=_cparams(("parallel", "arbitrary")),
        name="dft_filter_a",
    )(m1f, kern)


def _dft_filt_kernel(a_ref, m_ref, o_ref):
    rows = m_ref.shape[1]
    a = jnp.concatenate([a_ref[c, :, 0].reshape(rows, LANES) for c in range(a_ref.shape[0])], axis=1)
    o_ref[0] = _dot3(m_ref[0], a)


def _dft_filt(a, m2):
    NB, G, N2, W, _ = a.shape
    R = m2.shape[1]
    return pl.pallas_call(
        _dft_filt_kernel,
        grid=(N2,),
        in_specs=[pl.BlockSpec((NB, G, 1, W, LANES), lambda k: (0, 0, k, 0, 0)),
                  pl.BlockSpec((1, R, R), lambda k: (k, 0, 0))],
        out_specs=pl.BlockSpec((1, R, NB * LANES), lambda k: (k, 0, 0)),
        out_shape=jax.ShapeDtypeStruct((N2, R, NB * LANES), F32),
        compiler_params=_cparams(("parallel",)),
        name="dft_filter_mid",
    )(a, m2)


def _packed_rows(ref, lead, start, size, stride):
    cols = [jnp.concatenate([ref[(c,) + l + (pl.ds(start, size, stride=stride), slice(None))] for l in lead], axis=0)
            for c in range(ref.shape[0])]
    return pltpu.bitcast(jnp.concatenate(cols, axis=1), BF16)


def _dft_a_pk_kernel(m_ref, z_ref, o_ref, *, tm, nh, npairs):
    j0 = pl.program_id(2) * tm
    rows = m_ref.shape[0] // 2
    for j in range(tm):
        rhs = _packed_rows(z_ref, [(0,), (1,)], j0 + j, nh, npairs)
        res = jnp.dot(m_ref[...], rhs, preferred_element_type=F32)
        pk = pltpu.bitcast(res.astype(BF16), U32)
        for c in range(o_ref.shape[0]):
            o_ref[c, 0, 0, pl.ds(j, rows, stride=tm), :] = pk[:, c * LANES:(c + 1) * LANES]


def _dft_a_pk(m1x, z, slot):
    _, NC, B, L2, _ = z.shape
    npairs = DFT_N1 // 2
    nh = L2 // npairs
    tm, ncb = DFT_PAIRS, DFT_LANE_BLOCKS
    rows = m1x.shape[0] // 2
    P = B // 2
    return pl.pallas_call(
        functools.partial(_dft_a_pk_kernel, tm=tm, nh=nh, npairs=npairs),
        grid=(P, NC // ncb, npairs // tm),
        in_specs=[pl.BlockSpec(m1x.shape, lambda p, h, j: (0, 0)),
                  pl.BlockSpec((None, ncb, 2, L2, LANES), lambda p, h, j: (slot, h, p, 0, 0))],
        out_specs=pl.BlockSpec((ncb, 1, 1, rows * tm, LANES), lambda p, h, j: (h, p, j, 0, 0)),
        out_shape=jax.ShapeDtypeStruct((NC, P, npairs // tm, rows * tm, LANES), U32),
        compiler_params=_cparams(("parallel", "parallel", "arbitrary")),
        name="dft_a",
    )(m1x, z)


def _dft_mid_kernel(a_ref, m_ref, mt_ref, kf_ref, o_ref):
    m = m_ref[0]
    mt = mt_ref[0]
    n1 = m.shape[0] // 2
    kr = kf_ref[0, :n1]
    ki = kf_ref[0, n1:]
    nc, npair = a_ref.shape[0], a_ref.shape[1]
    blk = a_ref.shape[2], a_ref.shape[4], LANES
    for p in range(npair):
        cols = [a_ref[c, p, :, 0].reshape(n1, LANES) for c in range(nc)]
        a = pltpu.bitcast(jnp.concatenate(cols, axis=1), BF16)
        x = jnp.dot(m, a, preferred_element_type=F32)
        xr = x[:n1]
        xi = x[n1:]
        y = jnp.concatenate([xr * kr - xi * ki, xr * ki + xi * kr], axis=0).astype(BF16)
        pk = pltpu.bitcast(jnp.dot(mt, y, preferred_element_type=F32).astype(BF16), U32)
        for c in range(nc):
            o_ref[c, p, :, 0] = pk[:, c * LANES:(c + 1) * LANES].reshape(blk)


def _dft_mid(a, m2, m2t, kf, order):
    NC, P, G, N2, W, _ = a.shape
    R = m2.shape[1]
    C = NC * LANES
    blk = pl.BlockSpec((NC, P, G, 1, W, LANES), lambda k: (0, 0, 0, k, 0, 0))
    return pl.pallas_call(
        _dft_mid_kernel,
        grid=(N2,),
        in_specs=[blk,
                  pl.BlockSpec((1, R, R), lambda k: (k, 0, 0)),
                  pl.BlockSpec((1, R, R), lambda k: (k, 0, 0)),
                  pl.BlockSpec((1, R, C), lambda k: (k, 0, order))],
        out_specs=blk,
        out_shape=jax.ShapeDtypeStruct(a.shape, U32),
        compiler_params=_cparams(("parallel",)),
        name="dft_mid",
    )(a, m2, m2t, kf)


def _dft_c_pk_kernel(m_ref, b_ref, z_ref, g_ref, bias_ref, o_ref, *, tm, nh, npairs):
    j0 = pl.program_id(2) * tm
    rows = b_ref.shape[3] // tm
    bias = bias_ref[...]
    both = [(0,), (1,)]
    for j in range(tm):
        rhs = _packed_rows(b_ref, [(0, 0)], j, rows, tm)
        y = jnp.dot(m_ref[...], rhs, preferred_element_type=F32)
        z = _packed_rows(z_ref, both, j0 + j, nh, npairs).astype(F32)
        g = _packed_rows(g_ref, both, j0 + j, nh, npairs).astype(F32)
        pk = pltpu.bitcast((g * (y + z * bias)).astype(BF16), U32)
        for c in range(o_ref.shape[0]):
            for b in range(2):
                o_ref[c, b, pl.ds(j0 + j, nh, stride=npairs), :] = pk[b * nh:(b + 1) * nh, c * LANES:(c + 1) * LANES]


def _dft_c_pk(m3x, b, z, zslot, g, gslot, bias):
    NC, P, G, W, _ = b.shape
    L2 = z.shape[3]
    npairs = DFT_N1 // 2
    nh = L2 // npairs
    tm, ncb = DFT_PAIRS, DFT_LANE_BLOCKS
    seq = lambda slot: pl.BlockSpec((None, ncb, 2, L2, LANES), lambda p, h, j: (slot, h, p, 0, 0))
    return pl.pallas_call(
        functools.partial(_dft_c_pk_kernel, tm=tm, nh=nh, npairs=npairs),
        grid=(P, NC // ncb, G),
        in_specs=[pl.BlockSpec(m3x.shape, lambda p, h, j: (0, 0)),
                  pl.BlockSpec((ncb, 1, 1, W, LANES), lambda p, h, j: (h, p, j, 0, 0)),
                  seq(zslot), seq(gslot),
                  pl.BlockSpec((1, ncb * LANES), lambda p, h, j: (0, h))],
        out_specs=seq(0),
        out_shape=jax.ShapeDtypeStruct((1, NC, 2 * P, L2, LANES), U32),
        compiler_params=_cparams(("parallel", "parallel", "arbitrary")),
        name="dft_c",
    )(m3x, b, z, g, bias)


def _mid_row_order(step, per):
    N1 = DFT_N1
    idx = []
    for mt in range(N1 // per // step):
        for ri in range(2):
            for jj in range(step):
                for sub in range(per):
                    idx.append(ri * N1 + per * (mt * step + jj) + sub)
    return np.asarray(idx, np.int32)


def _hyena(u3, kern, hy_bias, mats):
    _, NC, B, L2, _ = u3.shape
    C = NC * LANES
    N1 = DFT_N1
    N2 = 4 * L2 // N1
    af = _dft_filter_a(mats["m1f"], kern)
    NB, G, W, _ = af.shape
    kf = _dft_filt(af.reshape(NB, G, N2, W // N2, LANES), mats["m2f"])
    bias = hy_bias.astype(F32)

    def long_conv(z, zslot, gslot, order):
        a = _dft_a_pk(mats["m1x"], z, zslot)
        NCa, P, G, W, _ = a.shape
        b = _dft_mid(a.reshape(NCa, P, G, N2, W // N2, LANES), mats["m2p"], mats["m2tp"], kf, order)
        return _dft_c_pk(mats["m3x"], b.reshape(a.shape), z, zslot, u3, gslot, bias[order][None, :])

    z = long_conv(u3, 2, 0, 0)
    return long_conv(z, 0, 1, 1)


SHIFT_LIMIT = 50.0


def _attn_kernel(lam_ref, q_ref, k_ref, v_ref, g_ref, o_ref, kmax_ref, *, post):
    k = k_ref[0]
    v = v_ref[0]

    @pl.when(pl.program_id(2) == 0)
    def _():
        kmax_ref[...] = jnp.max(jnp.abs(k.astype(F32)), axis=0, keepdims=True)

    q = q_ref[0]
    lane = lax.broadcasted_iota(jnp.int32, q.shape, 1)
    lo = lane < HEAD_DIM
    zero = jnp.zeros_like(q)
    reach = jnp.abs(q.astype(F32)) * kmax_ref[...]
    shifts = [jnp.sum(jnp.where(lo, reach, 0.0), axis=-1, keepdims=True),
              jnp.sum(jnp.where(lo, 0.0, reach), axis=-1, keepdims=True)]
    worst = jnp.max(jnp.maximum(shifts[0], shifts[1]))

    def run(row_max):
        outs = []
        for c in range(2):
            qc = jnp.where(lo, q, zero) if c == 0 else jnp.where(lo, zero, q)
            s = lax.dot_general(qc, k, (((1,), (1,)), ((), ())), preferred_element_type=F32)
            m = jnp.max(s, axis=-1, keepdims=True) if row_max else shifts[c]
            p = jnp.exp2(s - m)
            l = jnp.sum(p, axis=-1, keepdims=True)
            outs.append(jnp.dot(p.astype(BF16), v, preferred_element_type=F32) / l)
        a = outs[0] - lam_ref[0] * outs[1]
        ms = jnp.mean(a * a, axis=-1, keepdims=True)
        o_ref[0] = (a * lax.rsqrt(ms + RMS_EPS) * g_ref[...] * post).astype(o_ref.dtype)

    @pl.when(worst <= SHIFT_LIMIT)
    def _():
        run(False)

    @pl.when(jnp.logical_not(worst <= SHIFT_LIMIT))
    def _():
        run(True)


def _attention(qn, kn, proj3, vcol, lam, subln_g, post):
    B, L, W = qn.shape
    tq = min(512, L)
    return pl.pallas_call(
        functools.partial(_attn_kernel, post=post),
        grid=(B, W // V_DIM, L // tq),
        in_specs=[pl.BlockSpec(memory_space=pltpu.SMEM),
                  pl.BlockSpec((1, tq, V_DIM), lambda b, h, i: (b, i, h)),
                  pl.BlockSpec((1, L, V_DIM), lambda b, h, i: (b, 0, h)),
                  pl.BlockSpec((1, L, V_DIM), lambda b, h, i: (b, 0, vcol + h)),
                  pl.BlockSpec((1, V_DIM), lambda b, h, i: (0, 0))],
        out_specs=pl.BlockSpec((1, tq, V_DIM), lambda b, h, i: (b, i, h)),
        out_shape=jax.ShapeDtypeStruct((B, L, W), BF16),
        scratch_shapes=[pltpu.VMEM((1, V_DIM), F32)],
        compiler_params=_cparams(("parallel", "parallel", "arbitrary")),
        name="diff_attention",
    )(lam.reshape(1).astype(F32), qn, kn, proj3, subln_g.astype(F32).reshape(1, V_DIM))


def _merge_kernel(x_ref, yh_ref, ya_ref, gh_ref, ga_ref, who_ref, wao_ref, wout_ref, g_ref, *rest,
                  with_router):
    if with_router:
        wr_ref, xo_ref, hn_ref, lg_ref = rest
    else:
        xo_ref, hn_ref = rest
    yh = jnp.concatenate([pltpu.bitcast(yh_ref[c], BF16) for c in range(yh_ref.shape[0])], axis=1)
    th = jnp.dot(yh, who_ref[...], preferred_element_type=F32)
    ta = jnp.dot(ya_ref[...], wao_ref[...], preferred_element_type=F32)
    merged = (jax.nn.sigmoid(gh_ref[...].astype(F32)) * th
              + jax.nn.sigmoid(ga_ref[...].astype(F32)) * ta)
    xn = x_ref[...] + jnp.dot(merged.astype(BF16), wout_ref[...], preferred_element_type=F32)
    xo_ref[...] = xn
    ms = jnp.mean(xn * xn, axis=-1, keepdims=True)
    h = xn * lax.rsqrt(ms + RMS_EPS) * g_ref[...]
    hn_ref[...] = h.astype(hn_ref.dtype)
    if with_router:
        lane = lax.broadcasted_iota(jnp.int32, lg_ref.shape, 1)
        lg = jnp.zeros(lg_ref.shape, F32)
        for e in range(wr_ref.shape[0]):
            lg = jnp.where(lane == e, jnp.sum(h * wr_ref[e:e + 1, :], axis=-1, keepdims=True), lg)
        lg_ref[...] = lg


def _merge(x2, yh, ya, proj, gcol, who, wao, wout, g_ffn, w_router):
    T, D = x2.shape
    W = ya.shape[1]
    tm = min(512, T)
    with_router = w_router is not None
    row = lambda w: pl.BlockSpec((tm, w), lambda i: (i, 0))
    full = lambda a: pl.BlockSpec(a.shape, lambda i: (0, 0))
    in_specs = [row(D), pl.BlockSpec((yh.shape[0], tm // 2, LANES), lambda i: (0, i, 0)), row(W),
                pl.BlockSpec((tm, D), lambda i: (i, gcol)),
                pl.BlockSpec((tm, D), lambda i: (i, gcol + 1)),
                full(who), full(wao), full(wout), pl.BlockSpec((1, D), lambda i: (0, 0))]
    args = [x2, yh, ya, proj, proj, who, wao, wout, g_ffn.astype(F32).reshape(1, D)]
    out_specs = [row(D), row(D)]
    out_shape = [jax.ShapeDtypeStruct((T, D), F32),
                 jax.ShapeDtypeStruct((T, D), F32 if with_router else BF16)]
    if with_router:
        wr = w_router.astype(F32).T
        in_specs.append(full(wr))
        args.append(wr)
        out_specs.append(row(LANES))
        out_shape.append(jax.ShapeDtypeStruct((T, LANES), F32))
    return pl.pallas_call(
        functools.partial(_merge_kernel, with_router=with_router),
        grid=(T // tm,),
        in_specs=in_specs,
        out_specs=out_specs,
        out_shape=out_shape,
        compiler_params=_cparams(("parallel",)),
        name="merge_outproj",
    )(*args)


def _swiglu(h, wg_ref, wu_ref, wd_ref, chunks):
    F = wg_ref.shape[-1]
    fc = F // chunks
    acc = None
    for c in range(chunks):
        sl = slice(c * fc, (c + 1) * fc)
        g = jnp.dot(h, wg_ref[:, sl], preferred_element_type=F32)
        u = jnp.dot(h, wu_ref[:, sl], preferred_element_type=F32)
        a = (g * jax.nn.sigmoid(g) * u).astype(BF16)
        d = jnp.dot(a, wd_ref[sl, :], preferred_element_type=F32)
        acc = d if acc is None else acc + d
    return acc


def _ffn_kernel(h_ref, x_ref, wg_ref, wu_ref, wd_ref, o_ref, *, chunks):
    o_ref[...] = x_ref[...] + _swiglu(h_ref[...], wg_ref, wu_ref, wd_ref, chunks)


def _ff_chunks(F):
    return F // (2 * LANES) if F % (2 * LANES) == 0 else 1


def _dense_ffn(hn, x2, wg, wu, wd):
    T, D = x2.shape
    F = wg.shape[1]
    tm = min(512, T)
    row = pl.BlockSpec((tm, D), lambda i: (i, 0))
    return pl.pallas_call(
        functools.partial(_ffn_kernel, chunks=_ff_chunks(F)),
        grid=(T // tm,),
        in_specs=[row, row,
                  pl.BlockSpec((D, F), lambda i: (0, 0)),
                  pl.BlockSpec((D, F), lambda i: (0, 0)),
                  pl.BlockSpec((F, D), lambda i: (0, 0))],
        out_specs=row,
        out_shape=jax.ShapeDtypeStruct((T, D), F32),
        compiler_params=_cparams(("parallel",)),
        name="dense_ffn",
    )(hn, x2, wg, wu, wd)


def _row_copy(src_hbm, row, dst, r, sem):
    return pltpu.make_async_copy(src_hbm.at[pl.ds(row, 1)], dst.at[pl.ds(r, 1)], sem)


def _rows_done(src_hbm, dst, sem):
    return pltpu.make_async_copy(src_hbm.at[pl.ds(0, dst.shape[0])], dst, sem)


def _expert_kernel(be_ref, tok_ref, hn_hbm, wg_ref, wu_ref, wd_ref, o_ref, xbuf, sem, *, chunks):
    i = pl.program_id(0)
    last = pl.num_programs(0) - 1
    rows = xbuf.shape[1]
    slot = i % 2
    nxt = 1 - slot

    @pl.when(i == 0)
    def _():
        def issue(r, carry):
            _row_copy(hn_hbm, tok_ref[r], xbuf.at[0], r, sem.at[0]).start()
            return carry

        lax.fori_loop(0, rows, issue, 0, unroll=8)

    _rows_done(hn_hbm, xbuf.at[slot], sem.at[slot]).wait()
    base = jnp.minimum(i + 1, last) * rows
    for r in range(rows):
        _row_copy(hn_hbm, tok_ref[base + r], xbuf.at[nxt], r, sem.at[nxt]).start()
    o_ref[...] = _swiglu(xbuf[slot].astype(BF16), wg_ref.at[0], wu_ref.at[0], wd_ref.at[0], chunks)

    @pl.when(i == last)
    def _():
        _rows_done(hn_hbm, xbuf.at[nxt], sem.at[nxt]).wait()


def _experts(block_e, tok_of_slot, hn, wg, wu, wd, layer):
    T, D = hn.shape
    F = wg.shape[3]
    P = tok_of_slot.shape[0]
    rows = MOE_ROWS
    wspec = lambda shape: pl.BlockSpec((None, 1) + shape, lambda i, be, tok: (layer, be[i], 0, 0))
    return pl.pallas_call(
        functools.partial(_expert_kernel, chunks=_ff_chunks(F)),
        grid_spec=pltpu.PrefetchScalarGridSpec(
            num_scalar_prefetch=2,
            grid=(P // rows,),
            in_specs=[pl.BlockSpec(memory_space=pl.ANY),
                      wspec((D, F)), wspec((D, F)), wspec((F, D))],
            out_specs=pl.BlockSpec((rows, D), lambda i, be, tok: (i, 0)),
            scratch_shapes=[pltpu.VMEM((2, rows, D), F32), pltpu.SemaphoreType.DMA((2,))]),
        out_shape=jax.ShapeDtypeStruct((P, D), F32),
        compiler_params=_cparams(("arbitrary",)),
        name="moe_experts",
    )(block_e, tok_of_slot, hn, wg, wu, wd)


def _combine_kernel(dest_ref, yp_hbm, x_ref, w_ref, o_ref, buf0, buf1, sem):
    i = pl.program_id(0)
    rows = buf0.shape[1]

    def gather(tile, slot):
        base = tile * rows * TOP_K

        for r in range(rows):
            _row_copy(yp_hbm, dest_ref[base + TOP_K * r], buf0.at[slot], r, sem.at[slot]).start()
            _row_copy(yp_hbm, dest_ref[base + TOP_K * r + 1], buf1.at[slot], r, sem.at[slot]).start()

    @pl.when(i == 0)
    def _():
        gather(0, 0)

    @pl.when(i + 1 < pl.num_programs(0))
    def _():
        gather(i + 1, (i + 1) % 2)

    slot = i % 2
    _rows_done(yp_hbm, buf0.at[slot], sem.at[slot]).wait()
    _rows_done(yp_hbm, buf1.at[slot], sem.at[slot]).wait()
    w = w_ref[...]
    o_ref[...] = x_ref[...] + (buf0[slot] * w[:, 0:1] + buf1[slot] * w[:, 1:2])


def _combine(dest, yp, x2, top_w):
    T, D = x2.shape
    tm = min(256, T)
    return pl.pallas_call(
        _combine_kernel,
        grid_spec=pltpu.PrefetchScalarGridSpec(
            num_scalar_prefetch=1,
            grid=(T // tm,),
            in_specs=[pl.BlockSpec(memory_space=pl.ANY),
                      pl.BlockSpec((tm, D), lambda i, d: (i, 0)),
                      pl.BlockSpec((tm, TOP_K), lambda i, d: (i, 0))],
            out_specs=pl.BlockSpec((tm, D), lambda i, d: (i, 0)),
            scratch_shapes=[pltpu.VMEM((2, tm, D), F32), pltpu.VMEM((2, tm, D), F32),
                            pltpu.SemaphoreType.DMA((2,))]),
        out_shape=jax.ShapeDtypeStruct((T, D), F32),
        compiler_params=_cparams(("arbitrary",)),
        name="moe_combine",
    )(dest, yp, x2, top_w)


def _route(logits, n_experts, rows):
    T = logits.shape[0]
    top_val, top_idx = lax.top_k(logits[:, :n_experts], TOP_K)
    top_w = jax.nn.softmax(top_val, axis=-1)
    A = T * TOP_K
    e_flat = top_idx.reshape(A).astype(jnp.int32)
    onehot = (e_flat[:, None] == jnp.arange(n_experts, dtype=jnp.int32)[None, :]).astype(jnp.int32)
    csum = jnp.cumsum(onehot, axis=0)
    counts = csum[-1]
    rank = jnp.sum((csum - onehot) * onehot, axis=1)
    padded = (counts + rows - 1) // rows * rows
    pad_end = jnp.cumsum(padded)
    pad_start = pad_end - padded
    dest = pad_start[e_flat] + rank
    n_blocks = -(-A // rows) + n_experts
    P = n_blocks * rows
    order = jnp.argsort(e_flat)
    starts = jnp.cumsum(counts) - counts
    slot = jnp.arange(P, dtype=jnp.int32)
    slot_e = jnp.minimum(jnp.sum((slot[:, None] >= pad_end[None, :]).astype(jnp.int32), axis=1),
                         n_experts - 1)
    within = slot - pad_start[slot_e]
    valid = within < counts[slot_e]
    src = jnp.clip(starts[slot_e] + within, 0, A - 1)
    tok_of_slot = jnp.where(valid, order[src] // TOP_K, 0).astype(jnp.int32)
    block_e = slot_e[::rows]
    return top_w.astype(F32), dest.astype(jnp.int32), tok_of_slot, block_e


def _pad_ff(w, axis):
    F = w.shape[axis]
    pad = [(0, 0)] * w.ndim
    pad[axis] = (0, _round_up(F, 2 * LANES) - F)
    return jnp.pad(w.astype(BF16), pad)


def kernel(x, norm_mix, w_in, conv_w, w_f1, b_f1, freq1, w_f2, b_f2, freq2, w_f3, hy_bias, q_norm, k_norm, lam_q1, lam_k1, lam_q2, lam_k2, subln_g, w_hy_o, w_at_o, w_out, norm_ffn, w_dense_gate, w_dense_up, w_dense_down, w_router, w_moe_gate, w_moe_up, w_moe_down):
    B, L, D = x.shape
    T = B * L
    depth = w_in.shape[0]
    hw = w_hy_o.shape[1]
    qk_w = N_HEADS * 2 * HEAD_DIM
    at_w = N_HEADS * V_DIM
    mats = _dft_matrices(L)
    moe_w = (_pad_ff(w_moe_gate, 3), _pad_ff(w_moe_up, 3), _pad_ff(w_moe_down, 2))

    x2 = x.reshape(T, D)
    for layer in range(depth):
        lambda_init = 0.8 - 0.6 * math.exp(-0.3 * layer)
        u3, qn, kn, v, gates = _inproj(x2, L, norm_mix[layer].astype(F32), w_in[layer].astype(BF16),
                                       conv_w[layer].astype(F32), q_norm[layer], k_norm[layer], hw, qk_w, at_w)
        kern = _hyena_time_filters(L, w_f1[layer], b_f1[layer], freq1[layer], w_f2[layer], b_f2[layer],
                                   freq2[layer], w_f3[layer], hw)
        y_h = _hyena(u3.reshape(3, hw // LANES, B, L // 2, LANES), kern, hy_bias[layer], mats)
        lam = (jnp.exp(jnp.sum(lam_q1[layer].astype(F32) * lam_k1[layer].astype(F32)))
               - jnp.exp(jnp.sum(lam_q2[layer].astype(F32) * lam_k2[layer].astype(F32))) + lambda_init)
        y_a = _attention(qn.reshape(B, L, qk_w), kn.reshape(B, L, qk_w), v.reshape(B, L, at_w), 0, lam,
                         subln_g[layer], 1.0 - lambda_init)
        i = layer // 2
        moe = layer % 2 == 1
        outs = _merge(x2, y_h.reshape(hw // LANES, T // 2, LANES), y_a.reshape(T, at_w), gates, 0,
                      w_hy_o[layer].astype(BF16), w_at_o[layer].astype(BF16), w_out[layer].astype(BF16),
                      norm_ffn[layer], w_router[i] if moe else None)
        if not moe:
            x2, hn = outs
            x2 = _dense_ffn(hn, x2, _pad_ff(w_dense_gate[i], 1), _pad_ff(w_dense_up[i], 1),
                            _pad_ff(w_dense_down[i], 0))
        else:
            x2, hn, logits = outs
            n_experts = w_router.shape[2]
            top_w, dest, tok_of_slot, block_e = _route(logits, n_experts, MOE_ROWS)
            yp = _experts(block_e, tok_of_slot, hn, *moe_w, i)
            x2 = _combine(dest, yp, x2, top_w)
    return x2.reshape(B, L, D)
```

```python
import functools
import math

import numpy as np
import jax
import jax.numpy as jnp
from jax import lax
from jax.experimental import pallas as pl
from jax.experimental.pallas import tpu as pltpu

F32 = jnp.float32
BF16 = jnp.bfloat16
U32 = jnp.uint32

N_HEADS = 4
HEAD_DIM = 64
V_DIM = 2 * HEAD_DIM
ROPE_THETA = 10000.0
FILTER_BANDS = 16
HYENA_TARGET = 1e-2
FAST_DECAY_PCT = 0.3
SLOW_DECAY_PCT = 1.5
TOP_K = 2
RMS_EPS = 1e-6

LANES = 128
BF16_SUBLANES = 16
VMEM_LIMIT = 56 * 1024 * 1024

DFT_N1 = 128
DFT_PAIRS = 16
DFT_FILTER_STEP = 16
DFT_MID_K2 = 2
DFT_LANE_BLOCKS = 2
MOE_ROWS = 512


def _cparams(sem):
    return pltpu.CompilerParams(dimension_semantics=sem, vmem_limit_bytes=VMEM_LIMIT)


def _inproj_kernel(x_ref, xp_ref, xn_ref, g_ref, w_ref, cw_ref, cos_ref, sin_ref, qg_ref, kg_ref, grp_ref,
                   u_ref, q_ref, k_ref, v_ref, gate_ref, *, hw, qk_w, tiles_per_seq):
    pos = pl.program_id(0) % tiles_per_seq
    tm = x_ref.shape[0]
    halo = xp_ref.shape[0]

    def normed(x):
        ms = jnp.mean(x * x, axis=-1, keepdims=True)
        return (x * lax.rsqrt(ms + RMS_EPS) * g_ref[...]).astype(BF16)

    hn = normed(x_ref[...])
    hext = jnp.concatenate([normed(xp_ref[...]), hn, normed(xn_ref[...])], axis=0)
    row = lax.broadcasted_iota(jnp.int32, (tm + 2 * halo, hw), 0)
    outside = jnp.logical_or(jnp.logical_and(row < halo, pos == 0),
                             jnp.logical_and(row >= tm + halo, pos == tiles_per_seq - 1))
    hy_all = jnp.dot(hext, w_ref[:, :3 * hw], preferred_element_type=F32)
    rest = jnp.dot(hn, w_ref[:, 3 * hw:], preferred_element_type=F32)

    nc = u_ref.shape[1]
    for s in range(u_ref.shape[0]):
        cols = slice(s * hw, (s + 1) * hw)
        hy = jnp.where(outside, 0.0, hy_all[:, cols])
        up = pltpu.roll(hy, 1, 0)[halo:halo + tm]
        dn = pltpu.roll(hy, tm + 2 * halo - 1, 0)[halo:halo + tm]
        cw = cw_ref[:, cols]
        res = (up * cw[0:1] + hy[halo:halo + tm] * cw[1:2] + dn * cw[2:3]).astype(BF16)
        pk = pltpu.bitcast(res, U32)
        for c in range(nc):
            u_ref[s, c] = pk[:, c * LANES:(c + 1) * LANES]

    cos = cos_ref[...]
    sin = sin_ref[...]
    lane = lax.broadcasted_iota(jnp.int32, cos.shape, 1)
    first = (lane % HEAD_DIM) < (HEAD_DIM // 2)
    grp = grp_ref[...]
    base = 0
    for o_ref, gain_ref in ((q_ref, qg_ref), (k_ref, kg_ref)):
        for h in range(qk_w // LANES):
            x = rest[:, base + h * LANES:base + (h + 1) * LANES]
            sq_hi, sq_lo = _split_bf16(x * x)
            ss = (jnp.dot(sq_hi, grp, preferred_element_type=F32)
                  + jnp.dot(sq_lo, grp, preferred_element_type=F32))
            xn = x * lax.rsqrt(ss * (1.0 / HEAD_DIM) + RMS_EPS) * gain_ref[...]
            partner = jnp.where(first, pltpu.roll(xn, LANES - HEAD_DIM // 2, 1),
                                pltpu.roll(xn, HEAD_DIM // 2, 1))
            o_ref[:, h * LANES:(h + 1) * LANES] = (xn * cos + partner * sin).astype(o_ref.dtype)
        base += qk_w
    v_ref[...] = rest[:, base:base + v_ref.shape[1]].astype(v_ref.dtype)
    gate_ref[...] = rest[:, base + v_ref.shape[1]:].astype(gate_ref.dtype)


def _inproj(x2, L, g, w, conv_w, q_norm, k_norm, hw, qk_w, at_w):
    T, D = x2.shape
    tm = min(512, L)
    halo = BF16_SUBLANES
    hb = tm // halo
    nhb = T // halo
    nc = hw // LANES
    gw = w.shape[1] - 3 * hw - 2 * qk_w - at_w
    cos_t, sin_t = _rope_tables(L)
    qg = jnp.tile(q_norm.astype(F32), LANES // HEAD_DIM)[None, :] * F32(HEAD_DIM ** -0.5 * math.log2(math.e))
    kg = jnp.tile(k_norm.astype(F32), LANES // HEAD_DIM)[None, :]
    lanes = np.arange(LANES)
    grp = jnp.asarray(lanes[:, None] // HEAD_DIM == lanes[None, :] // HEAD_DIM, BF16)
    tps = L // tm
    const = lambda a: pl.BlockSpec(a.shape, lambda i: (0,) * a.ndim)
    rows = lambda width: pl.BlockSpec((tm, width), lambda i: (i, 0))
    tab = pl.BlockSpec((tm, LANES), lambda i: (i % tps, 0))
    gr = g.reshape(1, D)
    return pl.pallas_call(
        functools.partial(_inproj_kernel, hw=hw, qk_w=qk_w, tiles_per_seq=tps),
        grid=(T // tm,),
        in_specs=[rows(D),
                  pl.BlockSpec((halo, D), lambda i: (jnp.maximum(i * hb - 1, 0), 0)),
                  pl.BlockSpec((halo, D), lambda i: (jnp.minimum((i + 1) * hb, nhb - 1), 0)),
                  const(gr), const(w), const(conv_w), tab, tab, const(qg), const(kg), const(grp)],
        out_specs=[pl.BlockSpec((3, nc, tm // 2, LANES), lambda i: (0, 0, i, 0)),
                   rows(qk_w), rows(qk_w), rows(at_w), rows(gw)],
        out_shape=[jax.ShapeDtypeStruct((3, nc, T // 2, LANES), U32),
                   jax.ShapeDtypeStruct((T, qk_w), BF16), jax.ShapeDtypeStruct((T, qk_w), BF16),
                   jax.ShapeDtypeStruct((T, at_w), BF16), jax.ShapeDtypeStruct((T, gw), BF16)],
        compiler_params=_cparams(("parallel",)),
        name="inproj",
    )(x2, x2, x2, gr, w, conv_w, cos_t, sin_t, qg, kg, grp)


def _rope_tables(L):
    half = HEAD_DIM // 2
    inv = 1.0 / (ROPE_THETA ** (np.arange(0, HEAD_DIM, 2, dtype=np.float64) / HEAD_DIM))
    ang = np.arange(L, dtype=np.float64)[:, None] * inv[None, :]
    cos, sin = np.cos(ang), np.sin(ang)
    cos_t = np.tile(cos, (1, LANES // half))
    sin_t = np.tile(np.concatenate([-sin, sin], axis=1), (1, LANES // HEAD_DIM))
    return jnp.asarray(cos_t, F32), jnp.asarray(sin_t, F32)


def _split_bf16(x):
    hi = x.astype(BF16)
    return hi, (x - hi.astype(F32)).astype(BF16)


def _dot3(a, b):
    ah, al = _split_bf16(a)
    bh, bl = _split_bf16(b)
    d = lambda x, y: jnp.dot(x, y, preferred_element_type=F32)
    return d(ah, bh) + (d(ah, bl) + d(al, bh))


def _filter_kernel(zz_ref, w1_ref, b1_ref, f1_ref, w2_ref, b2_ref, f2_ref, w3_ref, dl_ref, o_ref):
    zz = zz_ref[...]
    h = _dot3(zz, w1_ref[...])
    h = jnp.sin(f1_ref[...] * (h + b1_ref[...]))
    h = _dot3(h, w2_ref[...])
    h = jnp.sin(f2_ref[...] * (h + b2_ref[...]))
    hf = _dot3(h, w3_ref[0])
    t = zz[:, 0:1]
    mask = zz[:, _MASK_COL:_MASK_COL + 1]
    kern = hf * (jnp.exp(-t * dl_ref[...]) * mask)
    for c in range(o_ref.shape[0]):
        o_ref[c] = kern[:, c * LANES:(c + 1) * LANES]


_MASK_COL = 2 * FILTER_BANDS + 1


def _filter_positions(L):
    bands = FILTER_BANDS
    t = np.linspace(0.0, 1.0, L, dtype=np.float64)[:, None]
    w = 2.0 * math.pi * np.arange(L, dtype=np.float64)[:, None] / L
    f = np.linspace(1e-4, bands - 1, bands, dtype=np.float64)[None, :]
    z = np.concatenate([t, np.cos(f * w), -np.sin(f * w)], axis=-1)
    zz = np.zeros((2 * L, LANES), np.float64)
    zz[:L, :z.shape[1]] = z
    zz[L + 1:, :z.shape[1]] = z[1:][::-1]
    zz[:, _MASK_COL] = 1.0
    zz[L, :] = 0.0
    return zz.astype(np.float32)


def _hyena_time_filters(L, w_f1, b_f1, freq1, w_f2, b_f2, freq2, w_f3, hw):
    emb, hid = w_f1.shape
    zz = jnp.asarray(_filter_positions(L))
    w1 = jnp.zeros((LANES, hid), F32).at[:emb].set(w_f1)
    w3 = w_f3.reshape(hid, 2, 2, hw).transpose(2, 0, 1, 3).reshape(2, hid, 2 * hw)
    max_decay = math.log(HYENA_TARGET) / FAST_DECAY_PCT
    min_decay = math.log(HYENA_TARGET) / SLOW_DECAY_PCT
    deltas = np.abs(np.linspace(min_decay, max_decay, hw, dtype=np.float64))
    dl = jnp.asarray(np.tile(deltas, 2)[None, :], F32)
    tr = min(512, L)
    per_dir = L // tr
    row = lambda a: a.reshape(1, -1)
    return pl.pallas_call(
        _filter_kernel,
        grid=(2 * L // tr,),
        in_specs=[pl.BlockSpec((tr, LANES), lambda i: (i, 0)),
                  pl.BlockSpec((LANES, hid), lambda i: (0, 0)),
                  pl.BlockSpec((1, hid), lambda i: (0, 0)),
                  pl.BlockSpec((1, hid), lambda i: (0, 0)),
                  pl.BlockSpec((hid, hid), lambda i: (0, 0)),
                  pl.BlockSpec((1, hid), lambda i: (0, 0)),
                  pl.BlockSpec((1, hid), lambda i: (0, 0)),
                  pl.BlockSpec((1, hid, 2 * hw), lambda i: (i // per_dir, 0, 0)),
                  pl.BlockSpec((1, 2 * hw), lambda i: (0, 0))],
        out_specs=pl.BlockSpec((2 * hw // LANES, tr, LANES), lambda i: (0, i, 0)),
        out_shape=jax.ShapeDtypeStruct((2 * hw // LANES, 2 * L, LANES), F32),
        compiler_params=_cparams(("parallel",)),
        name="hyena_filter",
    )(zz, w1, row(b_f1), row(freq1), w_f2, row(b_f2), row(freq2), w3, dl)


def _dft_matrices(L):
    N = 2 * L
    N1 = DFT_N1
    N2 = N // N1
    h = N2 // 2
    ang = 2.0 * np.pi * np.outer(np.arange(N2), np.arange(N2)) / N2
    c, s = np.cos(ang), np.sin(ang)
    m1 = np.zeros((N2, 2, N2))
    m1[:, 0, :h], m1[:, 0, h:] = c[:, :h], s[:, :h]
    m1[:, 1, :h], m1[:, 1, h:] = -s[:, :h], c[:, :h]
    m1 = m1.reshape(2 * N2, N2)
    m1f = np.stack([c, -s], axis=1).reshape(2 * N2, N2)
    m3 = np.zeros((2, h, N2, 2))
    m3[0, :, :, 0], m3[0, :, :, 1] = c[:h], -s[:h]
    m3[1, :, :, 0], m3[1, :, :, 1] = s[:h], c[:h]
    m3 = (m3 / N).reshape(N2, 2 * N2)
    def mid(order, transposed):
        f = np.arange(2 * N1)[:, None]
        d = np.asarray(order)[None, :]
        a = 2.0 * np.pi * ((d % N1) * (f % N1) % N1) / N1
        same, up = (f // N1 == d // N1), (f // N1 < d // N1)
        pc = np.where(same, 1.0, 0.0)
        ps = np.where(same, 0.0, np.where(up, 1.0, -1.0))
        u = pc * np.cos(a) + ps * np.sin(a)
        v = ps * np.cos(a) - pc * np.sin(a)
        b = 2.0 * np.pi * (np.arange(N2)[:, None] * (d % N1) % N) / N
        if transposed:
            u, v = u.T, v.T
            cb, sb = jnp.asarray(np.cos(b), F32)[:, :, None], jnp.asarray(np.sin(b), F32)[:, :, None]
        else:
            cb, sb = jnp.asarray(np.cos(b), F32)[:, None, :], jnp.asarray(np.sin(b), F32)[:, None, :]
        return jnp.asarray(u, F32)[None] * cb + jnp.asarray(v, F32)[None] * sb

    packed = _mid_row_order(DFT_PAIRS, 2)
    eye2 = np.eye(2)
    return dict(m1x=jnp.asarray(np.kron(m1, eye2), BF16),
                m3x=jnp.asarray(np.kron(m3, eye2), BF16),
                m1f=jnp.asarray(m1f, F32),
                m2f=mid(_mid_row_order(DFT_FILTER_STEP, 1), False),
                m2p=mid(packed, False).astype(BF16),
                m2tp=mid(packed, True).astype(BF16))


def _dft_filter_a_kernel(m_ref, k_ref, o_ref, *, step, n2):
    j0 = pl.program_id(1) * step
    rows = m_ref.shape[0]
    for jj in range(step):
        rhs = jnp.concatenate([k_ref[c, pl.ds(j0 + jj, n2, stride=DFT_N1), :] for c in range(k_ref.shape[0])],
                              axis=1)
        res = _dot3(m_ref[...], rhs)
        for c in range(o_ref.shape[0]):
            o_ref[c, 0, pl.ds(jj, rows, stride=step), :] = res[:, c * LANES:(c + 1) * LANES]


def _dft_filter_a(m1f, kern):
    NB, N, _ = kern.shape
    n2 = N // DFT_N1
    step, ncb = DFT_FILTER_STEP, DFT_LANE_BLOCKS
    rows = m1f.shape[0]
    return pl.pallas_call(
        functools.partial(_dft_filter_a_kernel, step=step, n2=n2),
        grid=(NB // ncb, DFT_N1 // step),
        in_specs=[pl.BlockSpec(m1f.shape, lambda h, j: (0, 0)),
                  pl.BlockSpec((ncb, N, LANES), lambda h, j: (h, 0, 0))],
        out_specs=pl.BlockSpec((ncb, 1, rows * step, LANES), lambda h, j: (h, j, 0, 0)),
        out_shape=jax.ShapeDtypeStruct((NB, DFT_N1 // step, rows * step, LANES), F32),
        compiler_params=_cparams(("parallel", "arbitrary")),
        name="dft_filter_a",
    )(m1f, kern)


def _dft_filt_kernel(a_ref, m_ref, o_ref):
    rows = m_ref.shape[1]
    a = jnp.concatenate([a_ref[c, :, 0].reshape(rows, LANES) for c in range(a_ref.shape[0])], axis=1)
    o_ref[0] = _dot3(m_ref[0], a)


def _dft_filt(a, m2):
    NB, G, N2, W, _ = a.shape
    R = m2.shape[1]
    return pl.pallas_call(
        _dft_filt_kernel,
        grid=(N2,),
        in_specs=[pl.BlockSpec((NB, G, 1, W, LANES), lambda k: (0, 0, k, 0, 0)),
                  pl.BlockSpec((1, R, R), lambda k: (k, 0, 0))],
        out_specs=pl.BlockSpec((1, R, NB * LANES), lambda k: (k, 0, 0)),
        out_shape=jax.ShapeDtypeStruct((N2, R, NB * LANES), F32),
        compiler_params=_cparams(("parallel",)),
        name="dft_filter_mid",
    )(a, m2)


def _packed_rows(ref, lead, start, size, stride):
    cols = [jnp.concatenate([ref[(c,) + l + (pl.ds(start, size, stride=stride), slice(None))] for l in lead], axis=0)
            for c in range(ref.shape[0])]
    return pltpu.bitcast(jnp.concatenate(cols, axis=1), BF16)


def _dft_a_pk_kernel(m_ref, z_ref, o_ref, *, tm, nh, npairs):
    j0 = pl.program_id(2) * tm
    rows = m_ref.shape[0] // 2
    for j in range(tm):
        rhs = _packed_rows(z_ref, [(0,), (1,)], j0 + j, nh, npairs)
        res = jnp.dot(m_ref[...], rhs, preferred_element_type=F32)
        pk = pltpu.bitcast(res.astype(BF16), U32)
        for c in range(o_ref.shape[0]):
            o_ref[c, 0, 0, pl.ds(j, rows, stride=tm), :] = pk[:, c * LANES:(c + 1) * LANES]


def _dft_a_pk(m1x, z, slot):
    _, NC, B, L2, _ = z.shape
    npairs = DFT_N1 // 2
    nh = L2 // npairs
    tm, ncb = DFT_PAIRS, DFT_LANE_BLOCKS
    rows = m1x.shape[0] // 2
    P = B // 2
    return pl.pallas_call(
        functools.partial(_dft_a_pk_kernel, tm=tm, nh=nh, npairs=npairs),
        grid=(P, NC // ncb, npairs // tm),
        in_specs=[pl.BlockSpec(m1x.shape, lambda p, h, j: (0, 0)),
                  pl.BlockSpec((None, ncb, 2, L2, LANES), lambda p, h, j: (slot, h, p, 0, 0))],
        out_specs=pl.BlockSpec((ncb, 1, 1, rows * tm, LANES), lambda p, h, j: (h, p, j, 0, 0)),
        out_shape=jax.ShapeDtypeStruct((NC, P, npairs // tm, rows * tm, LANES), U32),
        compiler_params=_cparams(("parallel", "parallel", "arbitrary")),
        name="dft_a",
    )(m1x, z)


def _dft_mid_kernel(a_ref, m_ref, mt_ref, kf_ref, o_ref):
    n1 = m_ref.shape[1] // 2
    nc, npair = a_ref.shape[0], a_ref.shape[1]
    blk = a_ref.shape[2], a_ref.shape[4], LANES
    for kk in range(a_ref.shape[3]):
        m = m_ref[kk]
        mt = mt_ref[kk]
        kr = kf_ref[kk, :n1]
        ki = kf_ref[kk, n1:]
        for p in range(npair):
            cols = [a_ref[c, p, :, kk].reshape(n1, LANES) for c in range(nc)]
            a = pltpu.bitcast(jnp.concatenate(cols, axis=1), BF16)
            x = jnp.dot(m, a, preferred_element_type=F32)
            xr = x[:n1]
            xi = x[n1:]
            y = jnp.concatenate([xr * kr - xi * ki, xr * ki + xi * kr], axis=0).astype(BF16)
            pk = pltpu.bitcast(jnp.dot(mt, y, preferred_element_type=F32).astype(BF16), U32)
            for c in range(nc):
                o_ref[c, p, :, kk] = pk[:, c * LANES:(c + 1) * LANES].reshape(blk)


def _dft_mid(a, m2, m2t, kf, order):
    NC, P, G, N2, W, _ = a.shape
    R = m2.shape[1]
    C = NC * LANES
    kb = DFT_MID_K2
    blk = pl.BlockSpec((NC, P, G, kb, W, LANES), lambda k: (0, 0, 0, k, 0, 0))
    return pl.pallas_call(
        _dft_mid_kernel,
        grid=(N2 // kb,),
        in_specs=[blk,
                  pl.BlockSpec((kb, R, R), lambda k: (k, 0, 0)),
                  pl.BlockSpec((kb, R, R), lambda k: (k, 0, 0)),
                  pl.BlockSpec((kb, R, C), lambda k: (k, 0, order))],
        out_specs=blk,
        out_shape=jax.ShapeDtypeStruct(a.shape, U32),
        compiler_params=_cparams(("parallel",)),
        name="dft_mid",
    )(a, m2, m2t, kf)


def _dft_c_pk_kernel(m_ref, b_ref, z_ref, g_ref, bias_ref, o_ref, *, tm, nh, npairs):
    j0 = pl.program_id(2) * tm
    rows = b_ref.shape[3] // tm
    bias = bias_ref[...]
    both = [(0,), (1,)]
    for j in range(tm):
        rhs = _packed_rows(b_ref, [(0, 0)], j, rows, tm)
        y = jnp.dot(m_ref[...], rhs, preferred_element_type=F32)
        z = _packed_rows(z_ref, both, j0 + j, nh, npairs).astype(F32)
        g = _packed_rows(g_ref, both, j0 + j, nh, npairs).astype(F32)
        pk = pltpu.bitcast((g * (y + z * bias)).astype(BF16), U32)
        for c in range(o_ref.shape[0]):
            for b in range(2):
                o_ref[c, b, pl.ds(j0 + j, nh, stride=npairs), :] = pk[b * nh:(b + 1) * nh, c * LANES:(c + 1) * LANES]


def _dft_c_pk(m3x, b, z, zslot, g, gslot, bias):
    NC, P, G, W, _ = b.shape
    L2 = z.shape[3]
    npairs = DFT_N1 // 2
    nh = L2 // npairs
    tm, ncb = DFT_PAIRS, DFT_LANE_BLOCKS
    seq = lambda slot: pl.BlockSpec((None, ncb, 2, L2, LANES), lambda p, h, j: (slot, h, p, 0, 0))
    return pl.pallas_call(
        functools.partial(_dft_c_pk_kernel, tm=tm, nh=nh, npairs=npairs),
        grid=(P, NC // ncb, G),
        in_specs=[pl.BlockSpec(m3x.shape, lambda p, h, j: (0, 0)),
                  pl.BlockSpec((ncb, 1, 1, W, LANES), lambda p, h, j: (h, p, j, 0, 0)),
                  seq(zslot), seq(gslot),
                  pl.BlockSpec((1, ncb * LANES), lambda p, h, j: (0, h))],
        out_specs=seq(0),
        out_shape=jax.ShapeDtypeStruct((1, NC, 2 * P, L2, LANES), U32),
        compiler_params=_cparams(("parallel", "parallel", "arbitrary")),
        name="dft_c",
    )(m3x, b, z, g, bias)


def _mid_row_order(step, per):
    N1 = DFT_N1
    idx = []
    for mt in range(N1 // per // step):
        for ri in range(2):
            for jj in range(step):
                for sub in range(per):
                    idx.append(ri * N1 + per * (mt * step + jj) + sub)
    return np.asarray(idx, np.int32)


def _hyena(u3, kern, hy_bias, mats):
    _, NC, B, L2, _ = u3.shape
    C = NC * LANES
    N1 = DFT_N1
    N2 = 4 * L2 // N1
    af = _dft_filter_a(mats["m1f"], kern)
    NB, G, W, _ = af.shape
    kf = _dft_filt(af.reshape(NB, G, N2, W // N2, LANES), mats["m2f"])
    bias = hy_bias.astype(F32)

    def long_conv(z, zslot, gslot, order):
        a = _dft_a_pk(mats["m1x"], z, zslot)
        NCa, P, G, W, _ = a.shape
        b = _dft_mid(a.reshape(NCa, P, G, N2, W // N2, LANES), mats["m2p"], mats["m2tp"], kf, order)
        return _dft_c_pk(mats["m3x"], b.reshape(a.shape), z, zslot, u3, gslot, bias[order][None, :])

    z = long_conv(u3, 2, 0, 0)
    return long_conv(z, 0, 1, 1)


SHIFT_LIMIT = 50.0


def _attn_kernel(lam_ref, q_ref, k_ref, v_ref, g_ref, o_ref, kmax_ref, *, post):
    k = k_ref[0]
    v = v_ref[0]

    @pl.when(pl.program_id(2) == 0)
    def _():
        kmax_ref[...] = jnp.max(jnp.abs(k.astype(F32)), axis=0, keepdims=True)

    q = q_ref[0]
    lane = lax.broadcasted_iota(jnp.int32, q.shape, 1)
    lo = lane < HEAD_DIM
    zero = jnp.zeros_like(q)
    reach = jnp.abs(q.astype(F32)) * kmax_ref[...]
    shifts = [jnp.sum(jnp.where(lo, reach, 0.0), axis=-1, keepdims=True),
              jnp.sum(jnp.where(lo, 0.0, reach), axis=-1, keepdims=True)]
    worst = jnp.max(jnp.maximum(shifts[0], shifts[1]))

    def run(row_max):
        outs = []
        for c in range(2):
            qc = jnp.where(lo, q, zero) if c == 0 else jnp.where(lo, zero, q)
            s = lax.dot_general(qc, k, (((1,), (1,)), ((), ())), preferred_element_type=F32)
            m = jnp.max(s, axis=-1, keepdims=True) if row_max else shifts[c]
            p = jnp.exp2(s - m)
            l = jnp.sum(p, axis=-1, keepdims=True)
            outs.append(jnp.dot(p.astype(BF16), v, preferred_element_type=F32) / l)
        a = outs[0] - lam_ref[0] * outs[1]
        ms = jnp.mean(a * a, axis=-1, keepdims=True)
        o_ref[0] = (a * lax.rsqrt(ms + RMS_EPS) * g_ref[...] * post).astype(o_ref.dtype)

    @pl.when(worst <= SHIFT_LIMIT)
    def _():
        run(False)

    @pl.when(jnp.logical_not(worst <= SHIFT_LIMIT))
    def _():
        run(True)


def _attention(qn, kn, proj3, vcol, lam, subln_g, post):
    B, L, W = qn.shape
    tq = min(512, L)
    return pl.pallas_call(
        functools.partial(_attn_kernel, post=post),
        grid=(B, W // V_DIM, L // tq),
        in_specs=[pl.BlockSpec(memory_space=pltpu.SMEM),
                  pl.BlockSpec((1, tq, V_DIM), lambda b, h, i: (b, i, h)),
                  pl.BlockSpec((1, L, V_DIM), lambda b, h, i: (b, 0, h)),
                  pl.BlockSpec((1, L, V_DIM), lambda b, h, i: (b, 0, vcol + h)),
                  pl.BlockSpec((1, V_DIM), lambda b, h, i: (0, 0))],
        out_specs=pl.BlockSpec((1, tq, V_DIM), lambda b, h, i: (b, i, h)),
        out_shape=jax.ShapeDtypeStruct((B, L, W), BF16),
        scratch_shapes=[pltpu.VMEM((1, V_DIM), F32)],
        compiler_params=_cparams(("parallel", "parallel", "arbitrary")),
        name="diff_attention",
    )(lam.reshape(1).astype(F32), qn, kn, proj3, subln_g.astype(F32).reshape(1, V_DIM))


def _merge_kernel(x_ref, yh_ref, ya_ref, gh_ref, ga_ref, who_ref, wao_ref, wout_ref, g_ref, *rest,
                  with_router):
    if with_router:
        wr_ref, xo_ref, hn_ref, lg_ref = rest
    else:
        xo_ref, hn_ref = rest
    yh = jnp.concatenate([pltpu.bitcast(yh_ref[c], BF16) for c in range(yh_ref.shape[0])], axis=1)
    th = jnp.dot(yh, who_ref[...], preferred_element_type=F32)
    ta = jnp.dot(ya_ref[...], wao_ref[...], preferred_element_type=F32)
    merged = (jax.nn.sigmoid(gh_ref[...].astype(F32)) * th
              + jax.nn.sigmoid(ga_ref[...].astype(F32)) * ta)
    xn = x_ref[...] + jnp.dot(merged.astype(BF16), wout_ref[...], preferred_element_type=F32)
    xo_ref[...] = xn
    ms = jnp.mean(xn * xn, axis=-1, keepdims=True)
    h = xn * lax.rsqrt(ms + RMS_EPS) * g_ref[...]
    hn_ref[...] = h.astype(hn_ref.dtype)
    if with_router:
        lane = lax.broadcasted_iota(jnp.int32, lg_ref.shape, 1)
        lg = jnp.zeros(lg_ref.shape, F32)
        for e in range(wr_ref.shape[0]):
            lg = jnp.where(lane == e, jnp.sum(h * wr_ref[e:e + 1, :], axis=-1, keepdims=True), lg)
        lg_ref[...] = lg


def _merge(x2, yh, ya, proj, gcol, who, wao, wout, g_ffn, w_router):
    T, D = x2.shape
    W = ya.shape[1]
    tm = min(512, T)
    with_router = w_router is not None
    row = lambda w: pl.BlockSpec((tm, w), lambda i: (i, 0))
    full = lambda a: pl.BlockSpec(a.shape, lambda i: (0, 0))
    in_specs = [row(D), pl.BlockSpec((yh.shape[0], tm // 2, LANES), lambda i: (0, i, 0)), row(W),
                pl.BlockSpec((tm, D), lambda i: (i, gcol)),
                pl.BlockSpec((tm, D), lambda i: (i, gcol + 1)),
                full(who), full(wao), full(wout), pl.BlockSpec((1, D), lambda i: (0, 0))]
    args = [x2, yh, ya, proj, proj, who, wao, wout, g_ffn.astype(F32).reshape(1, D)]
    out_specs = [row(D), row(D)]
    out_shape = [jax.ShapeDtypeStruct((T, D), F32),
                 jax.ShapeDtypeStruct((T, D), F32 if with_router else BF16)]
    if with_router:
        wr = w_router.astype(F32).T
        in_specs.append(full(wr))
        args.append(wr)
        out_specs.append(row(LANES))
        out_shape.append(jax.ShapeDtypeStruct((T, LANES), F32))
    return pl.pallas_call(
        functools.partial(_merge_kernel, with_router=with_router),
        grid=(T // tm,),
        in_specs=in_specs,
        out_specs=out_specs,
        out_shape=out_shape,
        compiler_params=_cparams(("parallel",)),
        name="merge_outproj",
    )(*args)


FF_CHUNK = 2 * LANES


def _swiglu(h, wg_ref, wu_ref, wd_ref):
    F = wg_ref.shape[-1]
    acc = None
    for lo in range(0, F, FF_CHUNK):
        sl = slice(lo, min(lo + FF_CHUNK, F))
        g = jnp.dot(h, wg_ref[:, sl], preferred_element_type=F32)
        u = jnp.dot(h, wu_ref[:, sl], preferred_element_type=F32)
        a = (g * jax.nn.sigmoid(g) * u).astype(BF16)
        d = jnp.dot(a, wd_ref[sl, :], preferred_element_type=F32)
        acc = d if acc is None else acc + d
    return acc


def _ffn_kernel(h_ref, x_ref, wg_ref, wu_ref, wd_ref, o_ref):
    o_ref[...] = x_ref[...] + _swiglu(h_ref[...], wg_ref, wu_ref, wd_ref)


def _dense_ffn(hn, x2, wg, wu, wd, layer):
    T, D = x2.shape
    F = wg.shape[2]
    tm = min(512, T)
    row = pl.BlockSpec((tm, D), lambda i: (i, 0))
    return pl.pallas_call(
        _ffn_kernel,
        grid=(T // tm,),
        in_specs=[row, row,
                  pl.BlockSpec((None, D, F), lambda i: (layer, 0, 0)),
                  pl.BlockSpec((None, D, F), lambda i: (layer, 0, 0)),
                  pl.BlockSpec((None, F, D), lambda i: (layer, 0, 0))],
        out_specs=row,
        out_shape=jax.ShapeDtypeStruct((T, D), F32),
        compiler_params=_cparams(("parallel",)),
        name="dense_ffn",
    )(hn, x2, wg, wu, wd)


def _row_copy(src_hbm, row, dst, r, sem):
    return pltpu.make_async_copy(src_hbm.at[pl.ds(row, 1)], dst.at[pl.ds(r, 1)], sem)


def _rows_done(src_hbm, dst, sem):
    return pltpu.make_async_copy(src_hbm.at[pl.ds(0, dst.shape[0])], dst, sem)


def _expert_kernel(be_ref, tok_ref, hn_hbm, wg_ref, wu_ref, wd_ref, o_ref, xbuf, sem):
    i = pl.program_id(0)
    last = pl.num_programs(0) - 1
    rows = xbuf.shape[1]
    slot = i % 2
    nxt = 1 - slot

    @pl.when(i == 0)
    def _():
        def issue(r, carry):
            _row_copy(hn_hbm, tok_ref[r], xbuf.at[0], r, sem.at[0]).start()
            return carry

        lax.fori_loop(0, rows, issue, 0, unroll=8)

    _rows_done(hn_hbm, xbuf.at[slot], sem.at[slot]).wait()
    base = jnp.minimum(i + 1, last) * rows
    for r in range(rows):
        _row_copy(hn_hbm, tok_ref[base + r], xbuf.at[nxt], r, sem.at[nxt]).start()
    o_ref[...] = _swiglu(xbuf[slot].astype(BF16), wg_ref.at[0], wu_ref.at[0], wd_ref.at[0])

    @pl.when(i == last)
    def _():
        _rows_done(hn_hbm, xbuf.at[nxt], sem.at[nxt]).wait()


def _experts(block_e, tok_of_slot, hn, wg, wu, wd, layer):
    T, D = hn.shape
    F = wg.shape[3]
    P = tok_of_slot.shape[0]
    rows = MOE_ROWS
    wspec = lambda shape: pl.BlockSpec((None, 1) + shape, lambda i, be, tok: (layer, be[i], 0, 0))
    return pl.pallas_call(
        _expert_kernel,
        grid_spec=pltpu.PrefetchScalarGridSpec(
            num_scalar_prefetch=2,
            grid=(P // rows,),
            in_specs=[pl.BlockSpec(memory_space=pl.ANY),
                      wspec((D, F)), wspec((D, F)), wspec((F, D))],
            out_specs=pl.BlockSpec((rows, D), lambda i, be, tok: (i, 0)),
            scratch_shapes=[pltpu.VMEM((2, rows, D), F32), pltpu.SemaphoreType.DMA((2,))]),
        out_shape=jax.ShapeDtypeStruct((P, D), F32),
        compiler_params=_cparams(("arbitrary",)),
        name="moe_experts",
    )(block_e, tok_of_slot, hn, wg, wu, wd)


def _combine_kernel(dest_ref, yp_hbm, x_ref, w_ref, o_ref, buf0, buf1, sem):
    i = pl.program_id(0)
    rows = buf0.shape[1]

    def gather(tile, slot):
        base = tile * rows * TOP_K

        for r in range(rows):
            _row_copy(yp_hbm, dest_ref[base + TOP_K * r], buf0.at[slot], r, sem.at[slot]).start()
            _row_copy(yp_hbm, dest_ref[base + TOP_K * r + 1], buf1.at[slot], r, sem.at[slot]).start()

    @pl.when(i == 0)
    def _():
        gather(0, 0)

    @pl.when(i + 1 < pl.num_programs(0))
    def _():
        gather(i + 1, (i + 1) % 2)

    slot = i % 2
    _rows_done(yp_hbm, buf0.at[slot], sem.at[slot]).wait()
    _rows_done(yp_hbm, buf1.at[slot], sem.at[slot]).wait()
    w = w_ref[...]
    o_ref[...] = x_ref[...] + (buf0[slot] * w[:, 0:1] + buf1[slot] * w[:, 1:2])


def _combine(dest, yp, x2, top_w):
    T, D = x2.shape
    tm = min(256, T)
    return pl.pallas_call(
        _combine_kernel,
        grid_spec=pltpu.PrefetchScalarGridSpec(
            num_scalar_prefetch=1,
            grid=(T // tm,),
            in_specs=[pl.BlockSpec(memory_space=pl.ANY),
                      pl.BlockSpec((tm, D), lambda i, d: (i, 0)),
                      pl.BlockSpec((tm, TOP_K), lambda i, d: (i, 0))],
            out_specs=pl.BlockSpec((tm, D), lambda i, d: (i, 0)),
            scratch_shapes=[pltpu.VMEM((2, tm, D), F32), pltpu.VMEM((2, tm, D), F32),
                            pltpu.SemaphoreType.DMA((2,))]),
        out_shape=jax.ShapeDtypeStruct((T, D), F32),
        compiler_params=_cparams(("arbitrary",)),
        name="moe_combine",
    )(dest, yp, x2, top_w)


def _route(logits, n_experts, rows):
    T = logits.shape[0]
    top_val, top_idx = lax.top_k(logits[:, :n_experts], TOP_K)
    top_w = jax.nn.softmax(top_val, axis=-1)
    A = T * TOP_K
    e_flat = top_idx.reshape(A).astype(jnp.int32)
    onehot = (e_flat[:, None] == jnp.arange(n_experts, dtype=jnp.int32)[None, :]).astype(jnp.int32)
    csum = jnp.cumsum(onehot, axis=0)
    counts = csum[-1]
    rank = jnp.sum((csum - onehot) * onehot, axis=1)
    padded = (counts + rows - 1) // rows * rows
    pad_end = jnp.cumsum(padded)
    pad_start = pad_end - padded
    dest = pad_start[e_flat] + rank
    n_blocks = -(-A // rows) + n_experts
    P = n_blocks * rows
    order = jnp.argsort(e_flat)
    starts = jnp.cumsum(counts) - counts
    slot = jnp.arange(P, dtype=jnp.int32)
    slot_e = jnp.minimum(jnp.sum((slot[:, None] >= pad_end[None, :]).astype(jnp.int32), axis=1),
                         n_experts - 1)
    within = slot - pad_start[slot_e]
    valid = within < counts[slot_e]
    src = jnp.clip(starts[slot_e] + within, 0, A - 1)
    tok_of_slot = jnp.where(valid, order[src] // TOP_K, 0).astype(jnp.int32)
    block_e = slot_e[::rows]
    return top_w.astype(F32), dest.astype(jnp.int32), tok_of_slot, block_e


def kernel(x, norm_mix, w_in, conv_w, w_f1, b_f1, freq1, w_f2, b_f2, freq2, w_f3, hy_bias, q_norm, k_norm, lam_q1, lam_k1, lam_q2, lam_k2, subln_g, w_hy_o, w_at_o, w_out, norm_ffn, w_dense_gate, w_dense_up, w_dense_down, w_router, w_moe_gate, w_moe_up, w_moe_down):
    B, L, D = x.shape
    T = B * L
    depth = w_in.shape[0]
    hw = w_hy_o.shape[1]
    qk_w = N_HEADS * 2 * HEAD_DIM
    at_w = N_HEADS * V_DIM
    mats = _dft_matrices(L)
    moe_w = tuple(w.astype(BF16) for w in (w_moe_gate, w_moe_up, w_moe_down))
    dense_w = tuple(w.astype(BF16) for w in (w_dense_gate, w_dense_up, w_dense_down))

    x2 = x.reshape(T, D)
    for layer in range(depth):
        lambda_init = 0.8 - 0.6 * math.exp(-0.3 * layer)
        u3, qn, kn, v, gates = _inproj(x2, L, norm_mix[layer].astype(F32), w_in[layer].astype(BF16),
                                       conv_w[layer].astype(F32), q_norm[layer], k_norm[layer], hw, qk_w, at_w)
        kern = _hyena_time_filters(L, w_f1[layer], b_f1[layer], freq1[layer], w_f2[layer], b_f2[layer],
                                   freq2[layer], w_f3[layer], hw)
        y_h = _hyena(u3.reshape(3, hw // LANES, B, L // 2, LANES), kern, hy_bias[layer], mats)
        lam = (jnp.exp(jnp.sum(lam_q1[layer].astype(F32) * lam_k1[layer].astype(F32)))
               - jnp.exp(jnp.sum(lam_q2[layer].astype(F32) * lam_k2[layer].astype(F32))) + lambda_init)
        y_a = _attention(qn.reshape(B, L, qk_w), kn.reshape(B, L, qk_w), v.reshape(B, L, at_w), 0, lam,
                         subln_g[layer], 1.0 - lambda_init)
        i = layer // 2
        moe = layer % 2 == 1
        outs = _merge(x2, y_h.reshape(hw // LANES, T // 2, LANES), y_a.reshape(T, at_w), gates, 0,
                      w_hy_o[layer].astype(BF16), w_at_o[layer].astype(BF16), w_out[layer].astype(BF16),
                      norm_ffn[layer], w_router[i] if moe else None)
        if not moe:
            x2, hn = outs
            x2 = _dense_ffn(hn, x2, *dense_w, i)
        else:
            x2, hn, logits = outs
            n_experts = w_router.shape[2]
            top_w, dest, tok_of_slot, block_e = _route(logits, n_experts, MOE_ROWS)
            yp = _experts(block_e, tok_of_slot, hn, *moe_w, i)
            x2 = _combine(dest, yp, x2, top_w)
    return x2.reshape(B, L, D)
```

```python
import functools
import math

import numpy as np
import jax
import jax.numpy as jnp
from jax import lax
from jax.experimental import pallas as pl
from jax.experimental.pallas import tpu as pltpu

F32 = jnp.float32
BF16 = jnp.bfloat16
U32 = jnp.uint32

N_HEADS = 4
HEAD_DIM = 64
V_DIM = 2 * HEAD_DIM
ROPE_THETA = 10000.0
FILTER_BANDS = 16
HYENA_TARGET = 1e-2
FAST_DECAY_PCT = 0.3
SLOW_DECAY_PCT = 1.5
TOP_K = 2
RMS_EPS = 1e-6

LANES = 128
BF16_SUBLANES = 16
VMEM_LIMIT = 56 * 1024 * 1024

DFT_N1 = 128
DFT_PAIRS = 16
DFT_FILTER_STEP = 16
DFT_MID_K2 = 4
DFT_LANE_BLOCKS = 2
MOE_ROWS = 512


def _cparams(sem):
    return pltpu.CompilerParams(dimension_semantics=sem, vmem_limit_bytes=VMEM_LIMIT)


def _inproj_kernel(x_ref, xp_ref, xn_ref, g_ref, w_ref, cw_ref, cos_ref, sin_ref, qg_ref, kg_ref, grp_ref,
                   u_ref, q_ref, k_ref, v_ref, gate_ref, *, hw, qk_w, tiles_per_seq):
    pos = pl.program_id(0) % tiles_per_seq
    tm = x_ref.shape[0]
    halo = xp_ref.shape[0]

    def normed(x):
        ms = jnp.mean(x * x, axis=-1, keepdims=True)
        return (x * lax.rsqrt(ms + RMS_EPS) * g_ref[...]).astype(BF16)

    hn = normed(x_ref[...])
    hext = jnp.concatenate([normed(xp_ref[...]), hn, normed(xn_ref[...])], axis=0)
    row = lax.broadcasted_iota(jnp.int32, (tm + 2 * halo, hw), 0)
    outside = jnp.logical_or(jnp.logical_and(row < halo, pos == 0),
                             jnp.logical_and(row >= tm + halo, pos == tiles_per_seq - 1))
    hy_all = jnp.dot(hext, w_ref[:, :3 * hw], preferred_element_type=F32)
    rest = jnp.dot(hn, w_ref[:, 3 * hw:], preferred_element_type=F32)

    nc = u_ref.shape[1]
    for s in range(u_ref.shape[0]):
        cols = slice(s * hw, (s + 1) * hw)
        hy = jnp.where(outside, 0.0, hy_all[:, cols])
        up = pltpu.roll(hy, 1, 0)[halo:halo + tm]
        dn = pltpu.roll(hy, tm + 2 * halo - 1, 0)[halo:halo + tm]
        cw = cw_ref[:, cols]
        res = (up * cw[0:1] + hy[halo:halo + tm] * cw[1:2] + dn * cw[2:3]).astype(BF16)
        pk = pltpu.bitcast(res, U32)
        for c in range(nc):
            u_ref[s, c] = pk[:, c * LANES:(c + 1) * LANES]

    cos = cos_ref[...]
    sin = sin_ref[...]
    lane = lax.broadcasted_iota(jnp.int32, cos.shape, 1)
    first = (lane % HEAD_DIM) < (HEAD_DIM // 2)
    grp = grp_ref[...]
    base = 0
    for o_ref, gain_ref in ((q_ref, qg_ref), (k_ref, kg_ref)):
        for h in range(qk_w // LANES):
            x = rest[:, base + h * LANES:base + (h + 1) * LANES]
            sq_hi, sq_lo = _split_bf16(x * x)
            ss = (jnp.dot(sq_hi, grp, preferred_element_type=F32)
                  + jnp.dot(sq_lo, grp, preferred_element_type=F32))
            xn = x * lax.rsqrt(ss * (1.0 / HEAD_DIM) + RMS_EPS) * gain_ref[...]
            partner = jnp.where(first, pltpu.roll(xn, LANES - HEAD_DIM // 2, 1),
                                pltpu.roll(xn, HEAD_DIM // 2, 1))
            o_ref[:, h * LANES:(h + 1) * LANES] = (xn * cos + partner * sin).astype(o_ref.dtype)
        base += qk_w
    v_ref[...] = rest[:, base:base + v_ref.shape[1]].astype(v_ref.dtype)
    gate_ref[...] = rest[:, base + v_ref.shape[1]:].astype(gate_ref.dtype)


def _inproj(x2, L, g, w, conv_w, q_norm, k_norm, hw, qk_w, at_w):
    T, D = x2.shape
    tm = min(512, L)
    halo = BF16_SUBLANES
    hb = tm // halo
    nhb = T // halo
    nc = hw // LANES
    gw = w.shape[1] - 3 * hw - 2 * qk_w - at_w
    cos_t, sin_t = _rope_tables(L)
    qg = jnp.tile(q_norm.astype(F32), LANES // HEAD_DIM)[None, :] * F32(HEAD_DIM ** -0.5 * math.log2(math.e))
    kg = jnp.tile(k_norm.astype(F32), LANES // HEAD_DIM)[None, :]
    lanes = np.arange(LANES)
    grp = jnp.asarray(lanes[:, None] // HEAD_DIM == lanes[None, :] // HEAD_DIM, BF16)
    tps = L // tm
    const = lambda a: pl.BlockSpec(a.shape, lambda i: (0,) * a.ndim)
    rows = lambda width: pl.BlockSpec((tm, width), lambda i: (i, 0))
    tab = pl.BlockSpec((tm, LANES), lambda i: (i % tps, 0))
    gr = g.reshape(1, D)
    return pl.pallas_call(
        functools.partial(_inproj_kernel, hw=hw, qk_w=qk_w, tiles_per_seq=tps),
        grid=(T // tm,),
        in_specs=[rows(D),
                  pl.BlockSpec((halo, D), lambda i: (jnp.maximum(i * hb - 1, 0), 0)),
                  pl.BlockSpec((halo, D), lambda i: (jnp.minimum((i + 1) * hb, nhb - 1), 0)),
                  const(gr), const(w), const(conv_w), tab, tab, const(qg), const(kg), const(grp)],
        out_specs=[pl.BlockSpec((3, nc, tm // 2, LANES), lambda i: (0, 0, i, 0)),
                   rows(qk_w), rows(qk_w), rows(at_w), rows(gw)],
        out_shape=[jax.ShapeDtypeStruct((3, nc, T // 2, LANES), U32),
                   jax.ShapeDtypeStruct((T, qk_w), BF16), jax.ShapeDtypeStruct((T, qk_w), BF16),
                   jax.ShapeDtypeStruct((T, at_w), BF16), jax.ShapeDtypeStruct((T, gw), BF16)],
        compiler_params=_cparams(("parallel",)),
        name="inproj",
    )(x2, x2, x2, gr, w, conv_w, cos_t, sin_t, qg, kg, grp)


def _rope_tables(L):
    half = HEAD_DIM // 2
    inv = 1.0 / (ROPE_THETA ** (np.arange(0, HEAD_DIM, 2, dtype=np.float64) / HEAD_DIM))
    ang = np.arange(L, dtype=np.float64)[:, None] * inv[None, :]
    cos, sin = np.cos(ang), np.sin(ang)
    cos_t = np.tile(cos, (1, LANES // half))
    sin_t = np.tile(np.concatenate([-sin, sin], axis=1), (1, LANES // HEAD_DIM))
    return jnp.asarray(cos_t, F32), jnp.asarray(sin_t, F32)


def _split_bf16(x):
    hi = x.astype(BF16)
    return hi, (x - hi.astype(F32)).astype(BF16)


def _dot3(a, b):
    ah, al = _split_bf16(a)
    bh, bl = _split_bf16(b)
    d = lambda x, y: jnp.dot(x, y, preferred_element_type=F32)
    return d(ah, bh) + (d(ah, bl) + d(al, bh))


def _filter_kernel(zz_ref, w1_ref, b1_ref, f1_ref, w2_ref, b2_ref, f2_ref, w3_ref, dl_ref, o_ref):
    zz = zz_ref[...]
    h = _dot3(zz, w1_ref[...])
    h = jnp.sin(f1_ref[...] * (h + b1_ref[...]))
    h = _dot3(h, w2_ref[...])
    h = jnp.sin(f2_ref[...] * (h + b2_ref[...]))
    hf = _dot3(h, w3_ref[0])
    t = zz[:, 0:1]
    mask = zz[:, _MASK_COL:_MASK_COL + 1]
    kern = hf * (jnp.exp(-t * dl_ref[...]) * mask)
    for c in range(o_ref.shape[0]):
        o_ref[c] = kern[:, c * LANES:(c + 1) * LANES]


_MASK_COL = 2 * FILTER_BANDS + 1


def _filter_positions(L):
    bands = FILTER_BANDS
    t = np.linspace(0.0, 1.0, L, dtype=np.float64)[:, None]
    w = 2.0 * math.pi * np.arange(L, dtype=np.float64)[:, None] / L
    f = np.linspace(1e-4, bands - 1, bands, dtype=np.float64)[None, :]
    z = np.concatenate([t, np.cos(f * w), -np.sin(f * w)], axis=-1)
    zz = np.zeros((2 * L, LANES), np.float64)
    zz[:L, :z.shape[1]] = z
    zz[L + 1:, :z.shape[1]] = z[1:][::-1]
    zz[:, _MASK_COL] = 1.0
    zz[L, :] = 0.0
    return zz.astype(np.float32)


def _hyena_time_filters(L, w_f1, b_f1, freq1, w_f2, b_f2, freq2, w_f3, hw):
    emb, hid = w_f1.shape
    zz = jnp.asarray(_filter_positions(L))
    w1 = jnp.zeros((LANES, hid), F32).at[:emb].set(w_f1)
    w3 = w_f3.reshape(hid, 2, 2, hw).transpose(2, 0, 1, 3).reshape(2, hid, 2 * hw)
    max_decay = math.log(HYENA_TARGET) / FAST_DECAY_PCT
    min_decay = math.log(HYENA_TARGET) / SLOW_DECAY_PCT
    deltas = np.abs(np.linspace(min_decay, max_decay, hw, dtype=np.float64))
    dl = jnp.asarray(np.tile(deltas, 2)[None, :], F32)
    tr = min(512, L)
    per_dir = L // tr
    row = lambda a: a.reshape(1, -1)
    return pl.pallas_call(
        _filter_kernel,
        grid=(2 * L // tr,),
        in_specs=[pl.BlockSpec((tr, LANES), lambda i: (i, 0)),
                  pl.BlockSpec((LANES, hid), lambda i: (0, 0)),
                  pl.BlockSpec((1, hid), lambda i: (0, 0)),
                  pl.BlockSpec((1, hid), lambda i: (0, 0)),
                  pl.BlockSpec((hid, hid), lambda i: (0, 0)),
                  pl.BlockSpec((1, hid), lambda i: (0, 0)),
                  pl.BlockSpec((1, hid), lambda i: (0, 0)),
                  pl.BlockSpec((1, hid, 2 * hw), lambda i: (i // per_dir, 0, 0)),
                  pl.BlockSpec((1, 2 * hw), lambda i: (0, 0))],
        out_specs=pl.BlockSpec((2 * hw // LANES, tr, LANES), lambda i: (0, i, 0)),
        out_shape=jax.ShapeDtypeStruct((2 * hw // LANES, 2 * L, LANES), F32),
        compiler_params=_cparams(("parallel",)),
        name="hyena_filter",
    )(zz, w1, row(b_f1), row(freq1), w_f2, row(b_f2), row(freq2), w3, dl)


def _dft_matrices(L):
    N = 2 * L
    N1 = DFT_N1
    N2 = N // N1
    h = N2 // 2
    ang = 2.0 * np.pi * np.outer(np.arange(N2), np.arange(N2)) / N2
    c, s = np.cos(ang), np.sin(ang)
    m1 = np.zeros((N2, 2, N2))
    m1[:, 0, :h], m1[:, 0, h:] = c[:, :h], s[:, :h]
    m1[:, 1, :h], m1[:, 1, h:] = -s[:, :h], c[:, :h]
    m1 = m1.reshape(2 * N2, N2)
    m1f = np.stack([c, -s], axis=1).reshape(2 * N2, N2)
    m3 = np.zeros((2, h, N2, 2))
    m3[0, :, :, 0], m3[0, :, :, 1] = c[:h], -s[:h]
    m3[1, :, :, 0], m3[1, :, :, 1] = s[:h], c[:h]
    m3 = (m3 / N).reshape(N2, 2 * N2)
    def mid(order, transposed):
        f = np.arange(2 * N1)[:, None]
        d = np.asarray(order)[None, :]
        a = 2.0 * np.pi * ((d % N1) * (f % N1) % N1) / N1
        same, up = (f // N1 == d // N1), (f // N1 < d // N1)
        pc = np.where(same, 1.0, 0.0)
        ps = np.where(same, 0.0, np.where(up, 1.0, -1.0))
        u = pc * np.cos(a) + ps * np.sin(a)
        v = ps * np.cos(a) - pc * np.sin(a)
        b = 2.0 * np.pi * (np.arange(N2)[:, None] * (d % N1) % N) / N
        if transposed:
            u, v = u.T, v.T
            cb, sb = jnp.asarray(np.cos(b), F32)[:, :, None], jnp.asarray(np.sin(b), F32)[:, :, None]
        else:
            cb, sb = jnp.asarray(np.cos(b), F32)[:, None, :], jnp.asarray(np.sin(b), F32)[:, None, :]
        return jnp.asarray(u, F32)[None] * cb + jnp.asarray(v, F32)[None] * sb

    packed = _mid_row_order(DFT_PAIRS, 2)
    eye2 = np.eye(2)
    return dict(m1x=jnp.asarray(np.kron(m1, eye2), BF16),
                m3x=jnp.asarray(np.kron(m3, eye2), BF16),
                m1f=jnp.asarray(m1f, F32),
                m2f=mid(_mid_row_order(DFT_FILTER_STEP, 1), False),
                m2p=mid(packed, False).astype(BF16),
                m2tp=mid(packed, True).astype(BF16))


def _dft_filter_a_kernel(m_ref, k_ref, o_ref, *, step, n2):
    j0 = pl.program_id(1) * step
    rows = m_ref.shape[0]
    for jj in range(step):
        rhs = jnp.concatenate([k_ref[c, pl.ds(j0 + jj, n2, stride=DFT_N1), :] for c in range(k_ref.shape[0])],
                              axis=1)
        res = _dot3(m_ref[...], rhs)
        for c in range(o_ref.shape[0]):
            o_ref[c, 0, pl.ds(jj, rows, stride=step), :] = res[:, c * LANES:(c + 1) * LANES]


def _dft_filter_a(m1f, kern):
    NB, N, _ = kern.shape
    n2 = N // DFT_N1
    step, ncb = DFT_FILTER_STEP, DFT_LANE_BLOCKS
    rows = m1f.shape[0]
    return pl.pallas_call(
        functools.partial(_dft_filter_a_kernel, step=step, n2=n2),
        grid=(NB // ncb, DFT_N1 // step),
        in_specs=[pl.BlockSpec(m1f.shape, lambda h, j: (0, 0)),
                  pl.BlockSpec((ncb, N, LANES), lambda h, j: (h, 0, 0))],
        out_specs=pl.BlockSpec((ncb, 1, rows * step, LANES), lambda h, j: (h, j, 0, 0)),
        out_shape=jax.ShapeDtypeStruct((NB, DFT_N1 // step, rows * step, LANES), F32),
        compiler_params=_cparams(("parallel", "arbitrary")),
        name="dft_filter_a",
    )(m1f, kern)


def _dft_filt_kernel(a_ref, m_ref, o_ref):
    rows = m_ref.shape[1]
    a = jnp.concatenate([a_ref[c, :, 0].reshape(rows, LANES) for c in range(a_ref.shape[0])], axis=1)
    o_ref[0] = _dot3(m_ref[0], a)


def _dft_filt(a, m2):
    NB, G, N2, W, _ = a.shape
    R = m2.shape[1]
    return pl.pallas_call(
        _dft_filt_kernel,
        grid=(N2,),
        in_specs=[pl.BlockSpec((NB, G, 1, W, LANES), lambda k: (0, 0, k, 0, 0)),
                  pl.BlockSpec((1, R, R), lambda k: (k, 0, 0))],
        out_specs=pl.BlockSpec((1, R, NB * LANES), lambda k: (k, 0, 0)),
        out_shape=jax.ShapeDtypeStruct((N2, R, NB * LANES), F32),
        compiler_params=_cparams(("parallel",)),
        name="dft_filter_mid",
    )(a, m2)


def _packed_rows(ref, lead, start, size, stride):
    cols = [jnp.concatenate([ref[(c,) + l + (pl.ds(start, size, stride=stride), slice(None))] for l in lead], axis=0)
            for c in range(ref.shape[0])]
    return pltpu.bitcast(jnp.concatenate(cols, axis=1), BF16)


def _dft_a_pk_kernel(m_ref, z_ref, o_ref, *, tm, nh, npairs):
    j0 = pl.program_id(2) * tm
    rows = m_ref.shape[0] // 2
    for j in range(tm):
        rhs = _packed_rows(z_ref, [(0,), (1,)], j0 + j, nh, npairs)
        res = jnp.dot(m_ref[...], rhs, preferred_element_type=F32)
        pk = pltpu.bitcast(res.astype(BF16), U32)
        for c in range(o_ref.shape[0]):
            o_ref[c, 0, 0, pl.ds(j, rows, stride=tm), :] = pk[:, c * LANES:(c + 1) * LANES]


def _dft_a_pk(m1x, z, slot):
    _, NC, B, L2, _ = z.shape
    npairs = DFT_N1 // 2
    nh = L2 // npairs
    tm, ncb = DFT_PAIRS, DFT_LANE_BLOCKS
    rows = m1x.shape[0] // 2
    P = B // 2
    return pl.pallas_call(
        functools.partial(_dft_a_pk_kernel, tm=tm, nh=nh, npairs=npairs),
        grid=(P, NC // ncb, npairs // tm),
        in_specs=[pl.BlockSpec(m1x.shape, lambda p, h, j: (0, 0)),
                  pl.BlockSpec((None, ncb, 2, L2, LANES), lambda p, h, j: (slot, h, p, 0, 0))],
        out_specs=pl.BlockSpec((ncb, 1, 1, rows * tm, LANES), lambda p, h, j: (h, p, j, 0, 0)),
        out_shape=jax.ShapeDtypeStruct((NC, P, npairs // tm, rows * tm, LANES), U32),
        compiler_params=_cparams(("parallel", "parallel", "arbitrary")),
        name="dft_a",
    )(m1x, z)


def _dft_mid_kernel(a_ref, m_ref, mt_ref, kf_ref, o_ref):
    n1 = m_ref.shape[1] // 2
    nc, npair = a_ref.shape[0], a_ref.shape[1]
    blk = a_ref.shape[2], a_ref.shape[4], LANES
    for kk in range(a_ref.shape[3]):
        m = m_ref[kk]
        mt = mt_ref[kk]
        kr = kf_ref[kk, :n1]
        ki = kf_ref[kk, n1:]
        for p in range(npair):
            cols = [a_ref[c, p, :, kk].reshape(n1, LANES) for c in range(nc)]
            a = pltpu.bitcast(jnp.concatenate(cols, axis=1), BF16)
            x = jnp.dot(m, a, preferred_element_type=F32)
            xr = x[:n1]
            xi = x[n1:]
            y = jnp.concatenate([xr * kr - xi * ki, xr * ki + xi * kr], axis=0).astype(BF16)
            pk = pltpu.bitcast(jnp.dot(mt, y, preferred_element_type=F32).astype(BF16), U32)
            for c in range(nc):
                o_ref[c, p, :, kk] = pk[:, c * LANES:(c + 1) * LANES].reshape(blk)


def _dft_mid(a, m2, m2t, kf, order):
    NC, P, G, N2, W, _ = a.shape
    R = m2.shape[1]
    C = NC * LANES
    kb = DFT_MID_K2
    blk = pl.BlockSpec((NC, P, G, kb, W, LANES), lambda k: (0, 0, 0, k, 0, 0))
    return pl.pallas_call(
        _dft_mid_kernel,
        grid=(N2 // kb,),
        in_specs=[blk,
                  pl.BlockSpec((kb, R, R), lambda k: (k, 0, 0)),
                  pl.BlockSpec((kb, R, R), lambda k: (k, 0, 0)),
                  pl.BlockSpec((kb, R, C), lambda k: (k, 0, order))],
        out_specs=blk,
        out_shape=jax.ShapeDtypeStruct(a.shape, U32),
        compiler_params=_cparams(("parallel",)),
        name="dft_mid",
    )(a, m2, m2t, kf)


def _dft_c_pk_kernel(m_ref, b_ref, z_ref, g_ref, bias_ref, o_ref, *, tm, nh, npairs):
    j0 = pl.program_id(2) * tm
    rows = b_ref.shape[3] // tm
    bias = bias_ref[...]
    both = [(0,), (1,)]
    for j in range(tm):
        rhs = _packed_rows(b_ref, [(0, 0)], j, rows, tm)
        y = jnp.dot(m_ref[...], rhs, preferred_element_type=F32)
        z = _packed_rows(z_ref, both, j0 + j, nh, npairs).astype(F32)
        g = _packed_rows(g_ref, both, j0 + j, nh, npairs).astype(F32)
        pk = pltpu.bitcast((g * (y + z * bias)).astype(BF16), U32)
        for c in range(o_ref.shape[0]):
            for b in range(2):
                o_ref[c, b, pl.ds(j0 + j, nh, stride=npairs), :] = pk[b * nh:(b + 1) * nh, c * LANES:(c + 1) * LANES]


def _dft_c_pk(m3x, b, z, zslot, g, gslot, bias):
    NC, P, G, W, _ = b.shape
    L2 = z.shape[3]
    npairs = DFT_N1 // 2
    nh = L2 // npairs
    tm, ncb = DFT_PAIRS, DFT_LANE_BLOCKS
    seq = lambda slot: pl.BlockSpec((None, ncb, 2, L2, LANES), lambda p, h, j: (slot, h, p, 0, 0))
    return pl.pallas_call(
        functools.partial(_dft_c_pk_kernel, tm=tm, nh=nh, npairs=npairs),
        grid=(P, NC // ncb, G),
        in_specs=[pl.BlockSpec(m3x.shape, lambda p, h, j: (0, 0)),
                  pl.BlockSpec((ncb, 1, 1, W, LANES), lambda p, h, j: (h, p, j, 0, 0)),
                  seq(zslot), seq(gslot),
                  pl.BlockSpec((1, ncb * LANES), lambda p, h, j: (0, h))],
        out_specs=seq(0),
        out_shape=jax.ShapeDtypeStruct((1, NC, 2 * P, L2, LANES), U32),
        compiler_params=_cparams(("parallel", "parallel", "arbitrary")),
        name="dft_c",
    )(m3x, b, z, g, bias)


def _mid_row_order(step, per):
    N1 = DFT_N1
    idx = []
    for mt in range(N1 // per // step):
        for ri in range(2):
            for jj in range(step):
                for sub in range(per):
                    idx.append(ri * N1 + per * (mt * step + jj) + sub)
    return np.asarray(idx, np.int32)


def _hyena(u3, kern, hy_bias, mats):
    _, NC, B, L2, _ = u3.shape
    C = NC * LANES
    N1 = DFT_N1
    N2 = 4 * L2 // N1
    af = _dft_filter_a(mats["m1f"], kern)
    NB, G, W, _ = af.shape
    kf = _dft_filt(af.reshape(NB, G, N2, W // N2, LANES), mats["m2f"])
    bias = hy_bias.astype(F32)

    def long_conv(z, zslot, gslot, order):
        a = _dft_a_pk(mats["m1x"], z, zslot)
        NCa, P, G, W, _ = a.shape
        b = _dft_mid(a.reshape(NCa, P, G, N2, W // N2, LANES), mats["m2p"], mats["m2tp"], kf, order)
        return _dft_c_pk(mats["m3x"], b.reshape(a.shape), z, zslot, u3, gslot, bias[order][None, :])

    z = long_conv(u3, 2, 0, 0)
    return long_conv(z, 0, 1, 1)


SHIFT_LIMIT = 50.0


def _attn_kernel(lam_ref, q_ref, k_ref, v_ref, g_ref, o_ref, kmax_ref, *, post):
    k = k_ref[0]
    v = v_ref[0]

    @pl.when(pl.program_id(2) == 0)
    def _():
        kmax_ref[...] = jnp.max(jnp.abs(k.astype(F32)), axis=0, keepdims=True)

    q = q_ref[0]
    lane = lax.broadcasted_iota(jnp.int32, q.shape, 1)
    lo = lane < HEAD_DIM
    zero = jnp.zeros_like(q)
    reach = jnp.abs(q.astype(F32)) * kmax_ref[...]
    shifts = [jnp.sum(jnp.where(lo, reach, 0.0), axis=-1, keepdims=True),
              jnp.sum(jnp.where(lo, 0.0, reach), axis=-1, keepdims=True)]
    worst = jnp.max(jnp.maximum(shifts[0], shifts[1]))

    def run(row_max):
        outs = []
        for c in range(2):
            qc = jnp.where(lo, q, zero) if c == 0 else jnp.where(lo, zero, q)
            s = lax.dot_general(qc, k, (((1,), (1,)), ((), ())), preferred_element_type=F32)
            m = jnp.max(s, axis=-1, keepdims=True) if row_max else shifts[c]
            p = jnp.exp2(s - m)
            l = jnp.sum(p, axis=-1, keepdims=True)
            outs.append(jnp.dot(p.astype(BF16), v, preferred_element_type=F32) / l)
        a = outs[0] - lam_ref[0] * outs[1]
        ms = jnp.mean(a * a, axis=-1, keepdims=True)
        o_ref[0] = (a * lax.rsqrt(ms + RMS_EPS) * g_ref[...] * post).astype(o_ref.dtype)

    @pl.when(worst <= SHIFT_LIMIT)
    def _():
        run(False)

    @pl.when(jnp.logical_not(worst <= SHIFT_LIMIT))
    def _():
        run(True)


def _attention(qn, kn, proj3, vcol, lam, subln_g, post):
    B, L, W = qn.shape
    tq = min(512, L)
    return pl.pallas_call(
        functools.partial(_attn_kernel, post=post),
        grid=(B, W // V_DIM, L // tq),
        in_specs=[pl.BlockSpec(memory_space=pltpu.SMEM),
                  pl.BlockSpec((1, tq, V_DIM), lambda b, h, i: (b, i, h)),
                  pl.BlockSpec((1, L, V_DIM), lambda b, h, i: (b, 0, h)),
                  pl.BlockSpec((1, L, V_DIM), lambda b, h, i: (b, 0, vcol + h)),
                  pl.BlockSpec((1, V_DIM), lambda b, h, i: (0, 0))],
        out_specs=pl.BlockSpec((1, tq, V_DIM), lambda b, h, i: (b, i, h)),
        out_shape=jax.ShapeDtypeStruct((B, L, W), BF16),
        scratch_shapes=[pltpu.VMEM((1, V_DIM), F32)],
        compiler_params=_cparams(("parallel", "parallel", "arbitrary")),
        name="diff_attention",
    )(lam.reshape(1).astype(F32), qn, kn, proj3, subln_g.astype(F32).reshape(1, V_DIM))


def _merge_kernel(x_ref, yh_ref, ya_ref, gh_ref, ga_ref, who_ref, wao_ref, wout_ref, g_ref, *rest,
                  with_router):
    if with_router:
        wr_ref, xo_ref, hn_ref, lg_ref = rest
    else:
        xo_ref, hn_ref = rest
    yh = jnp.concatenate([pltpu.bitcast(yh_ref[c], BF16) for c in range(yh_ref.shape[0])], axis=1)
    th = jnp.dot(yh, who_ref[...], preferred_element_type=F32)
    ta = jnp.dot(ya_ref[...], wao_ref[...], preferred_element_type=F32)
    merged = (jax.nn.sigmoid(gh_ref[...].astype(F32)) * th
              + jax.nn.sigmoid(ga_ref[...].astype(F32)) * ta)
    xn = x_ref[...] + jnp.dot(merged.astype(BF16), wout_ref[...], preferred_element_type=F32)
    xo_ref[...] = xn
    ms = jnp.mean(xn * xn, axis=-1, keepdims=True)
    h = xn * lax.rsqrt(ms + RMS_EPS) * g_ref[...]
    hn_ref[...] = h.astype(hn_ref.dtype)
    if with_router:
        lane = lax.broadcasted_iota(jnp.int32, lg_ref.shape, 1)
        lg = jnp.zeros(lg_ref.shape, F32)
        for e in range(wr_ref.shape[0]):
            lg = jnp.where(lane == e, jnp.sum(h * wr_ref[e:e + 1, :], axis=-1, keepdims=True), lg)
        lg_ref[...] = lg


def _merge(x2, yh, ya, proj, gcol, who, wao, wout, g_ffn, w_router):
    T, D = x2.shape
    W = ya.shape[1]
    tm = min(512, T)
    with_router = w_router is not None
    row = lambda w: pl.BlockSpec((tm, w), lambda i: (i, 0))
    full = lambda a: pl.BlockSpec(a.shape, lambda i: (0, 0))
    in_specs = [row(D), pl.BlockSpec((yh.shape[0], tm // 2, LANES), lambda i: (0, i, 0)), row(W),
                pl.BlockSpec((tm, D), lambda i: (i, gcol)),
                pl.BlockSpec((tm, D), lambda i: (i, gcol + 1)),
                full(who), full(wao), full(wout), pl.BlockSpec((1, D), lambda i: (0, 0))]
    args = [x2, yh, ya, proj, proj, who, wao, wout, g_ffn.astype(F32).reshape(1, D)]
    out_specs = [row(D), row(D)]
    out_shape = [jax.ShapeDtypeStruct((T, D), F32),
                 jax.ShapeDtypeStruct((T, D), F32 if with_router else BF16)]
    if with_router:
        wr = w_router.astype(F32).T
        in_specs.append(full(wr))
        args.append(wr)
        out_specs.append(row(LANES))
        out_shape.append(jax.ShapeDtypeStruct((T, LANES), F32))
    return pl.pallas_call(
        functools.partial(_merge_kernel, with_router=with_router),
        grid=(T // tm,),
        in_specs=in_specs,
        out_specs=out_specs,
        out_shape=out_shape,
        compiler_params=_cparams(("parallel",)),
        name="merge_outproj",
    )(*args)


FF_CHUNK = 2 * LANES


def _swiglu(h, wg_ref, wu_ref, wd_ref):
    F = wg_ref.shape[-1]
    acc = None
    for lo in range(0, F, FF_CHUNK):
        sl = slice(lo, min(lo + FF_CHUNK, F))
        g = jnp.dot(h, wg_ref[:, sl], preferred_element_type=F32)
        u = jnp.dot(h, wu_ref[:, sl], preferred_element_type=F32)
        a = (g * jax.nn.sigmoid(g) * u).astype(BF16)
        d = jnp.dot(a, wd_ref[sl, :], preferred_element_type=F32)
        acc = d if acc is None else acc + d
    return acc


def _ffn_kernel(h_ref, x_ref, wg_ref, wu_ref, wd_ref, o_ref):
    o_ref[...] = x_ref[...] + _swiglu(h_ref[...], wg_ref, wu_ref, wd_ref)


def _dense_ffn(hn, x2, wg, wu, wd, layer):
    T, D = x2.shape
    F = wg.shape[2]
    tm = min(512, T)
    row = pl.BlockSpec((tm, D), lambda i: (i, 0))
    return pl.pallas_call(
        _ffn_kernel,
        grid=(T // tm,),
        in_specs=[row, row,
                  pl.BlockSpec((None, D, F), lambda i: (layer, 0, 0)),
                  pl.BlockSpec((None, D, F), lambda i: (layer, 0, 0)),
                  pl.BlockSpec((None, F, D), lambda i: (layer, 0, 0))],
        out_specs=row,
        out_shape=jax.ShapeDtypeStruct((T, D), F32),
        compiler_params=_cparams(("parallel",)),
        name="dense_ffn",
    )(hn, x2, wg, wu, wd)


def _row_copy(src_hbm, row, dst, r, sem):
    return pltpu.make_async_copy(src_hbm.at[pl.ds(row, 1)], dst.at[pl.ds(r, 1)], sem)


def _rows_done(src_hbm, dst, sem):
    return pltpu.make_async_copy(src_hbm.at[pl.ds(0, dst.shape[0])], dst, sem)


def _expert_kernel(be_ref, tok_ref, hn_hbm, wg_ref, wu_ref, wd_ref, o_ref, xbuf, sem):
    i = pl.program_id(0)
    last = pl.num_programs(0) - 1
    rows = xbuf.shape[1]
    slot = i % 2
    nxt = 1 - slot

    @pl.when(i == 0)
    def _():
        def issue(r, carry):
            _row_copy(hn_hbm, tok_ref[r], xbuf.at[0], r, sem.at[0]).start()
            return carry

        lax.fori_loop(0, rows, issue, 0, unroll=8)

    _rows_done(hn_hbm, xbuf.at[slot], sem.at[slot]).wait()
    base = jnp.minimum(i + 1, last) * rows
    for r in range(rows):
        _row_copy(hn_hbm, tok_ref[base + r], xbuf.at[nxt], r, sem.at[nxt]).start()
    o_ref[...] = _swiglu(xbuf[slot].astype(BF16), wg_ref.at[0], wu_ref.at[0], wd_ref.at[0])

    @pl.when(i == last)
    def _():
        _rows_done(hn_hbm, xbuf.at[nxt], sem.at[nxt]).wait()


def _experts(block_e, tok_of_slot, hn, wg, wu, wd, layer):
    T, D = hn.shape
    F = wg.shape[3]
    P = tok_of_slot.shape[0]
    rows = MOE_ROWS
    wspec = lambda shape: pl.BlockSpec((None, 1) + shape, lambda i, be, tok: (layer, be[i], 0, 0))
    return pl.pallas_call(
        _expert_kernel,
        grid_spec=pltpu.PrefetchScalarGridSpec(
            num_scalar_prefetch=2,
            grid=(P // rows,),
            in_specs=[pl.BlockSpec(memory_space=pl.ANY),
                      wspec((D, F)), wspec((D, F)), wspec((F, D))],
            out_specs=pl.BlockSpec((rows, D), lambda i, be, tok: (i, 0)),
            scratch_shapes=[pltpu.VMEM((2, rows, D), F32), pltpu.SemaphoreType.DMA((2,))]),
        out_shape=jax.ShapeDtypeStruct((P, D), F32),
        compiler_params=_cparams(("arbitrary",)),
        name="moe_experts",
    )(block_e, tok_of_slot, hn, wg, wu, wd)


def _combine_kernel(dest_ref, yp_hbm, x_ref, w_ref, o_ref, buf0, buf1, sem):
    i = pl.program_id(0)
    rows = buf0.shape[1]

    def gather(tile, slot):
        base = tile * rows * TOP_K

        for r in range(rows):
            _row_copy(yp_hbm, dest_ref[base + TOP_K * r], buf0.at[slot], r, sem.at[slot]).start()
            _row_copy(yp_hbm, dest_ref[base + TOP_K * r + 1], buf1.at[slot], r, sem.at[slot]).start()

    @pl.when(i == 0)
    def _():
        gather(0, 0)

    @pl.when(i + 1 < pl.num_programs(0))
    def _():
        gather(i + 1, (i + 1) % 2)

    slot = i % 2
    _rows_done(yp_hbm, buf0.at[slot], sem.at[slot]).wait()
    _rows_done(yp_hbm, buf1.at[slot], sem.at[slot]).wait()
    w = w_ref[...]
    o_ref[...] = x_ref[...] + (buf0[slot] * w[:, 0:1] + buf1[slot] * w[:, 1:2])


def _combine(dest, yp, x2, top_w):
    T, D = x2.shape
    tm = min(256, T)
    return pl.pallas_call(
        _combine_kernel,
        grid_spec=pltpu.PrefetchScalarGridSpec(
            num_scalar_prefetch=1,
            grid=(T // tm,),
            in_specs=[pl.BlockSpec(memory_space=pl.ANY),
                      pl.BlockSpec((tm, D), lambda i, d: (i, 0)),
                      pl.BlockSpec((tm, TOP_K), lambda i, d: (i, 0))],
            out_specs=pl.BlockSpec((tm, D), lambda i, d: (i, 0)),
            scratch_shapes=[pltpu.VMEM((2, tm, D), F32), pltpu.VMEM((2, tm, D), F32),
                            pltpu.SemaphoreType.DMA((2,))]),
        out_shape=jax.ShapeDtypeStruct((T, D), F32),
        compiler_params=_cparams(("arbitrary",)),
        name="moe_combine",
    )(dest, yp, x2, top_w)


def _route(logits, n_experts, rows):
    T = logits.shape[0]
    top_val, top_idx = lax.top_k(logits[:, :n_experts], TOP_K)
    top_w = jax.nn.softmax(top_val, axis=-1)
    A = T * TOP_K
    e_flat = top_idx.reshape(A).astype(jnp.int32)
    onehot = (e_flat[:, None] == jnp.arange(n_experts, dtype=jnp.int32)[None, :]).astype(jnp.int32)
    csum = jnp.cumsum(onehot, axis=0)
    counts = csum[-1]
    rank = jnp.sum((csum - onehot) * onehot, axis=1)
    padded = (counts + rows - 1) // rows * rows
    pad_end = jnp.cumsum(padded)
    pad_start = pad_end - padded
    dest = pad_start[e_flat] + rank
    n_blocks = -(-A // rows) + n_experts
    P = n_blocks * rows
    order = jnp.argsort(e_flat)
    starts = jnp.cumsum(counts) - counts
    slot = jnp.arange(P, dtype=jnp.int32)
    slot_e = jnp.minimum(jnp.sum((slot[:, None] >= pad_end[None, :]).astype(jnp.int32), axis=1),
                         n_experts - 1)
    within = slot - pad_start[slot_e]
    valid = within < counts[slot_e]
    src = jnp.clip(starts[slot_e] + within, 0, A - 1)
    tok_of_slot = jnp.where(valid, order[src] // TOP_K, 0).astype(jnp.int32)
    block_e = slot_e[::rows]
    return top_w.astype(F32), dest.astype(jnp.int32), tok_of_slot, block_e


def _cast_cols_kernel(w_ref, o_ref):
    F = w_ref.shape[-1]
    o_ref[0, 0] = jnp.zeros(o_ref.shape[2:], o_ref.dtype)
    o_ref[0, 0, :, :F] = w_ref[0, 0].astype(o_ref.dtype)


def _cast_rows_kernel(w_ref, o_ref):
    F = w_ref.shape[2]
    o_ref[0, 0, :F, :] = w_ref[0, 0].astype(o_ref.dtype)
    o_ref[0, 0, F:, :] = jnp.zeros((o_ref.shape[2] - F, o_ref.shape[3]), o_ref.dtype)


def _expert_weights_bf16(w, axis):
    n, E = w.shape[:2]
    F = w.shape[axis]
    Fp = -(-F // FF_CHUNK) * FF_CHUNK
    t = FF_CHUNK
    if axis == 3:
        D = w.shape[2]
        body, grid_d = _cast_cols_kernel, D // t
        blk = lambda f: pl.BlockSpec((1, 1, t, f), lambda a, e, i: (a, e, i, 0))
        shape = (n, E, D, Fp)
    else:
        D = w.shape[3]
        body, grid_d = _cast_rows_kernel, D // t
        blk = lambda f: pl.BlockSpec((1, 1, f, t), lambda a, e, i: (a, e, 0, i))
        shape = (n, E, Fp, D)
    return pl.pallas_call(
        body,
        grid=(n, E, grid_d),
        in_specs=[blk(F)],
        out_specs=blk(Fp),
        out_shape=jax.ShapeDtypeStruct(shape, BF16),
        compiler_params=_cparams(("parallel", "parallel", "parallel")),
        name="expert_weights_bf16",
    )(w)


def kernel(x, norm_mix, w_in, conv_w, w_f1, b_f1, freq1, w_f2, b_f2, freq2, w_f3, hy_bias, q_norm, k_norm, lam_q1, lam_k1, lam_q2, lam_k2, subln_g, w_hy_o, w_at_o, w_out, norm_ffn, w_dense_gate, w_dense_up, w_dense_down, w_router, w_moe_gate, w_moe_up, w_moe_down):
    B, L, D = x.shape
    T = B * L
    depth = w_in.shape[0]
    hw = w_hy_o.shape[1]
    qk_w = N_HEADS * 2 * HEAD_DIM
    at_w = N_HEADS * V_DIM
    mats = _dft_matrices(L)
    moe_w = (_expert_weights_bf16(w_moe_gate, 3), _expert_weights_bf16(w_moe_up, 3),
             _expert_weights_bf16(w_moe_down, 2))
    dense_w = tuple(w.astype(BF16) for w in (w_dense_gate, w_dense_up, w_dense_down))

    x2 = x.reshape(T, D)
    for layer in range(depth):
        lambda_init = 0.8 - 0.6 * math.exp(-0.3 * layer)
        u3, qn, kn, v, gates = _inproj(x2, L, norm_mix[layer].astype(F32), w_in[layer].astype(BF16),
                                       conv_w[layer].astype(F32), q_norm[layer], k_norm[layer], hw, qk_w, at_w)
        kern = _hyena_time_filters(L, w_f1[layer], b_f1[layer], freq1[layer], w_f2[layer], b_f2[layer],
                                   freq2[layer], w_f3[layer], hw)
        y_h = _hyena(u3.reshape(3, hw // LANES, B, L // 2, LANES), kern, hy_bias[layer], mats)
        lam = (jnp.exp(jnp.sum(lam_q1[layer].astype(F32) * lam_k1[layer].astype(F32)))
               - jnp.exp(jnp.sum(lam_q2[layer].astype(F32) * lam_k2[layer].astype(F32))) + lambda_init)
        y_a = _attention(qn.reshape(B, L, qk_w), kn.reshape(B, L, qk_w), v.reshape(B, L, at_w), 0, lam,
                         subln_g[layer], 1.0 - lambda_init)
        i = layer // 2
        moe = layer % 2 == 1
        outs = _merge(x2, y_h.reshape(hw // LANES, T // 2, LANES), y_a.reshape(T, at_w), gates, 0,
                      w_hy_o[layer].astype(BF16), w_at_o[layer].astype(BF16), w_out[layer].astype(BF16),
                      norm_ffn[layer], w_router[i] if moe else None)
        if not moe:
            x2, hn = outs
            x2 = _dense_ffn(hn, x2, *dense_w, i)
        else:
            x2, hn, logits = outs
            n_experts = w_router.shape[2]
            top_w, dest, tok_of_slot, block_e = _route(logits, n_experts, MOE_ROWS)
            yp = _experts(block_e, tok_of_slot, hn, *moe_w, i)
            x2 = _combine(dest, yp, x2, top_w)
    return x2.reshape(B, L, D)
```

```python
import functools
import math

import numpy as np
import jax
import jax.numpy as jnp
from jax import lax
from jax.experimental import pallas as pl
from jax.experimental.pallas import tpu as pltpu

F32 = jnp.float32
BF16 = jnp.bfloat16
U32 = jnp.uint32

N_HEADS = 4
HEAD_DIM = 64
V_DIM = 2 * HEAD_DIM
ROPE_THETA = 10000.0
FILTER_BANDS = 16
HYENA_TARGET = 1e-2
FAST_DECAY_PCT = 0.3
SLOW_DECAY_PCT = 1.5
TOP_K = 2
RMS_EPS = 1e-6

LANES = 128
BF16_SUBLANES = 16
VMEM_LIMIT = 56 * 1024 * 1024

DFT_N1 = 128
DFT_PAIRS = 16
DFT_FILTER_STEP = 16
DFT_MID_K2 = 4
DFT_LANE_BLOCKS = 2
MOE_ROWS = 512


def _cparams(sem):
    return pltpu.CompilerParams(dimension_semantics=sem, vmem_limit_bytes=VMEM_LIMIT)


def _inproj_kernel(x_ref, xp_ref, xn_ref, g_ref, w_ref, cw_ref, cos_ref, sin_ref, qg_ref, kg_ref, grp_ref,
                   u_ref, q_ref, k_ref, v_ref, gate_ref, *, hw, qk_w, tiles_per_seq):
    pos = pl.program_id(0) % tiles_per_seq
    tm = x_ref.shape[0]
    halo = xp_ref.shape[0]

    def normed(x):
        ms = jnp.mean(x * x, axis=-1, keepdims=True)
        return (x * lax.rsqrt(ms + RMS_EPS) * g_ref[...]).astype(BF16)

    hn = normed(x_ref[...])
    hext = jnp.concatenate([normed(xp_ref[...]), hn, normed(xn_ref[...])], axis=0)
    row = lax.broadcasted_iota(jnp.int32, (tm + 2 * halo, hw), 0)
    outside = jnp.logical_or(jnp.logical_and(row < halo, pos == 0),
                             jnp.logical_and(row >= tm + halo, pos == tiles_per_seq - 1))
    hy_all = jnp.dot(hext, w_ref[:, :3 * hw], preferred_element_type=F32)
    rest = jnp.dot(hn, w_ref[:, 3 * hw:], preferred_element_type=F32)

    nc = u_ref.shape[1]
    for s in range(u_ref.shape[0]):
        cols = slice(s * hw, (s + 1) * hw)
        hy = jnp.where(outside, 0.0, hy_all[:, cols])
        up = pltpu.roll(hy, 1, 0)[halo:halo + tm]
        dn = pltpu.roll(hy, tm + 2 * halo - 1, 0)[halo:halo + tm]
        cw = cw_ref[:, cols]
        res = (up * cw[0:1] + hy[halo:halo + tm] * cw[1:2] + dn * cw[2:3]).astype(BF16)
        pk = pltpu.bitcast(res, U32)
        for c in range(nc):
            u_ref[s, c] = pk[:, c * LANES:(c + 1) * LANES]

    cos = cos_ref[...]
    sin = sin_ref[...]
    lane = lax.broadcasted_iota(jnp.int32, cos.shape, 1)
    first = (lane % HEAD_DIM) < (HEAD_DIM // 2)
    grp = grp_ref[...]
    base = 0
    for o_ref, gain_ref in ((q_ref, qg_ref), (k_ref, kg_ref)):
        for h in range(qk_w // LANES):
            x = rest[:, base + h * LANES:base + (h + 1) * LANES]
            sq_hi, sq_lo = _split_bf16(x * x)
            ss = (jnp.dot(sq_hi, grp, preferred_element_type=F32)
                  + jnp.dot(sq_lo, grp, preferred_element_type=F32))
            xn = x * lax.rsqrt(ss * (1.0 / HEAD_DIM) + RMS_EPS) * gain_ref[...]
            partner = jnp.where(first, pltpu.roll(xn, LANES - HEAD_DIM // 2, 1),
                                pltpu.roll(xn, HEAD_DIM // 2, 1))
            o_ref[:, h * LANES:(h + 1) * LANES] = (xn * cos + partner * sin).astype(o_ref.dtype)
        base += qk_w
    v_ref[...] = rest[:, base:base + v_ref.shape[1]].astype(v_ref.dtype)
    gate_ref[...] = rest[:, base + v_ref.shape[1]:].astype(gate_ref.dtype)


def _inproj(x2, L, g, w, conv_w, q_norm, k_norm, hw, qk_w, at_w):
    T, D = x2.shape
    tm = min(512, L)
    halo = BF16_SUBLANES
    hb = tm // halo
    nhb = T // halo
    nc = hw // LANES
    gw = w.shape[1] - 3 * hw - 2 * qk_w - at_w
    cos_t, sin_t = _rope_tables(L)
    qg = jnp.tile(q_norm.astype(F32), LANES // HEAD_DIM)[None, :] * F32(HEAD_DIM ** -0.5 * math.log2(math.e))
    kg = jnp.tile(k_norm.astype(F32), LANES // HEAD_DIM)[None, :]
    lanes = np.arange(LANES)
    grp = jnp.asarray(lanes[:, None] // HEAD_DIM == lanes[None, :] // HEAD_DIM, BF16)
    tps = L // tm
    const = lambda a: pl.BlockSpec(a.shape, lambda i: (0,) * a.ndim)
    rows = lambda width: pl.BlockSpec((tm, width), lambda i: (i, 0))
    tab = pl.BlockSpec((tm, LANES), lambda i: (i % tps, 0))
    gr = g.reshape(1, D)
    return pl.pallas_call(
        functools.partial(_inproj_kernel, hw=hw, qk_w=qk_w, tiles_per_seq=tps),
        grid=(T // tm,),
        in_specs=[rows(D),
                  pl.BlockSpec((halo, D), lambda i: (jnp.maximum(i * hb - 1, 0), 0)),
                  pl.BlockSpec((halo, D), lambda i: (jnp.minimum((i + 1) * hb, nhb - 1), 0)),
                  const(gr), const(w), const(conv_w), tab, tab, const(qg), const(kg), const(grp)],
        out_specs=[pl.BlockSpec((3, nc, tm // 2, LANES), lambda i: (0, 0, i, 0)),
                   rows(qk_w), rows(qk_w), rows(at_w), rows(gw)],
        out_shape=[jax.ShapeDtypeStruct((3, nc, T // 2, LANES), U32),
                   jax.ShapeDtypeStruct((T, qk_w), BF16), jax.ShapeDtypeStruct((T, qk_w), BF16),
                   jax.ShapeDtypeStruct((T, at_w), BF16), jax.ShapeDtypeStruct((T, gw), BF16)],
        compiler_params=_cparams(("parallel",)),
        name="inproj",
    )(x2, x2, x2, gr, w, conv_w, cos_t, sin_t, qg, kg, grp)


def _rope_tables(L):
    half = HEAD_DIM // 2
    inv = 1.0 / (ROPE_THETA ** (np.arange(0, HEAD_DIM, 2, dtype=np.float64) / HEAD_DIM))
    ang = np.arange(L, dtype=np.float64)[:, None] * inv[None, :]
    cos, sin = np.cos(ang), np.sin(ang)
    cos_t = np.tile(cos, (1, LANES // half))
    sin_t = np.tile(np.concatenate([-sin, sin], axis=1), (1, LANES // HEAD_DIM))
    return jnp.asarray(cos_t, F32), jnp.asarray(sin_t, F32)


def _split_bf16(x):
    hi = x.astype(BF16)
    return hi, (x - hi.astype(F32)).astype(BF16)


def _dot3(a, b):
    ah, al = _split_bf16(a)
    bh, bl = _split_bf16(b)
    d = lambda x, y: jnp.dot(x, y, preferred_element_type=F32)
    return d(ah, bh) + (d(ah, bl) + d(al, bh))


def _filter_kernel(zz_ref, w1_ref, b1_ref, f1_ref, w2_ref, b2_ref, f2_ref, w3_ref, dl_ref, o_ref):
    zz = zz_ref[...]
    h = _dot3(zz, w1_ref[...])
    h = jnp.sin(f1_ref[...] * (h + b1_ref[...]))
    h = _dot3(h, w2_ref[...])
    h = jnp.sin(f2_ref[...] * (h + b2_ref[...]))
    hf = _dot3(h, w3_ref[0])
    t = zz[:, 0:1]
    mask = zz[:, _MASK_COL:_MASK_COL + 1]
    kern = hf * (jnp.exp(-t * dl_ref[...]) * mask)
    for c in range(o_ref.shape[0]):
        o_ref[c] = kern[:, c * LANES:(c + 1) * LANES]


_MASK_COL = 2 * FILTER_BANDS + 1


def _filter_positions(L):
    bands = FILTER_BANDS
    t = np.linspace(0.0, 1.0, L, dtype=np.float64)[:, None]
    w = 2.0 * math.pi * np.arange(L, dtype=np.float64)[:, None] / L
    f = np.linspace(1e-4, bands - 1, bands, dtype=np.float64)[None, :]
    z = np.concatenate([t, np.cos(f * w), -np.sin(f * w)], axis=-1)
    zz = np.zeros((2 * L, LANES), np.float64)
    zz[:L, :z.shape[1]] = z
    zz[L + 1:, :z.shape[1]] = z[1:][::-1]
    zz[:, _MASK_COL] = 1.0
    zz[L, :] = 0.0
    return zz.astype(np.float32)


def _hyena_time_filters(L, w_f1, b_f1, freq1, w_f2, b_f2, freq2, w_f3, hw):
    emb, hid = w_f1.shape
    zz = jnp.asarray(_filter_positions(L))
    w1 = jnp.zeros((LANES, hid), F32).at[:emb].set(w_f1)
    w3 = w_f3.reshape(hid, 2, 2, hw).transpose(2, 0, 1, 3).reshape(2, hid, 2 * hw)
    max_decay = math.log(HYENA_TARGET) / FAST_DECAY_PCT
    min_decay = math.log(HYENA_TARGET) / SLOW_DECAY_PCT
    deltas = np.abs(np.linspace(min_decay, max_decay, hw, dtype=np.float64))
    dl = jnp.asarray(np.tile(deltas, 2)[None, :], F32)
    tr = min(512, L)
    per_dir = L // tr
    row = lambda a: a.reshape(1, -1)
    return pl.pallas_call(
        _filter_kernel,
        grid=(2 * L // tr,),
        in_specs=[pl.BlockSpec((tr, LANES), lambda i: (i, 0)),
                  pl.BlockSpec((LANES, hid), lambda i: (0, 0)),
                  pl.BlockSpec((1, hid), lambda i: (0, 0)),
                  pl.BlockSpec((1, hid), lambda i: (0, 0)),
                  pl.BlockSpec((hid, hid), lambda i: (0, 0)),
                  pl.BlockSpec((1, hid), lambda i: (0, 0)),
                  pl.BlockSpec((1, hid), lambda i: (0, 0)),
                  pl.BlockSpec((1, hid, 2 * hw), lambda i: (i // per_dir, 0, 0)),
                  pl.BlockSpec((1, 2 * hw), lambda i: (0, 0))],
        out_specs=pl.BlockSpec((2 * hw // LANES, tr, LANES), lambda i: (0, i, 0)),
        out_shape=jax.ShapeDtypeStruct((2 * hw // LANES, 2 * L, LANES), F32),
        compiler_params=_cparams(("parallel",)),
        name="hyena_filter",
    )(zz, w1, row(b_f1), row(freq1), w_f2, row(b_f2), row(freq2), w3, dl)


def _dft_matrices(L):
    N = 2 * L
    N1 = DFT_N1
    N2 = N // N1
    h = N2 // 2
    ang = 2.0 * np.pi * np.outer(np.arange(N2), np.arange(N2)) / N2
    c, s = np.cos(ang), np.sin(ang)
    m1 = np.zeros((N2, 2, N2))
    m1[:, 0, :h], m1[:, 0, h:] = c[:, :h], s[:, :h]
    m1[:, 1, :h], m1[:, 1, h:] = -s[:, :h], c[:, :h]
    m1 = m1.reshape(2 * N2, N2)
    m1f = np.stack([c, -s], axis=1).reshape(2 * N2, N2)
    m3 = np.zeros((2, h, N2, 2))
    m3[0, :, :, 0], m3[0, :, :, 1] = c[:h], -s[:h]
    m3[1, :, :, 0], m3[1, :, :, 1] = s[:h], c[:h]
    m3 = (m3 / N).reshape(N2, 2 * N2)
    def mid(order, transposed):
        f = np.arange(2 * N1)[:, None]
        d = np.asarray(order)[None, :]
        a = 2.0 * np.pi * ((d % N1) * (f % N1) % N1) / N1
        same, up = (f // N1 == d // N1), (f // N1 < d // N1)
        pc = np.where(same, 1.0, 0.0)
        ps = np.where(same, 0.0, np.where(up, 1.0, -1.0))
        u = pc * np.cos(a) + ps * np.sin(a)
        v = ps * np.cos(a) - pc * np.sin(a)
        b = 2.0 * np.pi * (np.arange(N2)[:, None] * (d % N1) % N) / N
        if transposed:
            u, v = u.T, v.T
            cb, sb = jnp.asarray(np.cos(b), F32)[:, :, None], jnp.asarray(np.sin(b), F32)[:, :, None]
        else:
            cb, sb = jnp.asarray(np.cos(b), F32)[:, None, :], jnp.asarray(np.sin(b), F32)[:, None, :]
        return jnp.asarray(u, F32)[None] * cb + jnp.asarray(v, F32)[None] * sb

    packed = _mid_row_order(DFT_PAIRS, 2)
    eye2 = np.eye(2)
    return dict(m1x=jnp.asarray(np.kron(m1, eye2), BF16),
                m3x=jnp.asarray(np.kron(m3, eye2), BF16),
                m1f=jnp.asarray(m1f, F32),
                m2f=mid(_mid_row_order(DFT_FILTER_STEP, 1), False),
                m2p=mid(packed, False).astype(BF16),
                m2tp=mid(packed, True).astype(BF16))


def _dft_filter_a_kernel(m_ref, k_ref, o_ref, *, step, n2):
    j0 = pl.program_id(1) * step
    rows = m_ref.shape[0]
    for jj in range(step):
        rhs = jnp.concatenate([k_ref[c, pl.ds(j0 + jj, n2, stride=DFT_N1), :] for c in range(k_ref.shape[0])],
                              axis=1)
        res = _dot3(m_ref[...], rhs)
        for c in range(o_ref.shape[0]):
            o_ref[c, 0, pl.ds(jj, rows, stride=step), :] = res[:, c * LANES:(c + 1) * LANES]


def _dft_filter_a(m1f, kern):
    NB, N, _ = kern.shape
    n2 = N // DFT_N1
    step, ncb = DFT_FILTER_STEP, DFT_LANE_BLOCKS
    rows = m1f.shape[0]
    return pl.pallas_call(
        functools.partial(_dft_filter_a_kernel, step=step, n2=n2),
        grid=(NB // ncb, DFT_N1 // step),
        in_specs=[pl.BlockSpec(m1f.shape, lambda h, j: (0, 0)),
                  pl.BlockSpec((ncb, N, LANES), lambda h, j: (h, 0, 0))],
        out_specs=pl.BlockSpec((ncb, 1, rows * step, LANES), lambda h, j: (h, j, 0, 0)),
        out_shape=jax.ShapeDtypeStruct((NB, DFT_N1 // step, rows * step, LANES), F32),
        compiler_params=_cparams(("parallel", "arbitrary")),
        name="dft_filter_a",
    )(m1f, kern)


def _dft_filt_kernel(a_ref, m_ref, o_ref):
    rows = m_ref.shape[1]
    a = jnp.concatenate([a_ref[c, :, 0].reshape(rows, LANES) for c in range(a_ref.shape[0])], axis=1)
    o_ref[0] = _dot3(m_ref[0], a)


def _dft_filt(a, m2):
    NB, G, N2, W, _ = a.shape
    R = m2.shape[1]
    return pl.pallas_call(
        _dft_filt_kernel,
        grid=(N2,),
        in_specs=[pl.BlockSpec((NB, G, 1, W, LANES), lambda k: (0, 0, k, 0, 0)),
                  pl.BlockSpec((1, R, R), lambda k: (k, 0, 0))],
        out_specs=pl.BlockSpec((1, R, NB * LANES), lambda k: (k, 0, 0)),
        out_shape=jax.ShapeDtypeStruct((N2, R, NB * LANES), F32),
        compiler_params=_cparams(("parallel",)),
        name="dft_filter_mid",
    )(a, m2)


def _packed_rows(ref, lead, start, size, stride):
    cols = [jnp.concatenate([ref[(c,) + l + (pl.ds(start, size, stride=stride), slice(None))] for l in lead], axis=0)
            for c in range(ref.shape[0])]
    return pltpu.bitcast(jnp.concatenate(cols, axis=1), BF16)


def _dft_a_pk_kernel(m_ref, z_ref, o_ref, *, tm, nh, npairs):
    j0 = pl.program_id(2) * tm
    rows = m_ref.shape[0] // 2
    for j in range(tm):
        rhs = _packed_rows(z_ref, [(0,), (1,)], j0 + j, nh, npairs)
        res = jnp.dot(m_ref[...], rhs, preferred_element_type=F32)
        pk = pltpu.bitcast(res.astype(BF16), U32)
        for c in range(o_ref.shape[0]):
            o_ref[c, 0, 0, pl.ds(j, rows, stride=tm), :] = pk[:, c * LANES:(c + 1) * LANES]


def _dft_a_pk(m1x, z, slot):
    _, NC, B, L2, _ = z.shape
    npairs = DFT_N1 // 2
    nh = L2 // npairs
    tm, ncb = DFT_PAIRS, DFT_LANE_BLOCKS
    rows = m1x.shape[0] // 2
    P = B // 2
    return pl.pallas_call(
        functools.partial(_dft_a_pk_kernel, tm=tm, nh=nh, npairs=npairs),
        grid=(P, NC // ncb, npairs // tm),
        in_specs=[pl.BlockSpec(m1x.shape, lambda p, h, j: (0, 0)),
                  pl.BlockSpec((None, ncb, 2, L2, LANES), lambda p, h, j: (slot, h, p, 0, 0))],
        out_specs=pl.BlockSpec((ncb, 1, 1, rows * tm, LANES), lambda p, h, j: (h, p, j, 0, 0)),
        out_shape=jax.ShapeDtypeStruct((NC, P, npairs // tm, rows * tm, LANES), U32),
        compiler_params=_cparams(("parallel", "parallel", "arbitrary")),
        name="dft_a",
    )(m1x, z)


def _dft_mid_kernel(a_ref, m_ref, mt_ref, kf_ref, o_ref):
    n1 = m_ref.shape[1] // 2
    nc, npair = a_ref.shape[0], a_ref.shape[1]
    blk = a_ref.shape[2], a_ref.shape[4], LANES
    for kk in range(a_ref.shape[3]):
        m = m_ref[kk]
        mt = mt_ref[kk]
        kr = kf_ref[kk, :n1]
        ki = kf_ref[kk, n1:]
        for p in range(npair):
            cols = [a_ref[c, p, :, kk].reshape(n1, LANES) for c in range(nc)]
            a = pltpu.bitcast(jnp.concatenate(cols, axis=1), BF16)
            x = jnp.dot(m, a, preferred_element_type=F32)
            xr = x[:n1]
            xi = x[n1:]
            y = jnp.concatenate([xr * kr - xi * ki, xr * ki + xi * kr], axis=0).astype(BF16)
            pk = pltpu.bitcast(jnp.dot(mt, y, preferred_element_type=F32).astype(BF16), U32)
            for c in range(nc):
                o_ref[c, p, :, kk] = pk[:, c * LANES:(c + 1) * LANES].reshape(blk)


def _dft_mid(a, m2, m2t, kf, order):
    NC, P, G, N2, W, _ = a.shape
    R = m2.shape[1]
    C = NC * LANES
    kb = DFT_MID_K2
    blk = pl.BlockSpec((NC, P, G, kb, W, LANES), lambda k: (0, 0, 0, k, 0, 0))
    return pl.pallas_call(
        _dft_mid_kernel,
        grid=(N2 // kb,),
        in_specs=[blk,
                  pl.BlockSpec((kb, R, R), lambda k: (k, 0, 0)),
                  pl.BlockSpec((kb, R, R), lambda k: (k, 0, 0)),
                  pl.BlockSpec((kb, R, C), lambda k: (k, 0, order))],
        out_specs=blk,
        out_shape=jax.ShapeDtypeStruct(a.shape, U32),
        compiler_params=_cparams(("parallel",)),
        name="dft_mid",
    )(a, m2, m2t, kf)


def _dft_c_pk_kernel(m_ref, b_ref, z_ref, g_ref, bias_ref, o_ref, *, tm, nh, npairs):
    j0 = pl.program_id(2) * tm
    rows = b_ref.shape[3] // tm
    bias = bias_ref[...]
    both = [(0,), (1,)]
    for j in range(tm):
        rhs = _packed_rows(b_ref, [(0, 0)], j, rows, tm)
        y = jnp.dot(m_ref[...], rhs, preferred_element_type=F32)
        z = _packed_rows(z_ref, both, j0 + j, nh, npairs).astype(F32)
        g = _packed_rows(g_ref, both, j0 + j, nh, npairs).astype(F32)
        pk = pltpu.bitcast((g * (y + z * bias)).astype(BF16), U32)
        for c in range(o_ref.shape[0]):
            for b in range(2):
                o_ref[c, b, pl.ds(j0 + j, nh, stride=npairs), :] = pk[b * nh:(b + 1) * nh, c * LANES:(c + 1) * LANES]


def _dft_c_pk(m3x, b, z, zslot, g, gslot, bias):
    NC, P, G, W, _ = b.shape
    L2 = z.shape[3]
    npairs = DFT_N1 // 2
    nh = L2 // npairs
    tm, ncb = DFT_PAIRS, DFT_LANE_BLOCKS
    seq = lambda slot: pl.BlockSpec((None, ncb, 2, L2, LANES), lambda p, h, j: (slot, h, p, 0, 0))
    return pl.pallas_call(
        functools.partial(_dft_c_pk_kernel, tm=tm, nh=nh, npairs=npairs),
        grid=(P, NC // ncb, G),
        in_specs=[pl.BlockSpec(m3x.shape, lambda p, h, j: (0, 0)),
                  pl.BlockSpec((ncb, 1, 1, W, LANES), lambda p, h, j: (h, p, j, 0, 0)),
                  seq(zslot), seq(gslot),
                  pl.BlockSpec((1, ncb * LANES), lambda p, h, j: (0, h))],
        out_specs=seq(0),
        out_shape=jax.ShapeDtypeStruct((1, NC, 2 * P, L2, LANES), U32),
        compiler_params=_cparams(("parallel", "parallel", "arbitrary")),
        name="dft_c",
    )(m3x, b, z, g, bias)


def _mid_row_order(step, per):
    N1 = DFT_N1
    idx = []
    for mt in range(N1 // per // step):
        for ri in range(2):
            for jj in range(step):
                for sub in range(per):
                    idx.append(ri * N1 + per * (mt * step + jj) + sub)
    return np.asarray(idx, np.int32)


def _hyena(u3, kern, hy_bias, mats):
    _, NC, B, L2, _ = u3.shape
    C = NC * LANES
    N1 = DFT_N1
    N2 = 4 * L2 // N1
    af = _dft_filter_a(mats["m1f"], kern)
    NB, G, W, _ = af.shape
    kf = _dft_filt(af.reshape(NB, G, N2, W // N2, LANES), mats["m2f"])
    bias = hy_bias.astype(F32)

    def long_conv(z, zslot, gslot, order):
        a = _dft_a_pk(mats["m1x"], z, zslot)
        NCa, P, G, W, _ = a.shape
        b = _dft_mid(a.reshape(NCa, P, G, N2, W // N2, LANES), mats["m2p"], mats["m2tp"], kf, order)
        return _dft_c_pk(mats["m3x"], b.reshape(a.shape), z, zslot, u3, gslot, bias[order][None, :])

    z = long_conv(u3, 2, 0, 0)
    return long_conv(z, 0, 1, 1)


SHIFT_LIMIT = 50.0


def _attn_kernel(lam_ref, q_ref, k_ref, v_ref, g_ref, o_ref, kmax_ref, *, post):
    k = k_ref[0]
    v = v_ref[0]

    @pl.when(pl.program_id(2) == 0)
    def _():
        kmax_ref[...] = jnp.max(jnp.abs(k.astype(F32)), axis=0, keepdims=True)

    q = q_ref[0]
    lane = lax.broadcasted_iota(jnp.int32, q.shape, 1)
    lo = lane < HEAD_DIM
    zero = jnp.zeros_like(q)
    reach = jnp.abs(q.astype(F32)) * kmax_ref[...]
    shifts = [jnp.sum(jnp.where(lo, reach, 0.0), axis=-1, keepdims=True),
              jnp.sum(jnp.where(lo, 0.0, reach), axis=-1, keepdims=True)]
    worst = jnp.max(jnp.maximum(shifts[0], shifts[1]))

    def run(row_max):
        outs = []
        for c in range(2):
            qc = jnp.where(lo, q, zero) if c == 0 else jnp.where(lo, zero, q)
            s = lax.dot_general(qc, k, (((1,), (1,)), ((), ())), preferred_element_type=F32)
            m = jnp.max(s, axis=-1, keepdims=True) if row_max else shifts[c]
            p = jnp.exp2(s - m)
            l = jnp.sum(p, axis=-1, keepdims=True)
            outs.append(jnp.dot(p.astype(BF16), v, preferred_element_type=F32) / l)
        a = outs[0] - lam_ref[0] * outs[1]
        ms = jnp.mean(a * a, axis=-1, keepdims=True)
        o_ref[0] = (a * lax.rsqrt(ms + RMS_EPS) * g_ref[...] * post).astype(o_ref.dtype)

    @pl.when(worst <= SHIFT_LIMIT)
    def _():
        run(False)

    @pl.when(jnp.logical_not(worst <= SHIFT_LIMIT))
    def _():
        run(True)


def _attention(qn, kn, proj3, vcol, lam, subln_g, post):
    B, L, W = qn.shape
    tq = min(512, L)
    return pl.pallas_call(
        functools.partial(_attn_kernel, post=post),
        grid=(B, W // V_DIM, L // tq),
        in_specs=[pl.BlockSpec(memory_space=pltpu.SMEM),
                  pl.BlockSpec((1, tq, V_DIM), lambda b, h, i: (b, i, h)),
                  pl.BlockSpec((1, L, V_DIM), lambda b, h, i: (b, 0, h)),
                  pl.BlockSpec((1, L, V_DIM), lambda b, h, i: (b, 0, vcol + h)),
                  pl.BlockSpec((1, V_DIM), lambda b, h, i: (0, 0))],
        out_specs=pl.BlockSpec((1, tq, V_DIM), lambda b, h, i: (b, i, h)),
        out_shape=jax.ShapeDtypeStruct((B, L, W), BF16),
        scratch_shapes=[pltpu.VMEM((1, V_DIM), F32)],
        compiler_params=_cparams(("parallel", "parallel", "arbitrary")),
        name="diff_attention",
    )(lam.reshape(1).astype(F32), qn, kn, proj3, subln_g.astype(F32).reshape(1, V_DIM))


def _merge_kernel(x_ref, yh_ref, ya_ref, gh_ref, ga_ref, who_ref, wao_ref, wout_ref, g_ref, *rest,
                  with_router):
    if with_router:
        wr_ref, xo_ref, hn_ref, lg_ref = rest
    else:
        xo_ref, hn_ref = rest
    yh = jnp.concatenate([pltpu.bitcast(yh_ref[c], BF16) for c in range(yh_ref.shape[0])], axis=1)
    th = jnp.dot(yh, who_ref[...], preferred_element_type=F32)
    ta = jnp.dot(ya_ref[...], wao_ref[...], preferred_element_type=F32)
    merged = (jax.nn.sigmoid(gh_ref[...].astype(F32)) * th
              + jax.nn.sigmoid(ga_ref[...].astype(F32)) * ta)
    xn = x_ref[...] + jnp.dot(merged.astype(BF16), wout_ref[...], preferred_element_type=F32)
    xo_ref[...] = xn
    ms = jnp.mean(xn * xn, axis=-1, keepdims=True)
    h = xn * lax.rsqrt(ms + RMS_EPS) * g_ref[...]
    hn_ref[...] = h.astype(hn_ref.dtype)
    if with_router:
        lane = lax.broadcasted_iota(jnp.int32, lg_ref.shape, 1)
        lg = jnp.zeros(lg_ref.shape, F32)
        for e in range(wr_ref.shape[0]):
            lg = jnp.where(lane == e, jnp.sum(h * wr_ref[e:e + 1, :], axis=-1, keepdims=True), lg)
        lg_ref[...] = lg


def _merge(x2, yh, ya, proj, gcol, who, wao, wout, g_ffn, w_router):
    T, D = x2.shape
    W = ya.shape[1]
    tm = min(512, T)
    with_router = w_router is not None
    row = lambda w: pl.BlockSpec((tm, w), lambda i: (i, 0))
    full = lambda a: pl.BlockSpec(a.shape, lambda i: (0, 0))
    in_specs = [row(D), pl.BlockSpec((yh.shape[0], tm // 2, LANES), lambda i: (0, i, 0)), row(W),
                pl.BlockSpec((tm, D), lambda i: (i, gcol)),
                pl.BlockSpec((tm, D), lambda i: (i, gcol + 1)),
                full(who), full(wao), full(wout), pl.BlockSpec((1, D), lambda i: (0, 0))]
    args = [x2, yh, ya, proj, proj, who, wao, wout, g_ffn.astype(F32).reshape(1, D)]
    out_specs = [row(D), row(D)]
    out_shape = [jax.ShapeDtypeStruct((T, D), F32),
                 jax.ShapeDtypeStruct((T, D), F32 if with_router else BF16)]
    if with_router:
        wr = w_router.astype(F32).T
        in_specs.append(full(wr))
        args.append(wr)
        out_specs.append(row(LANES))
        out_shape.append(jax.ShapeDtypeStruct((T, LANES), F32))
    return pl.pallas_call(
        functools.partial(_merge_kernel, with_router=with_router),
        grid=(T // tm,),
        in_specs=in_specs,
        out_specs=out_specs,
        out_shape=out_shape,
        compiler_params=_cparams(("parallel",)),
        name="merge_outproj",
    )(*args)


FF_CHUNK = 2 * LANES


def _swiglu(h, wg_ref, wu_ref, wd_ref):
    F = wg_ref.shape[-1]
    acc = None
    for lo in range(0, F, FF_CHUNK):
        sl = slice(lo, min(lo + FF_CHUNK, F))
        g = jnp.dot(h, wg_ref[:, sl], preferred_element_type=F32)
        u = jnp.dot(h, wu_ref[:, sl], preferred_element_type=F32)
        a = (g * jax.nn.sigmoid(g) * u).astype(BF16)
        d = jnp.dot(a, wd_ref[sl, :], preferred_element_type=F32)
        acc = d if acc is None else acc + d
    return acc


def _ffn_kernel(h_ref, x_ref, wg_ref, wu_ref, wd_ref, o_ref):
    o_ref[...] = x_ref[...] + _swiglu(h_ref[...], wg_ref, wu_ref, wd_ref)


def _dense_ffn(hn, x2, wg, wu, wd, layer):
    T, D = x2.shape
    F = wg.shape[2]
    tm = min(512, T)
    row = pl.BlockSpec((tm, D), lambda i: (i, 0))
    return pl.pallas_call(
        _ffn_kernel,
        grid=(T // tm,),
        in_specs=[row, row,
                  pl.BlockSpec((None, D, F), lambda i: (layer, 0, 0)),
                  pl.BlockSpec((None, D, F), lambda i: (layer, 0, 0)),
                  pl.BlockSpec((None, F, D), lambda i: (layer, 0, 0))],
        out_specs=row,
        out_shape=jax.ShapeDtypeStruct((T, D), F32),
        compiler_params=_cparams(("parallel",)),
        name="dense_ffn",
    )(hn, x2, wg, wu, wd)


def _row_copy(src_hbm, row, dst, r, sem):
    return pltpu.make_async_copy(src_hbm.at[pl.ds(row, 1)], dst.at[pl.ds(r, 1)], sem)


def _rows_done(src_hbm, dst, sem):
    return pltpu.make_async_copy(src_hbm.at[pl.ds(0, dst.shape[0])], dst, sem)


def _expert_kernel(be_ref, tok_ref, hn_hbm, wg_ref, wu_ref, wd_ref, o_ref, xbuf, sem):
    i = pl.program_id(0)
    last = pl.num_programs(0) - 1
    rows = xbuf.shape[1]
    slot = i % 2
    nxt = 1 - slot

    @pl.when(i == 0)
    def _():
        def issue(r, carry):
            _row_copy(hn_hbm, tok_ref[r], xbuf.at[0], r, sem.at[0]).start()
            return carry

        lax.fori_loop(0, rows, issue, 0, unroll=8)

    _rows_done(hn_hbm, xbuf.at[slot], sem.at[slot]).wait()
    base = jnp.minimum(i + 1, last) * rows
    for r in range(rows):
        _row_copy(hn_hbm, tok_ref[base + r], xbuf.at[nxt], r, sem.at[nxt]).start()
    o_ref[...] = _swiglu(xbuf[slot].astype(BF16), wg_ref.at[0], wu_ref.at[0], wd_ref.at[0])

    @pl.when(i == last)
    def _():
        _rows_done(hn_hbm, xbuf.at[nxt], sem.at[nxt]).wait()


def _experts(block_e, tok_of_slot, hn, wg, wu, wd, layer):
    T, D = hn.shape
    F = wg.shape[3]
    P = tok_of_slot.shape[0]
    rows = MOE_ROWS
    wspec = lambda shape: pl.BlockSpec((None, 1) + shape, lambda i, be, tok: (layer, be[i], 0, 0))
    return pl.pallas_call(
        _expert_kernel,
        grid_spec=pltpu.PrefetchScalarGridSpec(
            num_scalar_prefetch=2,
            grid=(P // rows,),
            in_specs=[pl.BlockSpec(memory_space=pl.ANY),
                      wspec((D, F)), wspec((D, F)), wspec((F, D))],
            out_specs=pl.BlockSpec((rows, D), lambda i, be, tok: (i, 0)),
            scratch_shapes=[pltpu.VMEM((2, rows, D), F32), pltpu.SemaphoreType.DMA((2,))]),
        out_shape=jax.ShapeDtypeStruct((P, D), F32),
        compiler_params=_cparams(("arbitrary",)),
        name="moe_experts",
    )(block_e, tok_of_slot, hn, wg, wu, wd)


def _combine_kernel(dest_ref, yp_hbm, x_ref, w_ref, o_ref, buf0, buf1, sem):
    i = pl.program_id(0)
    rows = buf0.shape[1]

    def gather(tile, slot):
        base = tile * rows * TOP_K

        for r in range(rows):
            _row_copy(yp_hbm, dest_ref[base + TOP_K * r], buf0.at[slot], r, sem.at[slot]).start()
            _row_copy(yp_hbm, dest_ref[base + TOP_K * r + 1], buf1.at[slot], r, sem.at[slot]).start()

    @pl.when(i == 0)
    def _():
        gather(0, 0)

    @pl.when(i + 1 < pl.num_programs(0))
    def _():
        gather(i + 1, (i + 1) % 2)

    slot = i % 2
    _rows_done(yp_hbm, buf0.at[slot], sem.at[slot]).wait()
    _rows_done(yp_hbm, buf1.at[slot], sem.at[slot]).wait()
    w = w_ref[...]
    o_ref[...] = x_ref[...] + (buf0[slot] * w[:, 0:1] + buf1[slot] * w[:, 1:2])


def _combine(dest, yp, x2, top_w):
    T, D = x2.shape
    tm = min(256, T)
    return pl.pallas_call(
        _combine_kernel,
        grid_spec=pltpu.PrefetchScalarGridSpec(
            num_scalar_prefetch=1,
            grid=(T // tm,),
            in_specs=[pl.BlockSpec(memory_space=pl.ANY),
                      pl.BlockSpec((tm, D), lambda i, d: (i, 0)),
                      pl.BlockSpec((tm, TOP_K), lambda i, d: (i, 0))],
            out_specs=pl.BlockSpec((tm, D), lambda i, d: (i, 0)),
            scratch_shapes=[pltpu.VMEM((2, tm, D), F32), pltpu.VMEM((2, tm, D), F32),
                            pltpu.SemaphoreType.DMA((2,))]),
        out_shape=jax.ShapeDtypeStruct((T, D), F32),
        compiler_params=_cparams(("arbitrary",)),
        name="moe_combine",
    )(dest, yp, x2, top_w)


def _route(logits, n_experts, rows):
    T = logits.shape[0]
    top_val, top_idx = lax.top_k(logits[:, :n_experts], TOP_K)
    top_w = jax.nn.softmax(top_val, axis=-1)
    A = T * TOP_K
    e_flat = top_idx.reshape(A).astype(jnp.int32)
    onehot = (e_flat[:, None] == jnp.arange(n_experts, dtype=jnp.int32)[None, :]).astype(jnp.int32)
    csum = jnp.cumsum(onehot, axis=0)
    counts = csum[-1]
    rank = jnp.sum((csum - onehot) * onehot, axis=1)
    padded = (counts + rows - 1) // rows * rows
    pad_end = jnp.cumsum(padded)
    pad_start = pad_end - padded
    dest = pad_start[e_flat] + rank
    n_blocks = -(-A // rows) + n_experts
    P = n_blocks * rows
    order = jnp.argsort(e_flat)
    starts = jnp.cumsum(counts) - counts
    slot = jnp.arange(P, dtype=jnp.int32)
    slot_e = jnp.minimum(jnp.sum((slot[:, None] >= pad_end[None, :]).astype(jnp.int32), axis=1),
                         n_experts - 1)
    within = slot - pad_start[slot_e]
    valid = within < counts[slot_e]
    src = jnp.clip(starts[slot_e] + within, 0, A - 1)
    tok_of_slot = jnp.where(valid, order[src] // TOP_K, 0).astype(jnp.int32)
    block_e = slot_e[::rows]
    return top_w.astype(F32), dest.astype(jnp.int32), tok_of_slot, block_e


def kernel(x, norm_mix, w_in, conv_w, w_f1, b_f1, freq1, w_f2, b_f2, freq2, w_f3, hy_bias, q_norm, k_norm, lam_q1, lam_k1, lam_q2, lam_k2, subln_g, w_hy_o, w_at_o, w_out, norm_ffn, w_dense_gate, w_dense_up, w_dense_down, w_router, w_moe_gate, w_moe_up, w_moe_down):
    B, L, D = x.shape
    T = B * L
    depth = w_in.shape[0]
    hw = w_hy_o.shape[1]
    qk_w = N_HEADS * 2 * HEAD_DIM
    at_w = N_HEADS * V_DIM
    mats = _dft_matrices(L)
    fpad = -w_moe_gate.shape[3] % LANES
    moe_w = (jnp.pad(w_moe_gate, ((0, 0), (0, 0), (0, 0), (0, fpad))).astype(BF16),
             jnp.pad(w_moe_up, ((0, 0), (0, 0), (0, 0), (0, fpad))).astype(BF16),
             jnp.pad(w_moe_down, ((0, 0), (0, 0), (0, fpad), (0, 0))).astype(BF16))
    dense_w = tuple(w.astype(BF16) for w in (w_dense_gate, w_dense_up, w_dense_down))

    x2 = x.reshape(T, D)
    for layer in range(depth):
        lambda_init = 0.8 - 0.6 * math.exp(-0.3 * layer)
        u3, qn, kn, v, gates = _inproj(x2, L, norm_mix[layer].astype(F32), w_in[layer].astype(BF16),
                                       conv_w[layer].astype(F32), q_norm[layer], k_norm[layer], hw, qk_w, at_w)
        kern = _hyena_time_filters(L, w_f1[layer], b_f1[layer], freq1[layer], w_f2[layer], b_f2[layer],
                                   freq2[layer], w_f3[layer], hw)
        y_h = _hyena(u3.reshape(3, hw // LANES, B, L // 2, LANES), kern, hy_bias[layer], mats)
        lam = (jnp.exp(jnp.sum(lam_q1[layer].astype(F32) * lam_k1[layer].astype(F32)))
               - jnp.exp(jnp.sum(lam_q2[layer].astype(F32) * lam_k2[layer].astype(F32))) + lambda_init)
        y_a = _attention(qn.reshape(B, L, qk_w), kn.reshape(B, L, qk_w), v.reshape(B, L, at_w), 0, lam,
                         subln_g[layer], 1.0 - lambda_init)
        i = layer // 2
        moe = layer % 2 == 1
        outs = _merge(x2, y_h.reshape(hw // LANES, T // 2, LANES), y_a.reshape(T, at_w), gates, 0,
                      w_hy_o[layer].astype(BF16), w_at_o[layer].astype(BF16), w_out[layer].astype(BF16),
                      norm_ffn[layer], w_router[i] if moe else None)
        if not moe:
            x2, hn = outs
            x2 = _dense_ffn(hn, x2, *dense_w, i)
        else:
            x2, hn, logits = outs
            n_experts = w_router.shape[2]
            top_w, dest, tok_of_slot, block_e = _route(logits, n_experts, MOE_ROWS)
            yp = _experts(block_e, tok_of_slot, hn, *moe_w, i)
            x2 = _combine(dest, yp, x2, top_w)
    return x2.reshape(B, L, D)
```

```python
import functools
import math

import numpy as np
import jax
import jax.numpy as jnp
from jax import lax
from jax.experimental import pallas as pl
from jax.experimental.pallas import tpu as pltpu

F32 = jnp.float32
BF16 = jnp.bfloat16
U32 = jnp.uint32

N_HEADS = 4
HEAD_DIM = 64
V_DIM = 2 * HEAD_DIM
ROPE_THETA = 10000.0
FILTER_BANDS = 16
HYENA_TARGET = 1e-2
FAST_DECAY_PCT = 0.3
SLOW_DECAY_PCT = 1.5
TOP_K = 2
RMS_EPS = 1e-6

LANES = 128
BF16_SUBLANES = 16
VMEM_LIMIT = 56 * 1024 * 1024

DFT_N1 = 128
DFT_PAIRS = 16
DFT_FILTER_STEP = 16
DFT_MID_K2 = 4
DFT_LANE_BLOCKS = 2
MOE_ROWS = 512


def _cparams(sem):
    return pltpu.CompilerParams(dimension_semantics=sem, vmem_limit_bytes=VMEM_LIMIT)


def _inproj_kernel(x_ref, xp_ref, xn_ref, g_ref, w_ref, cw_ref, cos_ref, sin_ref, qg_ref, kg_ref, grp_ref,
                   u_ref, q_ref, k_ref, v_ref, gate_ref, *, hw, qk_w, tiles_per_seq):
    pos = pl.program_id(0) % tiles_per_seq
    tm = x_ref.shape[0]
    halo = xp_ref.shape[0]

    def normed(x):
        ms = jnp.mean(x * x, axis=-1, keepdims=True)
        return (x * lax.rsqrt(ms + RMS_EPS) * g_ref[...]).astype(BF16)

    hn = normed(x_ref[...])
    hext = jnp.concatenate([normed(xp_ref[...]), hn, normed(xn_ref[...])], axis=0)
    row = lax.broadcasted_iota(jnp.int32, (tm + 2 * halo, hw), 0)
    outside = jnp.logical_or(jnp.logical_and(row < halo, pos == 0),
                             jnp.logical_and(row >= tm + halo, pos == tiles_per_seq - 1))
    hy_all = jnp.dot(hext, w_ref[:, :3 * hw], preferred_element_type=F32)
    rest = jnp.dot(hn, w_ref[:, 3 * hw:], preferred_element_type=F32)

    nc = u_ref.shape[1]
    for s in range(u_ref.shape[0]):
        cols = slice(s * hw, (s + 1) * hw)
        hy = jnp.where(outside, 0.0, hy_all[:, cols])
        up = pltpu.roll(hy, 1, 0)[halo:halo + tm]
        dn = pltpu.roll(hy, tm + 2 * halo - 1, 0)[halo:halo + tm]
        cw = cw_ref[:, cols]
        res = (up * cw[0:1] + hy[halo:halo + tm] * cw[1:2] + dn * cw[2:3]).astype(BF16)
        pk = pltpu.bitcast(res, U32)
        for c in range(nc):
            u_ref[s, c] = pk[:, c * LANES:(c + 1) * LANES]

    cos = cos_ref[...]
    sin = sin_ref[...]
    lane = lax.broadcasted_iota(jnp.int32, cos.shape, 1)
    first = (lane % HEAD_DIM) < (HEAD_DIM // 2)
    grp = grp_ref[...]
    base = 0
    for o_ref, gain_ref in ((q_ref, qg_ref), (k_ref, kg_ref)):
        for h in range(qk_w // LANES):
            x = rest[:, base + h * LANES:base + (h + 1) * LANES]
            sq_hi, sq_lo = _split_bf16(x * x)
            ss = (jnp.dot(sq_hi, grp, preferred_element_type=F32)
                  + jnp.dot(sq_lo, grp, preferred_element_type=F32))
            xn = x * lax.rsqrt(ss * (1.0 / HEAD_DIM) + RMS_EPS) * gain_ref[...]
            partner = jnp.where(first, pltpu.roll(xn, LANES - HEAD_DIM // 2, 1),
                                pltpu.roll(xn, HEAD_DIM // 2, 1))
            o_ref[:, h * LANES:(h + 1) * LANES] = (xn * cos + partner * sin).astype(o_ref.dtype)
        base += qk_w
    v_ref[...] = rest[:, base:base + v_ref.shape[1]].astype(v_ref.dtype)
    gate_ref[...] = rest[:, base + v_ref.shape[1]:].astype(gate_ref.dtype)


def _inproj(x2, L, g, w, conv_w, q_norm, k_norm, hw, qk_w, at_w):
    T, D = x2.shape
    tm = min(512, L)
    halo = BF16_SUBLANES
    hb = tm // halo
    nhb = T // halo
    nc = hw // LANES
    gw = w.shape[1] - 3 * hw - 2 * qk_w - at_w
    cos_t, sin_t = _rope_tables(L)
    qg = jnp.tile(q_norm.astype(F32), LANES // HEAD_DIM)[None, :] * F32(HEAD_DIM ** -0.5 * math.log2(math.e))
    kg = jnp.tile(k_norm.astype(F32), LANES // HEAD_DIM)[None, :]
    lanes = np.arange(LANES)
    grp = jnp.asarray(lanes[:, None] // HEAD_DIM == lanes[None, :] // HEAD_DIM, BF16)
    tps = L // tm
    const = lambda a: pl.BlockSpec(a.shape, lambda i: (0,) * a.ndim)
    rows = lambda width: pl.BlockSpec((tm, width), lambda i: (i, 0))
    tab = pl.BlockSpec((tm, LANES), lambda i: (i % tps, 0))
    gr = g.reshape(1, D)
    return pl.pallas_call(
        functools.partial(_inproj_kernel, hw=hw, qk_w=qk_w, tiles_per_seq=tps),
        grid=(T // tm,),
        in_specs=[rows(D),
                  pl.BlockSpec((halo, D), lambda i: (jnp.maximum(i * hb - 1, 0), 0)),
                  pl.BlockSpec((halo, D), lambda i: (jnp.minimum((i + 1) * hb, nhb - 1), 0)),
                  const(gr), const(w), const(conv_w), tab, tab, const(qg), const(kg), const(grp)],
        out_specs=[pl.BlockSpec((3, nc, tm // 2, LANES), lambda i: (0, 0, i, 0)),
                   rows(qk_w), rows(qk_w), rows(at_w), rows(gw)],
        out_shape=[jax.ShapeDtypeStruct((3, nc, T // 2, LANES), U32),
                   jax.ShapeDtypeStruct((T, qk_w), BF16), jax.ShapeDtypeStruct((T, qk_w), BF16),
                   jax.ShapeDtypeStruct((T, at_w), BF16), jax.ShapeDtypeStruct((T, gw), BF16)],
        compiler_params=_cparams(("parallel",)),
        name="inproj",
    )(x2, x2, x2, gr, w, conv_w, cos_t, sin_t, qg, kg, grp)


def _rope_tables(L):
    half = HEAD_DIM // 2
    inv = 1.0 / (ROPE_THETA ** (np.arange(0, HEAD_DIM, 2, dtype=np.float64) / HEAD_DIM))
    ang = np.arange(L, dtype=np.float64)[:, None] * inv[None, :]
    cos, sin = np.cos(ang), np.sin(ang)
    cos_t = np.tile(cos, (1, LANES // half))
    sin_t = np.tile(np.concatenate([-sin, sin], axis=1), (1, LANES // HEAD_DIM))
    return jnp.asarray(cos_t, F32), jnp.asarray(sin_t, F32)


def _split_bf16(x):
    hi = x.astype(BF16)
    return hi, (x - hi.astype(F32)).astype(BF16)


def _dot3(a, b):
    ah, al = _split_bf16(a)
    bh, bl = _split_bf16(b)
    d = lambda x, y: jnp.dot(x, y, preferred_element_type=F32)
    return d(ah, bh) + (d(ah, bl) + d(al, bh))


def _filter_kernel(zz_ref, w1_ref, b1_ref, f1_ref, w2_ref, b2_ref, f2_ref, w3_ref, dl_ref, o_ref):
    zz = zz_ref[...]
    h = _dot3(zz, w1_ref[...])
    h = jnp.sin(f1_ref[...] * (h + b1_ref[...]))
    h = _dot3(h, w2_ref[...])
    h = jnp.sin(f2_ref[...] * (h + b2_ref[...]))
    hf = _dot3(h, w3_ref[0])
    t = zz[:, 0:1]
    mask = zz[:, _MASK_COL:_MASK_COL + 1]
    kern = hf * (jnp.exp(-t * dl_ref[...]) * mask)
    for c in range(o_ref.shape[0]):
        o_ref[c] = kern[:, c * LANES:(c + 1) * LANES]


_MASK_COL = 2 * FILTER_BANDS + 1


def _filter_positions(L):
    bands = FILTER_BANDS
    t = np.linspace(0.0, 1.0, L, dtype=np.float64)[:, None]
    w = 2.0 * math.pi * np.arange(L, dtype=np.float64)[:, None] / L
    f = np.linspace(1e-4, bands - 1, bands, dtype=np.float64)[None, :]
    z = np.concatenate([t, np.cos(f * w), -np.sin(f * w)], axis=-1)
    zz = np.zeros((2 * L, LANES), np.float64)
    zz[:L, :z.shape[1]] = z
    zz[L + 1:, :z.shape[1]] = z[1:][::-1]
    zz[:, _MASK_COL] = 1.0
    zz[L, :] = 0.0
    return zz.astype(np.float32)


def _hyena_time_filters(L, w_f1, b_f1, freq1, w_f2, b_f2, freq2, w_f3, hw):
    emb, hid = w_f1.shape
    zz = jnp.asarray(_filter_positions(L))
    w1 = jnp.zeros((LANES, hid), F32).at[:emb].set(w_f1)
    w3 = w_f3.reshape(hid, 2, 2, hw).transpose(2, 0, 1, 3).reshape(2, hid, 2 * hw)
    max_decay = math.log(HYENA_TARGET) / FAST_DECAY_PCT
    min_decay = math.log(HYENA_TARGET) / SLOW_DECAY_PCT
    deltas = np.abs(np.linspace(min_decay, max_decay, hw, dtype=np.float64))
    dl = jnp.asarray(np.tile(deltas, 2)[None, :], F32)
    tr = min(512, L)
    per_dir = L // tr
    row = lambda a: a.reshape(1, -1)
    return pl.pallas_call(
        _filter_kernel,
        grid=(2 * L // tr,),
        in_specs=[pl.BlockSpec((tr, LANES), lambda i: (i, 0)),
                  pl.BlockSpec((LANES, hid), lambda i: (0, 0)),
                  pl.BlockSpec((1, hid), lambda i: (0, 0)),
                  pl.BlockSpec((1, hid), lambda i: (0, 0)),
                  pl.BlockSpec((hid, hid), lambda i: (0, 0)),
                  pl.BlockSpec((1, hid), lambda i: (0, 0)),
                  pl.BlockSpec((1, hid), lambda i: (0, 0)),
                  pl.BlockSpec((1, hid, 2 * hw), lambda i: (i // per_dir, 0, 0)),
                  pl.BlockSpec((1, 2 * hw), lambda i: (0, 0))],
        out_specs=pl.BlockSpec((2 * hw // LANES, tr, LANES), lambda i: (0, i, 0)),
        out_shape=jax.ShapeDtypeStruct((2 * hw // LANES, 2 * L, LANES), F32),
        compiler_params=_cparams(("parallel",)),
        name="hyena_filter",
    )(zz, w1, row(b_f1), row(freq1), w_f2, row(b_f2), row(freq2), w3, dl)


def _dft_matrices(L):
    N = 2 * L
    N1 = DFT_N1
    N2 = N // N1
    h = N2 // 2
    ang = 2.0 * np.pi * np.outer(np.arange(N2), np.arange(N2)) / N2
    c, s = np.cos(ang), np.sin(ang)
    m1 = np.zeros((N2, 2, N2))
    m1[:, 0, :h], m1[:, 0, h:] = c[:, :h], s[:, :h]
    m1[:, 1, :h], m1[:, 1, h:] = -s[:, :h], c[:, :h]
    m1 = m1.reshape(2 * N2, N2)
    m1f = np.stack([c, -s], axis=1).reshape(2 * N2, N2)
    m3 = np.zeros((2, h, N2, 2))
    m3[0, :, :, 0], m3[0, :, :, 1] = c[:h], -s[:h]
    m3[1, :, :, 0], m3[1, :, :, 1] = s[:h], c[:h]
    m3 = (m3 / N).reshape(N2, 2 * N2)
    def mid(order, transposed):
        f = np.arange(2 * N1)[:, None]
        d = np.asarray(order)[None, :]
        a = 2.0 * np.pi * ((d % N1) * (f % N1) % N1) / N1
        same, up = (f // N1 == d // N1), (f // N1 < d // N1)
        pc = np.where(same, 1.0, 0.0)
        ps = np.where(same, 0.0, np.where(up, 1.0, -1.0))
        u = pc * np.cos(a) + ps * np.sin(a)
        v = ps * np.cos(a) - pc * np.sin(a)
        b = 2.0 * np.pi * (np.arange(N2)[:, None] * (d % N1) % N) / N
        if transposed:
            u, v = u.T, v.T
            cb, sb = jnp.asarray(np.cos(b), F32)[:, :, None], jnp.asarray(np.sin(b), F32)[:, :, None]
        else:
            cb, sb = jnp.asarray(np.cos(b), F32)[:, None, :], jnp.asarray(np.sin(b), F32)[:, None, :]
        return jnp.asarray(u, F32)[None] * cb + jnp.asarray(v, F32)[None] * sb

    packed = _mid_row_order(DFT_PAIRS, 2)
    eye2 = np.eye(2)
    return dict(m1x=jnp.asarray(np.kron(m1, eye2), BF16),
                m3x=jnp.asarray(np.kron(m3, eye2), BF16),
                m1f=jnp.asarray(m1f, F32),
                m2f=mid(_mid_row_order(DFT_FILTER_STEP, 1), False),
                m2p=mid(packed, False).astype(BF16),
                m2tp=mid(packed, True).astype(BF16))


def _dft_filter_a_kernel(m_ref, k_ref, o_ref, *, step, n2):
    j0 = pl.program_id(1) * step
    rows = m_ref.shape[0]
    for jj in range(step):
        rhs = jnp.concatenate([k_ref[c, pl.ds(j0 + jj, n2, stride=DFT_N1), :] for c in range(k_ref.shape[0])],
                              axis=1)
        res = _dot3(m_ref[...], rhs)
        for c in range(o_ref.shape[0]):
            o_ref[c, 0, pl.ds(jj, rows, stride=step), :] = res[:, c * LANES:(c + 1) * LANES]


def _dft_filter_a(m1f, kern):
    NB, N, _ = kern.shape
    n2 = N // DFT_N1
    step, ncb = DFT_FILTER_STEP, DFT_LANE_BLOCKS
    rows = m1f.shape[0]
    return pl.pallas_call(
        functools.partial(_dft_filter_a_kernel, step=step, n2=n2),
        grid=(NB // ncb, DFT_N1 // step),
        in_specs=[pl.BlockSpec(m1f.shape, lambda h, j: (0, 0)),
                  pl.BlockSpec((ncb, N, LANES), lambda h, j: (h, 0, 0))],
        out_specs=pl.BlockSpec((ncb, 1, rows * step, LANES), lambda h, j: (h, j, 0, 0)),
        out_shape=jax.ShapeDtypeStruct((NB, DFT_N1 // step, rows * step, LANES), F32),
        compiler_params=_cparams(("parallel", "arbitrary")),
        name="dft_filter_a",
    )(m1f, kern)


def _dft_filt_kernel(a_ref, m_ref, o_ref):
    rows = m_ref.shape[1]
    a = jnp.concatenate([a_ref[c, :, 0].reshape(rows, LANES) for c in range(a_ref.shape[0])], axis=1)
    o_ref[0] = _dot3(m_ref[0], a)


def _dft_filt(a, m2):
    NB, G, N2, W, _ = a.shape
    R = m2.shape[1]
    return pl.pallas_call(
        _dft_filt_kernel,
        grid=(N2,),
        in_specs=[pl.BlockSpec((NB, G, 1, W, LANES), lambda k: (0, 0, k, 0, 0)),
                  pl.BlockSpec((1, R, R), lambda k: (k, 0, 0))],
        out_specs=pl.BlockSpec((1, R, NB * LANES), lambda k: (k, 0, 0)),
        out_shape=jax.ShapeDtypeStruct((N2, R, NB * LANES), F32),
        compiler_params=_cparams(("parallel",)),
        name="dft_filter_mid",
    )(a, m2)


def _packed_rows(ref, lead, start, size, stride):
    cols = [jnp.concatenate([ref[(c,) + l + (pl.ds(start, size, stride=stride), slice(None))] for l in lead], axis=0)
            for c in range(ref.shape[0])]
    return pltpu.bitcast(jnp.concatenate(cols, axis=1), BF16)


def _dft_a_pk_kernel(m_ref, z_ref, o_ref, *, tm, nh, npairs):
    j0 = pl.program_id(2) * tm
    rows = m_ref.shape[0] // 2
    for j in range(tm):
        rhs = _packed_rows(z_ref, [(0,), (1,)], j0 + j, nh, npairs)
        res = jnp.dot(m_ref[...], rhs, preferred_element_type=F32)
        pk = pltpu.bitcast(res.astype(BF16), U32)
        for c in range(o_ref.shape[0]):
            o_ref[c, 0, 0, pl.ds(j, rows, stride=tm), :] = pk[:, c * LANES:(c + 1) * LANES]


def _dft_a_pk(m1x, z, slot):
    _, NC, B, L2, _ = z.shape
    npairs = DFT_N1 // 2
    nh = L2 // npairs
    tm, ncb = DFT_PAIRS, DFT_LANE_BLOCKS
    rows = m1x.shape[0] // 2
    P = B // 2
    return pl.pallas_call(
        functools.partial(_dft_a_pk_kernel, tm=tm, nh=nh, npairs=npairs),
        grid=(P, NC // ncb, npairs // tm),
        in_specs=[pl.BlockSpec(m1x.shape, lambda p, h, j: (0, 0)),
                  pl.BlockSpec((None, ncb, 2, L2, LANES), lambda p, h, j: (slot, h, p, 0, 0))],
        out_specs=pl.BlockSpec((ncb, 1, 1, rows * tm, LANES), lambda p, h, j: (h, p, j, 0, 0)),
        out_shape=jax.ShapeDtypeStruct((NC, P, npairs // tm, rows * tm, LANES), U32),
        compiler_params=_cparams(("parallel", "parallel", "arbitrary")),
        name="dft_a",
    )(m1x, z)


def _dft_mid_kernel(a_ref, m_ref, mt_ref, kf_ref, o_ref):
    n1 = m_ref.shape[1] // 2
    nc, npair = a_ref.shape[0], a_ref.shape[1]
    blk = a_ref.shape[2], a_ref.shape[4], LANES
    for kk in range(a_ref.shape[3]):
        m = m_ref[kk]
        mt = mt_ref[kk]
        kr = kf_ref[kk, :n1]
        ki = kf_ref[kk, n1:]
        for p in range(npair):
            cols = [a_ref[c, p, :, kk].reshape(n1, LANES) for c in range(nc)]
            a = pltpu.bitcast(jnp.concatenate(cols, axis=1), BF16)
            x = jnp.dot(m, a, preferred_element_type=F32)
            xr = x[:n1]
            xi = x[n1:]
            y = jnp.concatenate([xr * kr - xi * ki, xr * ki + xi * kr], axis=0).astype(BF16)
            pk = pltpu.bitcast(jnp.dot(mt, y, preferred_element_type=F32).astype(BF16), U32)
            for c in range(nc):
                o_ref[c, p, :, kk] = pk[:, c * LANES:(c + 1) * LANES].reshape(blk)


def _dft_mid(a, m2, m2t, kf, order):
    NC, P, G, N2, W, _ = a.shape
    R = m2.shape[1]
    C = NC * LANES
    kb = DFT_MID_K2
    blk = pl.BlockSpec((NC, P, G, kb, W, LANES), lambda k: (0, 0, 0, k, 0, 0))
    return pl.pallas_call(
        _dft_mid_kernel,
        grid=(N2 // kb,),
        in_specs=[blk,
                  pl.BlockSpec((kb, R, R), lambda k: (k, 0, 0)),
                  pl.BlockSpec((kb, R, R), lambda k: (k, 0, 0)),
                  pl.BlockSpec((kb, R, C), lambda k: (k, 0, order))],
        out_specs=blk,
        out_shape=jax.ShapeDtypeStruct(a.shape, U32),
        compiler_params=_cparams(("parallel",)),
        name="dft_mid",
    )(a, m2, m2t, kf)


def _dft_c_pk_kernel(m_ref, b_ref, z_ref, g_ref, bias_ref, o_ref, *, tm, nh, npairs):
    j0 = pl.program_id(2) * tm
    rows = b_ref.shape[3] // tm
    bias = bias_ref[...]
    both = [(0,), (1,)]
    for j in range(tm):
        rhs = _packed_rows(b_ref, [(0, 0)], j, rows, tm)
        y = jnp.dot(m_ref[...], rhs, preferred_element_type=F32)
        z = _packed_rows(z_ref, both, j0 + j, nh, npairs).astype(F32)
        g = _packed_rows(g_ref, both, j0 + j, nh, npairs).astype(F32)
        pk = pltpu.bitcast((g * (y + z * bias)).astype(BF16), U32)
        for c in range(o_ref.shape[0]):
            for b in range(2):
                o_ref[c, b, pl.ds(j0 + j, nh, stride=npairs), :] = pk[b * nh:(b + 1) * nh, c * LANES:(c + 1) * LANES]


def _dft_c_pk(m3x, b, z, zslot, g, gslot, bias):
    NC, P, G, W, _ = b.shape
    L2 = z.shape[3]
    npairs = DFT_N1 // 2
    nh = L2 // npairs
    tm, ncb = DFT_PAIRS, DFT_LANE_BLOCKS
    seq = lambda slot: pl.BlockSpec((None, ncb, 2, L2, LANES), lambda p, h, j: (slot, h, p, 0, 0))
    return pl.pallas_call(
        functools.partial(_dft_c_pk_kernel, tm=tm, nh=nh, npairs=npairs),
        grid=(P, NC // ncb, G),
        in_specs=[pl.BlockSpec(m3x.shape, lambda p, h, j: (0, 0)),
                  pl.BlockSpec((ncb, 1, 1, W, LANES), lambda p, h, j: (h, p, j, 0, 0)),
                  seq(zslot), seq(gslot),
                  pl.BlockSpec((1, ncb * LANES), lambda p, h, j: (0, h))],
        out_specs=seq(0),
        out_shape=jax.ShapeDtypeStruct((1, NC, 2 * P, L2, LANES), U32),
        compiler_params=_cparams(("parallel", "parallel", "arbitrary")),
        name="dft_c",
    )(m3x, b, z, g, bias)


def _mid_row_order(step, per):
    N1 = DFT_N1
    idx = []
    for mt in range(N1 // per // step):
        for ri in range(2):
            for jj in range(step):
                for sub in range(per):
                    idx.append(ri * N1 + per * (mt * step + jj) + sub)
    return np.asarray(idx, np.int32)


def _hyena(u3, kern, hy_bias, mats):
    _, NC, B, L2, _ = u3.shape
    C = NC * LANES
    N1 = DFT_N1
    N2 = 4 * L2 // N1
    af = _dft_filter_a(mats["m1f"], kern)
    NB, G, W, _ = af.shape
    kf = _dft_filt(af.reshape(NB, G, N2, W // N2, LANES), mats["m2f"])
    bias = hy_bias.astype(F32)

    def long_conv(z, zslot, gslot, order):
        a = _dft_a_pk(mats["m1x"], z, zslot)
        NCa, P, G, W, _ = a.shape
        b = _dft_mid(a.reshape(NCa, P, G, N2, W // N2, LANES), mats["m2p"], mats["m2tp"], kf, order)
        return _dft_c_pk(mats["m3x"], b.reshape(a.shape), z, zslot, u3, gslot, bias[order][None, :])

    z = long_conv(u3, 2, 0, 0)
    return long_conv(z, 0, 1, 1)


SHIFT_LIMIT = 50.0


def _attn_kernel(lam_ref, q_ref, k_ref, v_ref, g_ref, o_ref, kmax_ref, *, post):
    k = k_ref[0]
    v = v_ref[0]

    @pl.when(pl.program_id(2) == 0)
    def _():
        kmax_ref[...] = jnp.max(jnp.abs(k.astype(F32)), axis=0, keepdims=True)

    q = q_ref[0]
    lane = lax.broadcasted_iota(jnp.int32, q.shape, 1)
    lo = lane < HEAD_DIM
    zero = jnp.zeros_like(q)
    reach = jnp.abs(q.astype(F32)) * kmax_ref[...]
    shifts = [jnp.sum(jnp.where(lo, reach, 0.0), axis=-1, keepdims=True),
              jnp.sum(jnp.where(lo, 0.0, reach), axis=-1, keepdims=True)]
    worst = jnp.max(jnp.maximum(shifts[0], shifts[1]))

    def run(row_max):
        outs = []
        for c in range(2):
            qc = jnp.where(lo, q, zero) if c == 0 else jnp.where(lo, zero, q)
            s = lax.dot_general(qc, k, (((1,), (1,)), ((), ())), preferred_element_type=F32)
            m = jnp.max(s, axis=-1, keepdims=True) if row_max else shifts[c]
            p = jnp.exp2(s - m)
            l = jnp.sum(p, axis=-1, keepdims=True)
            outs.append(jnp.dot(p.astype(BF16), v, preferred_element_type=F32) / l)
        a = outs[0] - lam_ref[0] * outs[1]
        ms = jnp.mean(a * a, axis=-1, keepdims=True)
        o_ref[0] = (a * lax.rsqrt(ms + RMS_EPS) * g_ref[...] * post).astype(o_ref.dtype)

    @pl.when(worst <= SHIFT_LIMIT)
    def _():
        run(False)

    @pl.when(jnp.logical_not(worst <= SHIFT_LIMIT))
    def _():
        run(True)


def _attention(qn, kn, proj3, vcol, lam, subln_g, post):
    B, L, W = qn.shape
    tq = min(512, L)
    return pl.pallas_call(
        functools.partial(_attn_kernel, post=post),
        grid=(B, W // V_DIM, L // tq),
        in_specs=[pl.BlockSpec(memory_space=pltpu.SMEM),
                  pl.BlockSpec((1, tq, V_DIM), lambda b, h, i: (b, i, h)),
                  pl.BlockSpec((1, L, V_DIM), lambda b, h, i: (b, 0, h)),
                  pl.BlockSpec((1, L, V_DIM), lambda b, h, i: (b, 0, vcol + h)),
                  pl.BlockSpec((1, V_DIM), lambda b, h, i: (0, 0))],
        out_specs=pl.BlockSpec((1, tq, V_DIM), lambda b, h, i: (b, i, h)),
        out_shape=jax.ShapeDtypeStruct((B, L, W), BF16),
        scratch_shapes=[pltpu.VMEM((1, V_DIM), F32)],
        compiler_params=_cparams(("parallel", "parallel", "arbitrary")),
        name="diff_attention",
    )(lam.reshape(1).astype(F32), qn, kn, proj3, subln_g.astype(F32).reshape(1, V_DIM))


def _merge_kernel(x_ref, yh_ref, ya_ref, gh_ref, ga_ref, who_ref, wao_ref, wout_ref, g_ref, *rest,
                  with_router):
    if with_router:
        wr_ref, xo_ref, hn_ref, lg_ref = rest
    else:
        xo_ref, hn_ref = rest
    yh = jnp.concatenate([pltpu.bitcast(yh_ref[c], BF16) for c in range(yh_ref.shape[0])], axis=1)
    th = jnp.dot(yh, who_ref[...], preferred_element_type=F32)
    ta = jnp.dot(ya_ref[...], wao_ref[...], preferred_element_type=F32)
    merged = (jax.nn.sigmoid(gh_ref[...].astype(F32)) * th
              + jax.nn.sigmoid(ga_ref[...].astype(F32)) * ta)
    xn = x_ref[...] + jnp.dot(merged.astype(BF16), wout_ref[...], preferred_element_type=F32)
    xo_ref[...] = xn
    ms = jnp.mean(xn * xn, axis=-1, keepdims=True)
    h = xn * lax.rsqrt(ms + RMS_EPS) * g_ref[...]
    hn_ref[...] = h.astype(hn_ref.dtype)
    if with_router:
        lane = lax.broadcasted_iota(jnp.int32, lg_ref.shape, 1)
        lg = jnp.zeros(lg_ref.shape, F32)
        for e in range(wr_ref.shape[0]):
            lg = jnp.where(lane == e, jnp.sum(h * wr_ref[e:e + 1, :], axis=-1, keepdims=True), lg)
        lg_ref[...] = lg


def _merge(x2, yh, ya, proj, gcol, who, wao, wout, g_ffn, w_router):
    T, D = x2.shape
    W = ya.shape[1]
    tm = min(512, T)
    with_router = w_router is not None
    row = lambda w: pl.BlockSpec((tm, w), lambda i: (i, 0))
    full = lambda a: pl.BlockSpec(a.shape, lambda i: (0, 0))
    in_specs = [row(D), pl.BlockSpec((yh.shape[0], tm // 2, LANES), lambda i: (0, i, 0)), row(W),
                pl.BlockSpec((tm, D), lambda i: (i, gcol)),
                pl.BlockSpec((tm, D), lambda i: (i, gcol + 1)),
                full(who), full(wao), full(wout), pl.BlockSpec((1, D), lambda i: (0, 0))]
    args = [x2, yh, ya, proj, proj, who, wao, wout, g_ffn.astype(F32).reshape(1, D)]
    out_specs = [row(D), row(D)]
    out_shape = [jax.ShapeDtypeStruct((T, D), F32),
                 jax.ShapeDtypeStruct((T, D), F32 if with_router else BF16)]
    if with_router:
        wr = w_router.astype(F32).T
        in_specs.append(full(wr))
        args.append(wr)
        out_specs.append(row(LANES))
        out_shape.append(jax.ShapeDtypeStruct((T, LANES), F32))
    return pl.pallas_call(
        functools.partial(_merge_kernel, with_router=with_router),
        grid=(T // tm,),
        in_specs=in_specs,
        out_specs=out_specs,
        out_shape=out_shape,
        compiler_params=_cparams(("parallel",)),
        name="merge_outproj",
    )(*args)


FF_CHUNK = 2 * LANES


def _swiglu(h, wg_ref, wu_ref, wd_ref):
    F = wg_ref.shape[-1]
    acc = None
    for lo in range(0, F, FF_CHUNK):
        sl = slice(lo, min(lo + FF_CHUNK, F))
        g = jnp.dot(h, wg_ref[:, sl], preferred_element_type=F32)
        u = jnp.dot(h, wu_ref[:, sl], preferred_element_type=F32)
        a = (g * jax.nn.sigmoid(g) * u).astype(BF16)
        d = jnp.dot(a, wd_ref[sl, :], preferred_element_type=F32)
        acc = d if acc is None else acc + d
    return acc


def _ffn_kernel(h_ref, x_ref, wg_ref, wu_ref, wd_ref, o_ref):
    o_ref[...] = x_ref[...] + _swiglu(h_ref[...], wg_ref, wu_ref, wd_ref)


def _dense_ffn(hn, x2, wg, wu, wd, layer):
    T, D = x2.shape
    F = wg.shape[2]
    tm = min(512, T)
    row = pl.BlockSpec((tm, D), lambda i: (i, 0))
    return pl.pallas_call(
        _ffn_kernel,
        grid=(T // tm,),
        in_specs=[row, row,
                  pl.BlockSpec((None, D, F), lambda i: (layer, 0, 0)),
                  pl.BlockSpec((None, D, F), lambda i: (layer, 0, 0)),
                  pl.BlockSpec((None, F, D), lambda i: (layer, 0, 0))],
        out_specs=row,
        out_shape=jax.ShapeDtypeStruct((T, D), F32),
        compiler_params=_cparams(("parallel",)),
        name="dense_ffn",
    )(hn, x2, wg, wu, wd)


def _row_copy(src_hbm, row, dst, r, sem):
    return pltpu.make_async_copy(src_hbm.at[pl.ds(row, 1)], dst.at[pl.ds(r, 1)], sem)


def _rows_done(src_hbm, dst, sem):
    return pltpu.make_async_copy(src_hbm.at[pl.ds(0, dst.shape[0])], dst, sem)


def _expert_kernel(be_ref, tok_ref, hn_hbm, wg_ref, wu_ref, wd_ref, o_ref, xbuf, sem):
    i = pl.program_id(0)
    used = be_ref[pl.num_programs(0)]
    last = used - 1
    rows = xbuf.shape[1]
    slot = i % 2
    nxt = 1 - slot

    @pl.when(i < used)
    def _():
        @pl.when(i == 0)
        def _():
            def issue(r, carry):
                _row_copy(hn_hbm, tok_ref[r], xbuf.at[0], r, sem.at[0]).start()
                return carry

            lax.fori_loop(0, rows, issue, 0, unroll=8)

        _rows_done(hn_hbm, xbuf.at[slot], sem.at[slot]).wait()
        base = jnp.minimum(i + 1, last) * rows
        for r in range(rows):
            _row_copy(hn_hbm, tok_ref[base + r], xbuf.at[nxt], r, sem.at[nxt]).start()
        o_ref[...] = _swiglu(xbuf[slot].astype(BF16), wg_ref.at[0], wu_ref.at[0], wd_ref.at[0])

        @pl.when(i == last)
        def _():
            _rows_done(hn_hbm, xbuf.at[nxt], sem.at[nxt]).wait()

    @pl.when(i >= used)
    def _():
        o_ref[...] = jnp.zeros_like(o_ref)


def _experts(block_e, tok_of_slot, hn, wg, wu, wd, layer):
    T, D = hn.shape
    F = wg.shape[3]
    P = tok_of_slot.shape[0]
    rows = MOE_ROWS
    wspec = lambda shape: pl.BlockSpec((None, 1) + shape, lambda i, be, tok: (layer, be[i], 0, 0))
    return pl.pallas_call(
        _expert_kernel,
        grid_spec=pltpu.PrefetchScalarGridSpec(
            num_scalar_prefetch=2,
            grid=(P // rows,),
            in_specs=[pl.BlockSpec(memory_space=pl.ANY),
                      wspec((D, F)), wspec((D, F)), wspec((F, D))],
            out_specs=pl.BlockSpec((rows, D), lambda i, be, tok: (i, 0)),
            scratch_shapes=[pltpu.VMEM((2, rows, D), F32), pltpu.SemaphoreType.DMA((2,))]),
        out_shape=jax.ShapeDtypeStruct((P, D), F32),
        compiler_params=_cparams(("arbitrary",)),
        name="moe_experts",
    )(block_e, tok_of_slot, hn, wg, wu, wd)


def _combine_kernel(dest_ref, yp_hbm, x_ref, w_ref, o_ref, buf0, buf1, sem):
    i = pl.program_id(0)
    rows = buf0.shape[1]

    def gather(tile, slot):
        base = tile * rows * TOP_K

        for r in range(rows):
            _row_copy(yp_hbm, dest_ref[base + TOP_K * r], buf0.at[slot], r, sem.at[slot]).start()
            _row_copy(yp_hbm, dest_ref[base + TOP_K * r + 1], buf1.at[slot], r, sem.at[slot]).start()

    @pl.when(i == 0)
    def _():
        gather(0, 0)

    @pl.when(i + 1 < pl.num_programs(0))
    def _():
        gather(i + 1, (i + 1) % 2)

    slot = i % 2
    _rows_done(yp_hbm, buf0.at[slot], sem.at[slot]).wait()
    _rows_done(yp_hbm, buf1.at[slot], sem.at[slot]).wait()
    w = w_ref[...]
    o_ref[...] = x_ref[...] + (buf0[slot] * w[:, 0:1] + buf1[slot] * w[:, 1:2])


def _combine(dest, yp, x2, top_w):
    T, D = x2.shape
    tm = min(256, T)
    return pl.pallas_call(
        _combine_kernel,
        grid_spec=pltpu.PrefetchScalarGridSpec(
            num_scalar_prefetch=1,
            grid=(T // tm,),
            in_specs=[pl.BlockSpec(memory_space=pl.ANY),
                      pl.BlockSpec((tm, D), lambda i, d: (i, 0)),
                      pl.BlockSpec((tm, TOP_K), lambda i, d: (i, 0))],
            out_specs=pl.BlockSpec((tm, D), lambda i, d: (i, 0)),
            scratch_shapes=[pltpu.VMEM((2, tm, D), F32), pltpu.VMEM((2, tm, D), F32),
                            pltpu.SemaphoreType.DMA((2,))]),
        out_shape=jax.ShapeDtypeStruct((T, D), F32),
        compiler_params=_cparams(("arbitrary",)),
        name="moe_combine",
    )(dest, yp, x2, top_w)


def _route(logits, n_experts, rows):
    T = logits.shape[0]
    top_val, top_idx = lax.top_k(logits[:, :n_experts], TOP_K)
    top_w = jax.nn.softmax(top_val, axis=-1)
    A = T * TOP_K
    e_flat = top_idx.reshape(A).astype(jnp.int32)
    onehot = (e_flat[:, None] == jnp.arange(n_experts, dtype=jnp.int32)[None, :]).astype(jnp.int32)
    csum = jnp.cumsum(onehot, axis=0)
    counts = csum[-1]
    rank = jnp.sum((csum - onehot) * onehot, axis=1)
    padded = (counts + rows - 1) // rows * rows
    pad_end = jnp.cumsum(padded)
    pad_start = pad_end - padded
    dest = pad_start[e_flat] + rank
    n_blocks = -(-A // rows) + n_experts
    P = n_blocks * rows
    order = jnp.argsort(e_flat)
    starts = jnp.cumsum(counts) - counts
    slot = jnp.arange(P, dtype=jnp.int32)
    slot_e = jnp.minimum(jnp.sum((slot[:, None] >= pad_end[None, :]).astype(jnp.int32), axis=1),
                         n_experts - 1)
    within = slot - pad_start[slot_e]
    valid = within < counts[slot_e]
    src = jnp.clip(starts[slot_e] + within, 0, A - 1)
    tok_of_slot = jnp.where(valid, order[src] // TOP_K, 0).astype(jnp.int32)
    n_used = (pad_end[-1] // rows).astype(jnp.int32)
    block_e = jnp.concatenate([slot_e[::rows].astype(jnp.int32), n_used[None]])
    return top_w.astype(F32), dest.astype(jnp.int32), tok_of_slot, block_e


def kernel(x, norm_mix, w_in, conv_w, w_f1, b_f1, freq1, w_f2, b_f2, freq2, w_f3, hy_bias, q_norm, k_norm, lam_q1, lam_k1, lam_q2, lam_k2, subln_g, w_hy_o, w_at_o, w_out, norm_ffn, w_dense_gate, w_dense_up, w_dense_down, w_router, w_moe_gate, w_moe_up, w_moe_down):
    B, L, D = x.shape
    T = B * L
    depth = w_in.shape[0]
    hw = w_hy_o.shape[1]
    qk_w = N_HEADS * 2 * HEAD_DIM
    at_w = N_HEADS * V_DIM
    mats = _dft_matrices(L)
    moe_w = tuple(w.astype(BF16) for w in (w_moe_gate, w_moe_up, w_moe_down))
    dense_w = tuple(w.astype(BF16) for w in (w_dense_gate, w_dense_up, w_dense_down))

    x2 = x.reshape(T, D)
    for layer in range(depth):
        lambda_init = 0.8 - 0.6 * math.exp(-0.3 * layer)
        u3, qn, kn, v, gates = _inproj(x2, L, norm_mix[layer].astype(F32), w_in[layer].astype(BF16),
                                       conv_w[layer].astype(F32), q_norm[layer], k_norm[layer], hw, qk_w, at_w)
        kern = _hyena_time_filters(L, w_f1[layer], b_f1[layer], freq1[layer], w_f2[layer], b_f2[layer],
                                   freq2[layer], w_f3[layer], hw)
        y_h = _hyena(u3.reshape(3, hw // LANES, B, L // 2, LANES), kern, hy_bias[layer], mats)
        lam = (jnp.exp(jnp.sum(lam_q1[layer].astype(F32) * lam_k1[layer].astype(F32)))
               - jnp.exp(jnp.sum(lam_q2[layer].astype(F32) * lam_k2[layer].astype(F32))) + lambda_init)
        y_a = _attention(qn.reshape(B, L, qk_w), kn.reshape(B, L, qk_w), v.reshape(B, L, at_w), 0, lam,
                         subln_g[layer], 1.0 - lambda_init)
        i = layer // 2
        moe = layer % 2 == 1
        outs = _merge(x2, y_h.reshape(hw // LANES, T // 2, LANES), y_a.reshape(T, at_w), gates, 0,
                      w_hy_o[layer].astype(BF16), w_at_o[layer].astype(BF16), w_out[layer].astype(BF16),
                      norm_ffn[layer], w_router[i] if moe else None)
        if not moe:
            x2, hn = outs
            x2 = _dense_ffn(hn, x2, *dense_w, i)
        else:
            x2, hn, logits = outs
            n_experts = w_router.shape[2]
            top_w, dest, tok_of_slot, block_e = _route(logits, n_experts, MOE_ROWS)
            yp = _experts(block_e, tok_of_slot, hn, *moe_w, i)
            x2 = _combine(dest, yp, x2, top_w)
    return x2.reshape(B, L, D)
```

```python
import functools
import math

import numpy as np
import jax
import jax.numpy as jnp
from jax import lax
from jax.experimental import pallas as pl
from jax.experimental.pallas import tpu as pltpu

F32 = jnp.float32
BF16 = jnp.bfloat16
U32 = jnp.uint32

N_HEADS = 4
HEAD_DIM = 64
V_DIM = 2 * HEAD_DIM
ROPE_THETA = 10000.0
FILTER_BANDS = 16
HYENA_TARGET = 1e-2
FAST_DECAY_PCT = 0.3
SLOW_DECAY_PCT = 1.5
TOP_K = 2
RMS_EPS = 1e-6

LANES = 128
BF16_SUBLANES = 16
VMEM_LIMIT = 56 * 1024 * 1024

DFT_N1 = 128
DFT_PAIRS = 16
DFT_FILTER_STEP = 16
DFT_MID_K2 = 4
DFT_LANE_BLOCKS = 2
MOE_ROWS = 512


def _cparams(sem):
    return pltpu.CompilerParams(dimension_semantics=sem, vmem_limit_bytes=VMEM_LIMIT)


def _inproj_kernel(x_ref, xp_ref, xn_ref, g_ref, w_ref, cw_ref, cos_ref, sin_ref, qg_ref, kg_ref, grp_ref,
                   u_ref, q_ref, k_ref, v_ref, gate_ref, *, hw, qk_w, tiles_per_seq):
    pos = pl.program_id(0) % tiles_per_seq
    tm = x_ref.shape[0]
    halo = xp_ref.shape[0]

    def normed(x):
        ms = jnp.mean(x * x, axis=-1, keepdims=True)
        return (x * lax.rsqrt(ms + RMS_EPS) * g_ref[...]).astype(BF16)

    hn = normed(x_ref[...])
    hext = jnp.concatenate([normed(xp_ref[...]), hn, normed(xn_ref[...])], axis=0)
    row = lax.broadcasted_iota(jnp.int32, (tm + 2 * halo, hw), 0)
    outside = jnp.logical_or(jnp.logical_and(row < halo, pos == 0),
                             jnp.logical_and(row >= tm + halo, pos == tiles_per_seq - 1))
    hy_all = jnp.dot(hext, w_ref[:, :3 * hw], preferred_element_type=F32)
    rest = jnp.dot(hn, w_ref[:, 3 * hw:], preferred_element_type=F32)

    nc = u_ref.shape[1]
    for s in range(u_ref.shape[0]):
        cols = slice(s * hw, (s + 1) * hw)
        hy = jnp.where(outside, 0.0, hy_all[:, cols])
        up = pltpu.roll(hy, 1, 0)[halo:halo + tm]
        dn = pltpu.roll(hy, tm + 2 * halo - 1, 0)[halo:halo + tm]
        cw = cw_ref[:, cols]
        res = (up * cw[0:1] + hy[halo:halo + tm] * cw[1:2] + dn * cw[2:3]).astype(BF16)
        pk = pltpu.bitcast(res, U32)
        for c in range(nc):
            u_ref[s, c] = pk[:, c * LANES:(c + 1) * LANES]

    cos = cos_ref[...]
    sin = sin_ref[...]
    lane = lax.broadcasted_iota(jnp.int32, cos.shape, 1)
    first = (lane % HEAD_DIM) < (HEAD_DIM // 2)
    grp = grp_ref[...]
    base = 0
    for o_ref, gain_ref in ((q_ref, qg_ref), (k_ref, kg_ref)):
        for h in range(qk_w // LANES):
            x = rest[:, base + h * LANES:base + (h + 1) * LANES]
            sq_hi, sq_lo = _split_bf16(x * x)
            ss = (jnp.dot(sq_hi, grp, preferred_element_type=F32)
                  + jnp.dot(sq_lo, grp, preferred_element_type=F32))
            xn = x * lax.rsqrt(ss * (1.0 / HEAD_DIM) + RMS_EPS) * gain_ref[...]
            partner = jnp.where(first, pltpu.roll(xn, LANES - HEAD_DIM // 2, 1),
                                pltpu.roll(xn, HEAD_DIM // 2, 1))
            o_ref[:, h * LANES:(h + 1) * LANES] = (xn * cos + partner * sin).astype(o_ref.dtype)
        base += qk_w
    v_ref[...] = rest[:, base:base + v_ref.shape[1]].astype(v_ref.dtype)
    gate_ref[...] = rest[:, base + v_ref.shape[1]:].astype(gate_ref.dtype)


def _inproj(x2, L, g, w, conv_w, q_norm, k_norm, hw, qk_w, at_w):
    T, D = x2.shape
    tm = min(512, L)
    halo = BF16_SUBLANES
    hb = tm // halo
    nhb = T // halo
    nc = hw // LANES
    gw = w.shape[1] - 3 * hw - 2 * qk_w - at_w
    cos_t, sin_t = _rope_tables(L)
    qg = jnp.tile(q_norm.astype(F32), LANES // HEAD_DIM)[None, :] * F32(HEAD_DIM ** -0.5 * math.log2(math.e))
    kg = jnp.tile(k_norm.astype(F32), LANES // HEAD_DIM)[None, :]
    lanes = np.arange(LANES)
    grp = jnp.asarray(lanes[:, None] // HEAD_DIM == lanes[None, :] // HEAD_DIM, BF16)
    tps = L // tm
    const = lambda a: pl.BlockSpec(a.shape, lambda i: (0,) * a.ndim)
    rows = lambda width: pl.BlockSpec((tm, width), lambda i: (i, 0))
    tab = pl.BlockSpec((tm, LANES), lambda i: (i % tps, 0))
    gr = g.reshape(1, D)
    return pl.pallas_call(
        functools.partial(_inproj_kernel, hw=hw, qk_w=qk_w, tiles_per_seq=tps),
        grid=(T // tm,),
        in_specs=[rows(D),
                  pl.BlockSpec((halo, D), lambda i: (jnp.maximum(i * hb - 1, 0), 0)),
                  pl.BlockSpec((halo, D), lambda i: (jnp.minimum((i + 1) * hb, nhb - 1), 0)),
                  const(gr), const(w), const(conv_w), tab, tab, const(qg), const(kg), const(grp)],
        out_specs=[pl.BlockSpec((3, nc, tm // 2, LANES), lambda i: (0, 0, i, 0)),
                   rows(qk_w), rows(qk_w), rows(at_w), rows(gw)],
        out_shape=[jax.ShapeDtypeStruct((3, nc, T // 2, LANES), U32),
                   jax.ShapeDtypeStruct((T, qk_w), BF16), jax.ShapeDtypeStruct((T, qk_w), BF16),
                   jax.ShapeDtypeStruct((T, at_w), BF16), jax.ShapeDtypeStruct((T, gw), BF16)],
        compiler_params=_cparams(("parallel",)),
        name="inproj",
    )(x2, x2, x2, gr, w, conv_w, cos_t, sin_t, qg, kg, grp)


def _rope_tables(L):
    half = HEAD_DIM // 2
    inv = 1.0 / (ROPE_THETA ** (np.arange(0, HEAD_DIM, 2, dtype=np.float64) / HEAD_DIM))
    ang = np.arange(L, dtype=np.float64)[:, None] * inv[None, :]
    cos, sin = np.cos(ang), np.sin(ang)
    cos_t = np.tile(cos, (1, LANES // half))
    sin_t = np.tile(np.concatenate([-sin, sin], axis=1), (1, LANES // HEAD_DIM))
    return jnp.asarray(cos_t, F32), jnp.asarray(sin_t, F32)


def _split_bf16(x):
    hi = x.astype(BF16)
    return hi, (x - hi.astype(F32)).astype(BF16)


def _dot3(a, b):
    ah, al = _split_bf16(a)
    bh, bl = _split_bf16(b)
    d = lambda x, y: jnp.dot(x, y, preferred_element_type=F32)
    return d(ah, bh) + (d(ah, bl) + d(al, bh))


def _filter_kernel(zz_ref, w1_ref, b1_ref, f1_ref, w2_ref, b2_ref, f2_ref, w3_ref, dl_ref, o_ref):
    zz = zz_ref[...]
    h = _dot3(zz, w1_ref[...])
    h = jnp.sin(f1_ref[...] * (h + b1_ref[...]))
    h = _dot3(h, w2_ref[...])
    h = jnp.sin(f2_ref[...] * (h + b2_ref[...]))
    hf = _dot3(h, w3_ref[0])
    t = zz[:, 0:1]
    mask = zz[:, _MASK_COL:_MASK_COL + 1]
    kern = hf * (jnp.exp(-t * dl_ref[...]) * mask)
    for c in range(o_ref.shape[0]):
        o_ref[c] = kern[:, c * LANES:(c + 1) * LANES]


_MASK_COL = 2 * FILTER_BANDS + 1


def _filter_positions(L):
    bands = FILTER_BANDS
    t = np.linspace(0.0, 1.0, L, dtype=np.float64)[:, None]
    w = 2.0 * math.pi * np.arange(L, dtype=np.float64)[:, None] / L
    f = np.linspace(1e-4, bands - 1, bands, dtype=np.float64)[None, :]
    z = np.concatenate([t, np.cos(f * w), -np.sin(f * w)], axis=-1)
    zz = np.zeros((2 * L, LANES), np.float64)
    zz[:L, :z.shape[1]] = z
    zz[L + 1:, :z.shape[1]] = z[1:][::-1]
    zz[:, _MASK_COL] = 1.0
    zz[L, :] = 0.0
    return zz.astype(np.float32)


def _hyena_time_filters(L, w_f1, b_f1, freq1, w_f2, b_f2, freq2, w_f3, hw):
    emb, hid = w_f1.shape
    zz = jnp.asarray(_filter_positions(L))
    w1 = jnp.zeros((LANES, hid), F32).at[:emb].set(w_f1)
    w3 = w_f3.reshape(hid, 2, 2, hw).transpose(2, 0, 1, 3).reshape(2, hid, 2 * hw)
    max_decay = math.log(HYENA_TARGET) / FAST_DECAY_PCT
    min_decay = math.log(HYENA_TARGET) / SLOW_DECAY_PCT
    deltas = np.abs(np.linspace(min_decay, max_decay, hw, dtype=np.float64))
    dl = jnp.asarray(np.tile(deltas, 2)[None, :], F32)
    tr = min(512, L)
    per_dir = L // tr
    row = lambda a: a.reshape(1, -1)
    return pl.pallas_call(
        _filter_kernel,
        grid=(2 * L // tr,),
        in_specs=[pl.BlockSpec((tr, LANES), lambda i: (i, 0)),
                  pl.BlockSpec((LANES, hid), lambda i: (0, 0)),
                  pl.BlockSpec((1, hid), lambda i: (0, 0)),
                  pl.BlockSpec((1, hid), lambda i: (0, 0)),
                  pl.BlockSpec((hid, hid), lambda i: (0, 0)),
                  pl.BlockSpec((1, hid), lambda i: (0, 0)),
                  pl.BlockSpec((1, hid), lambda i: (0, 0)),
                  pl.BlockSpec((1, hid, 2 * hw), lambda i: (i // per_dir, 0, 0)),
                  pl.BlockSpec((1, 2 * hw), lambda i: (0, 0))],
        out_specs=pl.BlockSpec((2 * hw // LANES, tr, LANES), lambda i: (0, i, 0)),
        out_shape=jax.ShapeDtypeStruct((2 * hw // LANES, 2 * L, LANES), F32),
        compiler_params=_cparams(("parallel",)),
        name="hyena_filter",
    )(zz, w1, row(b_f1), row(freq1), w_f2, row(b_f2), row(freq2), w3, dl)


def _dft_matrices(L):
    N = 2 * L
    N1 = DFT_N1
    N2 = N // N1
    h = N2 // 2
    ang = 2.0 * np.pi * np.outer(np.arange(N2), np.arange(N2)) / N2
    c, s = np.cos(ang), np.sin(ang)
    m1 = np.zeros((N2, 2, N2))
    m1[:, 0, :h], m1[:, 0, h:] = c[:, :h], s[:, :h]
    m1[:, 1, :h], m1[:, 1, h:] = -s[:, :h], c[:, :h]
    m1 = m1.reshape(2 * N2, N2)
    m1f = np.stack([c, -s], axis=1).reshape(2 * N2, N2)
    m3 = np.zeros((2, h, N2, 2))
    m3[0, :, :, 0], m3[0, :, :, 1] = c[:h], -s[:h]
    m3[1, :, :, 0], m3[1, :, :, 1] = s[:h], c[:h]
    m3 = (m3 / N).reshape(N2, 2 * N2)
    def mid(order, transposed):
        f = np.arange(2 * N1)[:, None]
        d = np.asarray(order)[None, :]
        a = 2.0 * np.pi * ((d % N1) * (f % N1) % N1) / N1
        same, up = (f // N1 == d // N1), (f // N1 < d // N1)
        pc = np.where(same, 1.0, 0.0)
        ps = np.where(same, 0.0, np.where(up, 1.0, -1.0))
        u = pc * np.cos(a) + ps * np.sin(a)
        v = ps * np.cos(a) - pc * np.sin(a)
        b = 2.0 * np.pi * (np.arange(N2)[:, None] * (d % N1) % N) / N
        if transposed:
            u, v = u.T, v.T
            cb, sb = jnp.asarray(np.cos(b), F32)[:, :, None], jnp.asarray(np.sin(b), F32)[:, :, None]
        else:
            cb, sb = jnp.asarray(np.cos(b), F32)[:, None, :], jnp.asarray(np.sin(b), F32)[:, None, :]
        return jnp.asarray(u, F32)[None] * cb + jnp.asarray(v, F32)[None] * sb

    packed = _mid_row_order(DFT_PAIRS, 2)
    eye2 = np.eye(2)
    return dict(m1x=jnp.asarray(np.kron(m1, eye2), BF16),
                m3x=jnp.asarray(np.kron(m3, eye2), BF16),
                m1f=jnp.asarray(m1f, F32),
                m2f=mid(_mid_row_order(DFT_FILTER_STEP, 1), False),
                m2p=mid(packed, False).astype(BF16),
                m2tp=mid(packed, True).astype(BF16))


def _dft_filter_a_kernel(m_ref, k_ref, o_ref, *, step, n2):
    j0 = pl.program_id(1) * step
    rows = m_ref.shape[0]
    for jj in range(step):
        rhs = jnp.concatenate([k_ref[c, pl.ds(j0 + jj, n2, stride=DFT_N1), :] for c in range(k_ref.shape[0])],
                              axis=1)
        res = _dot3(m_ref[...], rhs)
        for c in range(o_ref.shape[0]):
            o_ref[c, 0, pl.ds(jj, rows, stride=step), :] = res[:, c * LANES:(c + 1) * LANES]


def _dft_filter_a(m1f, kern):
    NB, N, _ = kern.shape
    n2 = N // DFT_N1
    step, ncb = DFT_FILTER_STEP, DFT_LANE_BLOCKS
    rows = m1f.shape[0]
    return pl.pallas_call(
        functools.partial(_dft_filter_a_kernel, step=step, n2=n2),
        grid=(NB // ncb, DFT_N1 // step),
        in_specs=[pl.BlockSpec(m1f.shape, lambda h, j: (0, 0)),
                  pl.BlockSpec((ncb, N, LANES), lambda h, j: (h, 0, 0))],
        out_specs=pl.BlockSpec((ncb, 1, rows * step, LANES), lambda h, j: (h, j, 0, 0)),
        out_shape=jax.ShapeDtypeStruct((NB, DFT_N1 // step, rows * step, LANES), F32),
        compiler_params=_cparams(("parallel", "arbitrary")),
        name="dft_filter_a",
    )(m1f, kern)


def _dft_filt_kernel(a_ref, m_ref, o_ref):
    rows = m_ref.shape[1]
    a = jnp.concatenate([a_ref[c, :, 0].reshape(rows, LANES) for c in range(a_ref.shape[0])], axis=1)
    o_ref[0] = _dot3(m_ref[0], a)


def _dft_filt(a, m2):
    NB, G, N2, W, _ = a.shape
    R = m2.shape[1]
    return pl.pallas_call(
        _dft_filt_kernel,
        grid=(N2,),
        in_specs=[pl.BlockSpec((NB, G, 1, W, LANES), lambda k: (0, 0, k, 0, 0)),
                  pl.BlockSpec((1, R, R), lambda k: (k, 0, 0))],
        out_specs=pl.BlockSpec((1, R, NB * LANES), lambda k: (k, 0, 0)),
        out_shape=jax.ShapeDtypeStruct((N2, R, NB * LANES), F32),
        compiler_params=_cparams(("parallel",)),
        name="dft_filter_mid",
    )(a, m2)


def _packed_rows(ref, lead, start, size, stride):
    cols = [jnp.concatenate([ref[(c,) + l + (pl.ds(start, size, stride=stride), slice(None))] for l in lead], axis=0)
            for c in range(ref.shape[0])]
    return pltpu.bitcast(jnp.concatenate(cols, axis=1), BF16)


def _dft_a_pk_kernel(m_ref, z_ref, o_ref, *, tm, nh, npairs):
    j0 = pl.program_id(2) * tm
    rows = m_ref.shape[0] // 2
    for j in range(tm):
        rhs = _packed_rows(z_ref, [(0,), (1,)], j0 + j, nh, npairs)
        res = jnp.dot(m_ref[...], rhs, preferred_element_type=F32)
        pk = pltpu.bitcast(res.astype(BF16), U32)
        for c in range(o_ref.shape[0]):
            o_ref[c, 0, 0, pl.ds(j, rows, stride=tm), :] = pk[:, c * LANES:(c + 1) * LANES]


def _dft_a_pk(m1x, z, slot):
    _, NC, B, L2, _ = z.shape
    npairs = DFT_N1 // 2
    nh = L2 // npairs
    tm, ncb = DFT_PAIRS, DFT_LANE_BLOCKS
    rows = m1x.shape[0] // 2
    P = B // 2
    return pl.pallas_call(
        functools.partial(_dft_a_pk_kernel, tm=tm, nh=nh, npairs=npairs),
        grid=(P, NC // ncb, npairs // tm),
        in_specs=[pl.BlockSpec(m1x.shape, lambda p, h, j: (0, 0)),
                  pl.BlockSpec((None, ncb, 2, L2, LANES), lambda p, h, j: (slot, h, p, 0, 0))],
        out_specs=pl.BlockSpec((ncb, 1, 1, rows * tm, LANES), lambda p, h, j: (h, p, j, 0, 0)),
        out_shape=jax.ShapeDtypeStruct((NC, P, npairs // tm, rows * tm, LANES), U32),
        compiler_params=_cparams(("parallel", "parallel", "arbitrary")),
        name="dft_a",
    )(m1x, z)


def _dft_mid_kernel(a_ref, m_ref, mt_ref, kf_ref, o_ref):
    n1 = m_ref.shape[1] // 2
    nc, npair = a_ref.shape[0], a_ref.shape[1]
    blk = a_ref.shape[2], a_ref.shape[4], LANES
    for kk in range(a_ref.shape[3]):
        m = m_ref[kk]
        mt = mt_ref[kk]
        kr = kf_ref[kk, :n1]
        ki = kf_ref[kk, n1:]
        for p in range(npair):
            cols = [a_ref[c, p, :, kk].reshape(n1, LANES) for c in range(nc)]
            a = pltpu.bitcast(jnp.concatenate(cols, axis=1), BF16)
            x = jnp.dot(m, a, preferred_element_type=F32)
            xr = x[:n1]
            xi = x[n1:]
            y = jnp.concatenate([xr * kr - xi * ki, xr * ki + xi * kr], axis=0).astype(BF16)
            pk = pltpu.bitcast(jnp.dot(mt, y, preferred_element_type=F32).astype(BF16), U32)
            for c in range(nc):
                o_ref[c, p, :, kk] = pk[:, c * LANES:(c + 1) * LANES].reshape(blk)


def _dft_mid(a, m2, m2t, kf, order):
    NC, P, G, N2, W, _ = a.shape
    R = m2.shape[1]
    C = NC * LANES
    kb = DFT_MID_K2
    blk = pl.BlockSpec((NC, P, G, kb, W, LANES), lambda k: (0, 0, 0, k, 0, 0))
    return pl.pallas_call(
        _dft_mid_kernel,
        grid=(N2 // kb,),
        in_specs=[blk,
                  pl.BlockSpec((kb, R, R), lambda k: (k, 0, 0)),
                  pl.BlockSpec((kb, R, R), lambda k: (k, 0, 0)),
                  pl.BlockSpec((kb, R, C), lambda k: (k, 0, order))],
        out_specs=blk,
        out_shape=jax.ShapeDtypeStruct(a.shape, U32),
        compiler_params=_cparams(("parallel",)),
        name="dft_mid",
    )(a, m2, m2t, kf)


def _dft_c_pk_kernel(m_ref, b_ref, z_ref, g_ref, bias_ref, o_ref, *, tm, nh, npairs):
    j0 = pl.program_id(2) * tm
    rows = b_ref.shape[3] // tm
    bias = bias_ref[...]
    both = [(0,), (1,)]
    for j in range(tm):
        rhs = _packed_rows(b_ref, [(0, 0)], j, rows, tm)
        y = jnp.dot(m_ref[...], rhs, preferred_element_type=F32)
        z = _packed_rows(z_ref, both, j0 + j, nh, npairs).astype(F32)
        g = _packed_rows(g_ref, both, j0 + j, nh, npairs).astype(F32)
        pk = pltpu.bitcast((g * (y + z * bias)).astype(BF16), U32)
        for c in range(o_ref.shape[0]):
            for b in range(2):
                o_ref[c, b, pl.ds(j0 + j, nh, stride=npairs), :] = pk[b * nh:(b + 1) * nh, c * LANES:(c + 1) * LANES]


def _dft_c_pk(m3x, b, z, zslot, g, gslot, bias):
    NC, P, G, W, _ = b.shape
    L2 = z.shape[3]
    npairs = DFT_N1 // 2
    nh = L2 // npairs
    tm, ncb = DFT_PAIRS, DFT_LANE_BLOCKS
    seq = lambda slot: pl.BlockSpec((None, ncb, 2, L2, LANES), lambda p, h, j: (slot, h, p, 0, 0))
    return pl.pallas_call(
        functools.partial(_dft_c_pk_kernel, tm=tm, nh=nh, npairs=npairs),
        grid=(P, NC // ncb, G),
        in_specs=[pl.BlockSpec(m3x.shape, lambda p, h, j: (0, 0)),
                  pl.BlockSpec((ncb, 1, 1, W, LANES), lambda p, h, j: (h, p, j, 0, 0)),
                  seq(zslot), seq(gslot),
                  pl.BlockSpec((1, ncb * LANES), lambda p, h, j: (0, h))],
        out_specs=seq(0),
        out_shape=jax.ShapeDtypeStruct((1, NC, 2 * P, L2, LANES), U32),
        compiler_params=_cparams(("parallel", "parallel", "arbitrary")),
        name="dft_c",
    )(m3x, b, z, g, bias)


def _mid_row_order(step, per):
    N1 = DFT_N1
    idx = []
    for mt in range(N1 // per // step):
        for ri in range(2):
            for jj in range(step):
                for sub in range(per):
                    idx.append(ri * N1 + per * (mt * step + jj) + sub)
    return np.asarray(idx, np.int32)


def _hyena(u3, kern, hy_bias, mats):
    _, NC, B, L2, _ = u3.shape
    C = NC * LANES
    N1 = DFT_N1
    N2 = 4 * L2 // N1
    af = _dft_filter_a(mats["m1f"], kern)
    NB, G, W, _ = af.shape
    kf = _dft_filt(af.reshape(NB, G, N2, W // N2, LANES), mats["m2f"])
    bias = hy_bias.astype(F32)

    def long_conv(z, zslot, gslot, order):
        a = _dft_a_pk(mats["m1x"], z, zslot)
        NCa, P, G, W, _ = a.shape
        b = _dft_mid(a.reshape(NCa, P, G, N2, W // N2, LANES), mats["m2p"], mats["m2tp"], kf, order)
        return _dft_c_pk(mats["m3x"], b.reshape(a.shape), z, zslot, u3, gslot, bias[order][None, :])

    z = long_conv(u3, 2, 0, 0)
    return long_conv(z, 0, 1, 1)


SHIFT_LIMIT = 50.0


def _attn_kernel(lam_ref, q_ref, k_ref, v_ref, g_ref, o_ref, kmax_ref, *, post):
    k = k_ref[0]
    v = v_ref[0]

    @pl.when(pl.program_id(2) == 0)
    def _():
        kmax_ref[...] = jnp.max(jnp.abs(k.astype(F32)), axis=0, keepdims=True)

    q = q_ref[0]
    lane = lax.broadcasted_iota(jnp.int32, q.shape, 1)
    lo = lane < HEAD_DIM
    zero = jnp.zeros_like(q)
    reach = jnp.abs(q.astype(F32)) * kmax_ref[...]
    shifts = [jnp.sum(jnp.where(lo, reach, 0.0), axis=-1, keepdims=True),
              jnp.sum(jnp.where(lo, 0.0, reach), axis=-1, keepdims=True)]
    worst = jnp.max(jnp.maximum(shifts[0], shifts[1]))

    def run(row_max):
        outs = []
        for c in range(2):
            qc = jnp.where(lo, q, zero) if c == 0 else jnp.where(lo, zero, q)
            s = lax.dot_general(qc, k, (((1,), (1,)), ((), ())), preferred_element_type=F32)
            m = jnp.max(s, axis=-1, keepdims=True) if row_max else shifts[c]
            p = jnp.exp2(s - m)
            l = jnp.sum(p, axis=-1, keepdims=True)
            outs.append(jnp.dot(p.astype(BF16), v, preferred_element_type=F32) / l)
        a = outs[0] - lam_ref[0] * outs[1]
        ms = jnp.mean(a * a, axis=-1, keepdims=True)
        o_ref[0] = (a * lax.rsqrt(ms + RMS_EPS) * g_ref[...] * post).astype(o_ref.dtype)

    @pl.when(worst <= SHIFT_LIMIT)
    def _():
        run(False)

    @pl.when(jnp.logical_not(worst <= SHIFT_LIMIT))
    def _():
        run(True)


def _attention(qn, kn, proj3, vcol, lam, subln_g, post):
    B, L, W = qn.shape
    tq = min(512, L)
    return pl.pallas_call(
        functools.partial(_attn_kernel, post=post),
        grid=(B, W // V_DIM, L // tq),
        in_specs=[pl.BlockSpec(memory_space=pltpu.SMEM),
                  pl.BlockSpec((1, tq, V_DIM), lambda b, h, i: (b, i, h)),
                  pl.BlockSpec((1, L, V_DIM), lambda b, h, i: (b, 0, h)),
                  pl.BlockSpec((1, L, V_DIM), lambda b, h, i: (b, 0, vcol + h)),
                  pl.BlockSpec((1, V_DIM), lambda b, h, i: (0, 0))],
        out_specs=pl.BlockSpec((1, tq, V_DIM), lambda b, h, i: (b, i, h)),
        out_shape=jax.ShapeDtypeStruct((B, L, W), BF16),
        scratch_shapes=[pltpu.VMEM((1, V_DIM), F32)],
        compiler_params=_cparams(("parallel", "parallel", "arbitrary")),
        name="diff_attention",
    )(lam.reshape(1).astype(F32), qn, kn, proj3, subln_g.astype(F32).reshape(1, V_DIM))


def _merge_kernel(x_ref, yh_ref, ya_ref, gh_ref, ga_ref, who_ref, wao_ref, wout_ref, g_ref, *rest,
                  with_router):
    if with_router:
        wr_ref, xo_ref, hn_ref, lg_ref = rest
    else:
        xo_ref, hn_ref = rest
    yh = jnp.concatenate([pltpu.bitcast(yh_ref[c], BF16) for c in range(yh_ref.shape[0])], axis=1)
    th = jnp.dot(yh, who_ref[...], preferred_element_type=F32)
    ta = jnp.dot(ya_ref[...], wao_ref[...], preferred_element_type=F32)
    merged = (jax.nn.sigmoid(gh_ref[...].astype(F32)) * th
              + jax.nn.sigmoid(ga_ref[...].astype(F32)) * ta)
    xn = x_ref[...] + jnp.dot(merged.astype(BF16), wout_ref[...], preferred_element_type=F32)
    xo_ref[...] = xn
    ms = jnp.mean(xn * xn, axis=-1, keepdims=True)
    h = xn * lax.rsqrt(ms + RMS_EPS) * g_ref[...]
    hn_ref[...] = h.astype(hn_ref.dtype)
    if with_router:
        lane = lax.broadcasted_iota(jnp.int32, lg_ref.shape, 1)
        lg = jnp.zeros(lg_ref.shape, F32)
        for e in range(wr_ref.shape[0]):
            lg = jnp.where(lane == e, jnp.sum(h * wr_ref[e:e + 1, :], axis=-1, keepdims=True), lg)
        lg_ref[...] = lg


def _merge(x2, yh, ya, proj, gcol, who, wao, wout, g_ffn, w_router):
    T, D = x2.shape
    W = ya.shape[1]
    tm = min(512, T)
    with_router = w_router is not None
    row = lambda w: pl.BlockSpec((tm, w), lambda i: (i, 0))
    full = lambda a: pl.BlockSpec(a.shape, lambda i: (0, 0))
    in_specs = [row(D), pl.BlockSpec((yh.shape[0], tm // 2, LANES), lambda i: (0, i, 0)), row(W),
                pl.BlockSpec((tm, D), lambda i: (i, gcol)),
                pl.BlockSpec((tm, D), lambda i: (i, gcol + 1)),
                full(who), full(wao), full(wout), pl.BlockSpec((1, D), lambda i: (0, 0))]
    args = [x2, yh, ya, proj, proj, who, wao, wout, g_ffn.astype(F32).reshape(1, D)]
    out_specs = [row(D), row(D)]
    out_shape = [jax.ShapeDtypeStruct((T, D), F32),
                 jax.ShapeDtypeStruct((T, D), F32 if with_router else BF16)]
    if with_router:
        wr = w_router.astype(F32).T
        in_specs.append(full(wr))
        args.append(wr)
        out_specs.append(row(LANES))
        out_shape.append(jax.ShapeDtypeStruct((T, LANES), F32))
    return pl.pallas_call(
        functools.partial(_merge_kernel, with_router=with_router),
        grid=(T // tm,),
        in_specs=in_specs,
        out_specs=out_specs,
        out_shape=out_shape,
        compiler_params=_cparams(("parallel",)),
        name="merge_outproj",
    )(*args)


FF_CHUNK = 2 * LANES


def _swiglu(h, wg_ref, wu_ref, wd_ref):
    F = wg_ref.shape[-1]
    acc = None
    for lo in range(0, F, FF_CHUNK):
        sl = slice(lo, min(lo + FF_CHUNK, F))
        g = jnp.dot(h, wg_ref[:, sl], preferred_element_type=F32)
        u = jnp.dot(h, wu_ref[:, sl], preferred_element_type=F32)
        a = (g * jax.nn.sigmoid(g) * u).astype(BF16)
        d = jnp.dot(a, wd_ref[sl, :], preferred_element_type=F32)
        acc = d if acc is None else acc + d
    return acc


def _ffn_kernel(h_ref, x_ref, wg_ref, wu_ref, wd_ref, o_ref):
    o_ref[...] = x_ref[...] + _swiglu(h_ref[...], wg_ref, wu_ref, wd_ref)


def _dense_ffn(hn, x2, wg, wu, wd, layer):
    T, D = x2.shape
    F = wg.shape[2]
    tm = min(512, T)
    row = pl.BlockSpec((tm, D), lambda i: (i, 0))
    return pl.pallas_call(
        _ffn_kernel,
        grid=(T // tm,),
        in_specs=[row, row,
                  pl.BlockSpec((None, D, F), lambda i: (layer, 0, 0)),
                  pl.BlockSpec((None, D, F), lambda i: (layer, 0, 0)),
                  pl.BlockSpec((None, F, D), lambda i: (layer, 0, 0))],
        out_specs=row,
        out_shape=jax.ShapeDtypeStruct((T, D), F32),
        compiler_params=_cparams(("parallel",)),
        name="dense_ffn",
    )(hn, x2, wg, wu, wd)


def _row_copy(src_hbm, row, dst, r, sem):
    return pltpu.make_async_copy(src_hbm.at[pl.ds(row, 1)], dst.at[pl.ds(r, 1)], sem)


def _rows_done(src_hbm, dst, sem):
    return pltpu.make_async_copy(src_hbm.at[pl.ds(0, dst.shape[0])], dst, sem)


def _expert_kernel(be_ref, tok_ref, hn_hbm, wg_ref, wu_ref, wd_ref, o_ref, xbuf, sem):
    i = pl.program_id(0)
    used = be_ref[pl.num_programs(0)]
    last = used - 1
    rows = xbuf.shape[1]
    slot = i % 2
    nxt = 1 - slot

    @pl.when(i < used)
    def _():
        @pl.when(i == 0)
        def _():
            def issue(r, carry):
                _row_copy(hn_hbm, tok_ref[r], xbuf.at[0], r, sem.at[0]).start()
                return carry

            lax.fori_loop(0, rows, issue, 0, unroll=8)

        _rows_done(hn_hbm, xbuf.at[slot], sem.at[slot]).wait()
        base = jnp.minimum(i + 1, last) * rows
        for r in range(rows):
            _row_copy(hn_hbm, tok_ref[base + r], xbuf.at[nxt], r, sem.at[nxt]).start()
        o_ref[...] = _swiglu(xbuf[slot].astype(BF16), wg_ref.at[0], wu_ref.at[0], wd_ref.at[0])

        @pl.when(i == last)
        def _():
            _rows_done(hn_hbm, xbuf.at[nxt], sem.at[nxt]).wait()

    @pl.when(i >= used)
    def _():
        o_ref[...] = jnp.zeros_like(o_ref)


def _experts(block_e, tok_of_slot, hn, wg, wu, wd, layer):
    T, D = hn.shape
    F = wg.shape[3]
    P = tok_of_slot.shape[0]
    rows = MOE_ROWS
    wspec = lambda shape: pl.BlockSpec((None, 1) + shape, lambda i, be, tok: (layer, be[i], 0, 0))
    return pl.pallas_call(
        _expert_kernel,
        grid_spec=pltpu.PrefetchScalarGridSpec(
            num_scalar_prefetch=2,
            grid=(P // rows,),
            in_specs=[pl.BlockSpec(memory_space=pl.ANY),
                      wspec((D, F)), wspec((D, F)), wspec((F, D))],
            out_specs=pl.BlockSpec((rows, D), lambda i, be, tok: (i, 0)),
            scratch_shapes=[pltpu.VMEM((2, rows, D), F32), pltpu.SemaphoreType.DMA((2,))]),
        out_shape=jax.ShapeDtypeStruct((P, D), F32),
        compiler_params=_cparams(("arbitrary",)),
        name="moe_experts",
    )(block_e, tok_of_slot, hn, wg, wu, wd)


def _combine_kernel(dest_ref, yp_hbm, x_ref, w_ref, o_ref, buf0, buf1, sem):
    i = pl.program_id(0)
    rows = buf0.shape[1]

    def gather(tile, slot):
        base = tile * rows * TOP_K

        for r in range(rows):
            _row_copy(yp_hbm, dest_ref[base + TOP_K * r], buf0.at[slot], r, sem.at[slot]).start(priority=0)
            _row_copy(yp_hbm, dest_ref[base + TOP_K * r + 1], buf1.at[slot], r, sem.at[slot]).start(priority=1)

    @pl.when(i == 0)
    def _():
        gather(0, 0)

    @pl.when(i + 1 < pl.num_programs(0))
    def _():
        gather(i + 1, (i + 1) % 2)

    slot = i % 2
    _rows_done(yp_hbm, buf0.at[slot], sem.at[slot]).wait()
    _rows_done(yp_hbm, buf1.at[slot], sem.at[slot]).wait()
    w = w_ref[...]
    o_ref[...] = x_ref[...] + (buf0[slot] * w[:, 0:1] + buf1[slot] * w[:, 1:2])


def _combine(dest, yp, x2, top_w):
    T, D = x2.shape
    tm = min(256, T)
    return pl.pallas_call(
        _combine_kernel,
        grid_spec=pltpu.PrefetchScalarGridSpec(
            num_scalar_prefetch=1,
            grid=(T // tm,),
            in_specs=[pl.BlockSpec(memory_space=pl.ANY),
                      pl.BlockSpec((tm, D), lambda i, d: (i, 0)),
                      pl.BlockSpec((tm, TOP_K), lambda i, d: (i, 0))],
            out_specs=pl.BlockSpec((tm, D), lambda i, d: (i, 0)),
            scratch_shapes=[pltpu.VMEM((2, tm, D), F32), pltpu.VMEM((2, tm, D), F32),
                            pltpu.SemaphoreType.DMA((2,))]),
        out_shape=jax.ShapeDtypeStruct((T, D), F32),
        compiler_params=_cparams(("arbitrary",)),
        name="moe_combine",
    )(dest, yp, x2, top_w)


def _route(logits, n_experts, rows):
    T = logits.shape[0]
    top_val, top_idx = lax.top_k(logits[:, :n_experts], TOP_K)
    top_w = jax.nn.softmax(top_val, axis=-1)
    A = T * TOP_K
    e_flat = top_idx.reshape(A).astype(jnp.int32)
    onehot = (e_flat[:, None] == jnp.arange(n_experts, dtype=jnp.int32)[None, :]).astype(jnp.int32)
    csum = jnp.cumsum(onehot, axis=0)
    counts = csum[-1]
    rank = jnp.sum((csum - onehot) * onehot, axis=1)
    padded = (counts + rows - 1) // rows * rows
    pad_end = jnp.cumsum(padded)
    pad_start = pad_end - padded
    dest = pad_start[e_flat] + rank
    n_blocks = -(-A // rows) + n_experts
    P = n_blocks * rows
    order = jnp.argsort(e_flat)
    starts = jnp.cumsum(counts) - counts
    slot = jnp.arange(P, dtype=jnp.int32)
    slot_e = jnp.minimum(jnp.sum((slot[:, None] >= pad_end[None, :]).astype(jnp.int32), axis=1),
                         n_experts - 1)
    within = slot - pad_start[slot_e]
    valid = within < counts[slot_e]
    src = jnp.clip(starts[slot_e] + within, 0, A - 1)
    tok_of_slot = jnp.where(valid, order[src] // TOP_K, 0).astype(jnp.int32)
    n_used = (pad_end[-1] // rows).astype(jnp.int32)
    block_e = jnp.concatenate([slot_e[::rows].astype(jnp.int32), n_used[None]])
    return top_w.astype(F32), dest.astype(jnp.int32), tok_of_slot, block_e


def kernel(x, norm_mix, w_in, conv_w, w_f1, b_f1, freq1, w_f2, b_f2, freq2, w_f3, hy_bias, q_norm, k_norm, lam_q1, lam_k1, lam_q2, lam_k2, subln_g, w_hy_o, w_at_o, w_out, norm_ffn, w_dense_gate, w_dense_up, w_dense_down, w_router, w_moe_gate, w_moe_up, w_moe_down):
    B, L, D = x.shape
    T = B * L
    depth = w_in.shape[0]
    hw = w_hy_o.shape[1]
    qk_w = N_HEADS * 2 * HEAD_DIM
    at_w = N_HEADS * V_DIM
    mats = _dft_matrices(L)
    moe_w = tuple(w.astype(BF16) for w in (w_moe_gate, w_moe_up, w_moe_down))
    dense_w = tuple(w.astype(BF16) for w in (w_dense_gate, w_dense_up, w_dense_down))

    x2 = x.reshape(T, D)
    for layer in range(depth):
        lambda_init = 0.8 - 0.6 * math.exp(-0.3 * layer)
        u3, qn, kn, v, gates = _inproj(x2, L, norm_mix[layer].astype(F32), w_in[layer].astype(BF16),
                                       conv_w[layer].astype(F32), q_norm[layer], k_norm[layer], hw, qk_w, at_w)
        kern = _hyena_time_filters(L, w_f1[layer], b_f1[layer], freq1[layer], w_f2[layer], b_f2[layer],
                                   freq2[layer], w_f3[layer], hw)
        y_h = _hyena(u3.reshape(3, hw // LANES, B, L // 2, LANES), kern, hy_bias[layer], mats)
        lam = (jnp.exp(jnp.sum(lam_q1[layer].astype(F32) * lam_k1[layer].astype(F32)))
               - jnp.exp(jnp.sum(lam_q2[layer].astype(F32) * lam_k2[layer].astype(F32))) + lambda_init)
        y_a = _attention(qn.reshape(B, L, qk_w), kn.reshape(B, L, qk_w), v.reshape(B, L, at_w), 0, lam,
                         subln_g[layer], 1.0 - lambda_init)
        i = layer // 2
        moe = layer % 2 == 1
        outs = _merge(x2, y_h.reshape(hw // LANES, T // 2, LANES), y_a.reshape(T, at_w), gates, 0,
                      w_hy_o[layer].astype(BF16), w_at_o[layer].astype(BF16), w_out[layer].astype(BF16),
                      norm_ffn[layer], w_router[i] if moe else None)
        if not moe:
            x2, hn = outs
            x2 = _dense_ffn(hn, x2, *dense_w, i)
        else:
            x2, hn, logits = outs
            n_experts = w_router.shape[2]
            top_w, dest, tok_of_slot, block_e = _route(logits, n_experts, MOE_ROWS)
            yp = _experts(block_e, tok_of_slot, hn, *moe_w, i)
            x2 = _combine(dest, yp, x2, top_w)
    return x2.reshape(B, L, D)
```
